```python
import math
import jax, jax.numpy as jnp
from jax import lax
import numpy as np

D_MODEL = 4096
BATCH = 1
SEQ = 8192
DEPTH = 1

N_META = 16
CONV_CH = D_MODEL // 2
CONV_WIDTH = 31
GDN_HEAD_DIM = 128
GDN_HEADS = (D_MODEL - CONV_CH) // GDN_HEAD_DIM
GDN_WIDTH = GDN_HEADS * GDN_HEAD_DIM
MIX_WIDTH = CONV_CH + GDN_WIDTH
SHORT_CONV = 4
CHUNK = 64
N_EXPERTS = 64
TOP_K = 8
N_EXPERT_GROUPS = 8
TOPK_GROUPS = 4
EXPERT_FF = D_MODEL // 8
SHARED_FF = EXPERT_FF
ROUTED_SCALE = 2.5
DEEPNORM_ALPHA = (2.0 * DEPTH) ** 0.25
DEEPNORM_BETA = (8.0 * DEPTH) ** -0.25
LN_EPS = 1e-5
RMS_EPS = 1e-6
IN_SPLITS = (CONV_CH, CONV_CH, GDN_WIDTH, GDN_WIDTH, GDN_WIDTH, GDN_WIDTH, GDN_HEADS, GDN_HEADS)
IN_WIDTH = sum(IN_SPLITS)

kernel_name = 'hybrid_conformer_gdn_moe_deepnorm'


def layer_norm(x, g, b):
    xf = x.astype(jnp.float32)
    mu = jnp.mean(xf, -1, keepdims=True)
    var = jnp.mean(jnp.square(xf - mu), -1, keepdims=True)
    return ((xf - mu) * lax.rsqrt(var + LN_EPS) * g + b).astype(x.dtype)


def l2norm(x):
    return x * lax.rsqrt(jnp.sum(jnp.square(x), -1, keepdims=True) + RMS_EPS)


def causal_dwconv(x, w):
    k = w.shape[0]
    return lax.conv_general_dilated(
        x, w[:, None, :].astype(x.dtype), window_strides=(1,), padding=[(k - 1, 0)],
        dimension_numbers=('NWC', 'WIO', 'NWC'), feature_group_count=x.shape[-1])


def chunked_gated_delta_rule(q, k, v, g, beta):
    B, L, H, DK = q.shape
    DV = v.shape[-1]
    pad = (-L) % CHUNK

    def prep(t):
        t = jnp.pad(t, [(0, 0), (pad, 0)] + [(0, 0)] * (t.ndim - 2))
        t = t.reshape((B, -1, CHUNK) + t.shape[2:])
        return jnp.moveaxis(t, 3, 1)

    q, k, v, g, beta = prep(q), prep(k), prep(v), prep(g), prep(beta)
    gc = jnp.cumsum(g, axis=-1)
    idx = jnp.arange(CHUNK)
    incl = idx[:, None] >= idx[None, :]
    strict = idx[:, None] > idx[None, :]
    diff = gc[..., :, None] - gc[..., None, :]
    decay = jnp.where(incl, jnp.exp(jnp.where(incl, diff, 0.0)), 0.0)
    kb = k * beta[..., None]
    lmat = jnp.where(strict, jnp.einsum('bhncd,bhnsd->bhncs', kb, k) * decay, 0.0)
    rhs = jnp.concatenate([v * beta[..., None], kb * jnp.exp(gc)[..., None]], -1)
    sol = lax.linalg.triangular_solve(lmat, rhs, left_side=True, lower=True, unit_diagonal=True)
    u, w = sol[..., :DV], sol[..., DV:]
    attn = jnp.einsum('bhncd,bhnsd->bhncs', q, k) * decay
    g_last = gc[..., -1:]
    q_dec = q * jnp.exp(gc)[..., None]
    k_dec = k * jnp.exp(g_last - gc)[..., None]

    def step(S, xs):
        qd, kd, wc, uc, ac, gl = xs
        v_new = uc - jnp.einsum('bhcd,bhde->bhce', wc, S)
        o = jnp.einsum('bhcd,bhde->bhce', qd, S) + jnp.einsum('bhcs,bhse->bhce', ac, v_new)
        S = S * jnp.exp(gl)[..., None] + jnp.einsum('bhcd,bhce->bhde', kd, v_new)
        return S, o

    xs = tuple(jnp.moveaxis(t, 2, 0) for t in (q_dec, k_dec, w, u, attn, g_last))
    S0 = jnp.zeros((B, H, DK, DV), q.dtype)
    _, o = lax.scan(step, S0, xs)
    o = jnp.transpose(o, (1, 0, 3, 2, 4)).reshape(B, -1, H, DV)
    return o[:, pad:]


def swiglu(x, wg, wu, wd):
    return (jax.nn.silu(x @ wg) * (x @ wu)) @ wd


def moe_ffn(h, router_w, router_bias, w_gate, w_up, w_down, s_gate, s_up, s_down):
    B, L, D = h.shape
    x = h.reshape(B * L, D)
    T = x.shape[0]
    scores = jax.nn.sigmoid((x @ router_w).astype(jnp.float32))
    biased = scores + router_bias
    grouped = biased.reshape(T, N_EXPERT_GROUPS, N_EXPERTS // N_EXPERT_GROUPS)
    group_score = jnp.sum(lax.top_k(grouped, 2)[0], -1)
    _, gidx = lax.top_k(group_score, TOPK_GROUPS)
    gmask = jnp.sum(jax.nn.one_hot(gidx, N_EXPERT_GROUPS, dtype=jnp.float32), axis=1) > 0
    emask = jnp.repeat(gmask, N_EXPERTS // N_EXPERT_GROUPS, axis=1)
    _, eidx = lax.top_k(jnp.where(emask, biased, -jnp.inf), TOP_K)
    wsel = jnp.take_along_axis(scores, eidx, axis=1)
    wsel = wsel / jnp.sum(wsel, -1, keepdims=True) * ROUTED_SCALE
    gates = jnp.einsum('tk,tke->te', wsel,
                       jax.nn.one_hot(eidx, N_EXPERTS, dtype=jnp.float32)).astype(x.dtype)
    y = swiglu(x, s_gate, s_up, s_down)
    per = N_EXPERTS // N_EXPERT_GROUPS
    for grp in range(N_EXPERT_GROUPS):
        sl = slice(grp * per, (grp + 1) * per)
        hg = jnp.einsum('td,edf->etf', x, w_gate[sl])
        hu = jnp.einsum('td,edf->etf', x, w_up[sl])
        act = jax.nn.silu(hg) * hu * gates[:, sl].T[..., None]
        y = y + jnp.einsum('etf,efd->td', act, w_down[sl])
    return y.reshape(B, L, D)


def hybrid_layer(h, w_in, conv_w, conv_b, conv_ln_g, conv_ln_b, short_conv_w, a_log, dt_bias,
                 gdn_norm_g, w_out, ln1_g, ln1_b, router_w, router_bias, expert_w_gate,
                 expert_w_up, expert_w_down, shared_w_gate, shared_w_up, shared_w_down,
                 ln2_g, ln2_b):
    B, L, _ = h.shape
    proj = jnp.einsum('bld,de->ble', h, w_in)
    offs = [int(o) for o in np.cumsum(IN_SPLITS)[:-1]]
    a_val, a_gate, q, k, v, z, b_logit, a_logit = jnp.split(proj, offs, axis=-1)

    c = a_val * jax.nn.sigmoid(a_gate)
    c = causal_dwconv(c, conv_w) + conv_b
    y_conv = jax.nn.silu(layer_norm(c, conv_ln_g, conv_ln_b))

    qkv = jax.nn.silu(causal_dwconv(jnp.concatenate([q, k, v], -1), short_conv_w))
    q, k, v = jnp.split(qkv.astype(jnp.float32), 3, axis=-1)

    def heads(t):
        return t.reshape(B, L, GDN_HEADS, GDN_HEAD_DIM)

    q = l2norm(heads(q)) * (GDN_HEAD_DIM ** -0.5)
    k = l2norm(heads(k))
    v = heads(v)
    beta = jax.nn.sigmoid(b_logit.astype(jnp.float32))
    g = -jnp.exp(a_log.astype(jnp.float32)) * jax.nn.softplus(
        a_logit.astype(jnp.float32) + dt_bias)
    o = chunked_gated_delta_rule(q, k, v, g, beta)
    o = o * lax.rsqrt(jnp.mean(jnp.square(o), -1, keepdims=True) + RMS_EPS) * gdn_norm_g
    o = o * jax.nn.silu(heads(z).astype(jnp.float32))
    y_gdn = o.reshape(B, L, GDN_WIDTH).astype(h.dtype)

    mix = jnp.einsum('blc,cd->bld', jnp.concatenate([y_conv, y_gdn], -1), w_out)
    h = layer_norm(DEEPNORM_ALPHA * h + mix, ln1_g, ln1_b)
    moe = moe_ffn(h, router_w, router_bias, expert_w_gate, expert_w_up, expert_w_down,
                  shared_w_gate, shared_w_up, shared_w_down)
    return layer_norm(DEEPNORM_ALPHA * h + moe, ln2_g, ln2_b)


def setup_inputs(seed: int = 0) -> dict:
    key = jax.random.key(seed)
    ks = jax.random.split(key, 26)

    def nrm(k, shape, scale):
        return jax.random.normal(k, shape, jnp.float32) * scale

    dt = jnp.exp(jax.random.uniform(ks[11], (DEPTH, GDN_HEADS), jnp.float32,
                                    minval=math.log(1e-3), maxval=math.log(1e-1)))
    return {
        'x': nrm(ks[0], (BATCH, SEQ, D_MODEL), 1.0),
        'meta_tokens': nrm(ks[1], (N_META, D_MODEL), 1.0),
        'emb_ln_g': 1.0 + nrm(ks[2], (D_MODEL,), 0.01),
        'emb_ln_b': nrm(ks[3], (D_MODEL,), 0.01),
        'w_in': nrm(ks[4], (DEPTH, D_MODEL, IN_WIDTH), D_MODEL ** -0.5),
        'conv_w': nrm(ks[5], (DEPTH, CONV_WIDTH, CONV_CH), CONV_WIDTH ** -0.5),
        'conv_b': nrm(ks[6], (DEPTH, CONV_CH), 0.01),
        'conv_ln_g': 1.0 + nrm(ks[7], (DEPTH, CONV_CH), 0.01),
        'conv_ln_b': nrm(ks[8], (DEPTH, CONV_CH), 0.01),
        'short_conv_w': nrm(ks[9], (DEPTH, SHORT_CONV, 3 * GDN_WIDTH), SHORT_CONV ** -0.5),
        'a_log': jnp.log(jax.random.uniform(ks[10], (DEPTH, GDN_HEADS), jnp.float32,
                                            minval=1.0, maxval=16.0)),
        'dt_bias': dt + jnp.log(-jnp.expm1(-dt)),
        'gdn_norm_g': 1.0 + nrm(ks[12], (DEPTH, GDN_HEAD_DIM), 0.01),
        'w_out': nrm(ks[13], (DEPTH, MIX_WIDTH, D_MODEL), MIX_WIDTH ** -0.5 * DEEPNORM_BETA),
        'ln1_g': 1.0 + nrm(ks[14], (DEPTH, D_MODEL), 0.01),
        'ln1_b': nrm(ks[15], (DEPTH, D_MODEL), 0.01),
        'router_w': nrm(ks[16], (DEPTH, D_MODEL, N_EXPERTS), D_MODEL ** -0.5),
        'router_bias': nrm(ks[17], (DEPTH, N_EXPERTS), 0.01),
        'expert_w_gate': nrm(ks[18], (DEPTH, N_EXPERTS, D_MODEL, EXPERT_FF), D_MODEL ** -0.5),
        'expert_w_up': nrm(ks[19], (DEPTH, N_EXPERTS, D_MODEL, EXPERT_FF), D_MODEL ** -0.5),
        'expert_w_down': nrm(ks[20], (DEPTH, N_EXPERTS, EXPERT_FF, D_MODEL),
                             EXPERT_FF ** -0.5 * DEEPNORM_BETA),
        'shared_w_gate': nrm(ks[21], (DEPTH, D_MODEL, SHARED_FF), D_MODEL ** -0.5),
        'shared_w_up': nrm(ks[22], (DEPTH, D_MODEL, SHARED_FF), D_MODEL ** -0.5),
        'shared_w_down': nrm(ks[23], (DEPTH, SHARED_FF, D_MODEL), SHARED_FF ** -0.5 * DEEPNORM_BETA),
        'ln2_g': 1.0 + nrm(ks[24], (DEPTH, D_MODEL), 0.01),
        'ln2_b': nrm(ks[25], (DEPTH, D_MODEL), 0.01),
    }


def reference(x, meta_tokens, emb_ln_g, emb_ln_b, w_in, conv_w, conv_b, conv_ln_g, conv_ln_b,
              short_conv_w, a_log, dt_bias, gdn_norm_g, w_out, ln1_g, ln1_b, router_w,
              router_bias, expert_w_gate, expert_w_up, expert_w_down, shared_w_gate,
              shared_w_up, shared_w_down, ln2_g, ln2_b):
    B = x.shape[0]
    meta = jnp.broadcast_to(meta_tokens[None].astype(x.dtype), (B, N_META, D_MODEL))
    h = layer_norm(jnp.concatenate([meta, x], axis=1), emb_ln_g, emb_ln_b)
    for l in range(DEPTH):
        h = hybrid_layer(h, w_in[l], conv_w[l], conv_b[l], conv_ln_g[l], conv_ln_b[l],
                         short_conv_w[l], a_log[l], dt_bias[l], gdn_norm_g[l], w_out[l],
                         ln1_g[l], ln1_b[l], router_w[l], router_bias[l], expert_w_gate[l],
                         expert_w_up[l], expert_w_down[l], shared_w_gate[l], shared_w_up[l],
                         shared_w_down[l], ln2_g[l], ln2_b[l])
    return h[:, N_META:]
```

```python
import functools

import jax
import jax.numpy as jnp
from jax import lax
from jax.experimental import pallas as pl
from jax.experimental.pallas import tpu as pltpu

F32 = jnp.float32
BF16 = jnp.bfloat16
I32 = jnp.int32
U32 = jnp.uint32

N_META = 16
CHUNK = 64
HEAD_DIM = 128
TOP_K = 8
N_EXPERT_GROUPS = 8
TOPK_GROUPS = 4
ROUTED_SCALE = 2.5
LN_EPS = 1e-5
RMS_EPS = 1e-6
CONV_HALO = 32
VMEM_LIMIT = 56 * 1024 * 1024


def _params(n_axes, vmem=VMEM_LIMIT):
    return pltpu.CompilerParams(dimension_semantics=("arbitrary",) * n_axes, vmem_limit_bytes=vmem)


def _ln_rows(x, g, b):
    mu = jnp.mean(x, -1, keepdims=True)
    xc = x - mu
    var = jnp.mean(xc * xc, -1, keepdims=True)
    return xc * lax.rsqrt(var + LN_EPS) * g + b


def _sigmoid(x):
    return 1.0 / (1.0 + jnp.exp(-x))


def _silu(x):
    return x * _sigmoid(x)


def _dot(a, b):
    return jnp.dot(a, b, preferred_element_type=F32)


def _dot_nt(a, b, precision=None):
    return lax.dot_general(a, b, (((1,), (1,)), ((), ())), precision=precision, preferred_element_type=F32)


def _dot_tn(a, b):
    return lax.dot_general(a, b, (((0,), (0,)), ((), ())), preferred_element_type=F32)


def _embed_ln_kernel(x_ref, g_ref, b_ref, hf_ref, hb_ref):
    h = _ln_rows(x_ref[...], g_ref[...], b_ref[...])
    hf_ref[...] = h
    hb_ref[...] = h.astype(BF16)


def _embed_ln(x, g, b, tm):
    m, d = x.shape
    return pl.pallas_call(
        _embed_ln_kernel,
        grid=(m // tm,),
        in_specs=[pl.BlockSpec((tm, d), lambda i: (i, 0)),
                  pl.BlockSpec((1, d), lambda i: (0, 0)),
                  pl.BlockSpec((1, d), lambda i: (0, 0))],
        out_specs=[pl.BlockSpec((tm, d), lambda i: (i, 0)),
                   pl.BlockSpec((tm, d), lambda i: (i, 0))],
        out_shape=[jax.ShapeDtypeStruct((m, d), F32), jax.ShapeDtypeStruct((m, d), BF16)],
        compiler_params=_params(1),
    )(x, g, b)


def _proj_plain_kernel(a_ref, w_ref, o_ref, wb_ref):
    @pl.when(pl.program_id(1) == 0)
    def _():
        wb_ref[...] = w_ref[...].astype(BF16)
    o_ref[...] = _dot(a_ref[...], wb_ref[...]).astype(o_ref.dtype)


def _proj_pair_kernel(a_ref, w1_ref, w2_ref, o_ref, w1b_ref, w2b_ref, *, act):
    @pl.when(pl.program_id(1) == 0)
    def _():
        w1b_ref[...] = w1_ref[...].astype(BF16)
        w2b_ref[...] = w2_ref[...].astype(BF16)
    a = a_ref[...]
    p1 = _dot(a, w1b_ref[...])
    p2 = _dot(a, w2b_ref[...])
    if act == "glu":
        o = p1 * _sigmoid(p2)
    else:
        o = _silu(p1) * p2
    o_ref[...] = o.astype(o_ref.dtype)


def _proj_decay_kernel(a_ref, w_ref, alog_ref, dtb_ref, o_ref, wb_ref, *, n_heads):
    @pl.when(pl.program_id(1) == 0)
    def _():
        wb_ref[...] = w_ref[...].astype(BF16)
    p = _dot(a_ref[...], wb_ref[...])
    x = p + dtb_ref[...]
    softplus = jnp.maximum(x, 0.0) + jnp.log1p(jnp.exp(-jnp.abs(x)))
    g = -jnp.exp(alog_ref[...]) * softplus
    lane = lax.broadcasted_iota(I32, p.shape, 1)
    o_ref[...] = jnp.where(lane < n_heads, _sigmoid(p), g)


def _proj_residual_kernel(a1_ref, a2_ref, w_ref, r_ref, o_ref, wb_ref, *, alpha, k1):
    @pl.when(pl.program_id(1) == 0)
    def _():
        wb_ref[...] = w_ref[...].astype(BF16)
    acc = _dot(a1_ref[...], wb_ref[0:k1, :]) + _dot(a2_ref[...], wb_ref[k1:, :])
    o_ref[...] = alpha * r_ref[...] + acc


def _proj_plain(a, w, col0, n, tm, tn, out_dtype=F32):
    m, k = a.shape
    cb = col0 // tn
    return pl.pallas_call(
        _proj_plain_kernel,
        grid=(n // tn, m // tm),
        in_specs=[pl.BlockSpec((tm, k), lambda j, i: (i, 0)),
                  pl.BlockSpec((k, tn), lambda j, i: (0, j + cb))],
        out_specs=pl.BlockSpec((tm, tn), lambda j, i: (i, j)),
        out_shape=jax.ShapeDtypeStruct((m, n), out_dtype),
        scratch_shapes=[pltpu.VMEM((k, tn), BF16)],
        compiler_params=_params(2),
    )(a, w)


def _proj_pair(a, w1, w2, col1, col2, n, tm, tn, act, out_dtype):
    m, k = a.shape
    c1, c2 = col1 // tn, col2 // tn
    return pl.pallas_call(
        functools.partial(_proj_pair_kernel, act=act),
        grid=(n // tn, m // tm),
        in_specs=[pl.BlockSpec((tm, k), lambda j, i: (i, 0)),
                  pl.BlockSpec((k, tn), lambda j, i: (0, j + c1)),
                  pl.BlockSpec((k, tn), lambda j, i: (0, j + c2))],
        out_specs=pl.BlockSpec((tm, tn), lambda j, i: (i, j)),
        out_shape=jax.ShapeDtypeStruct((m, n), out_dtype),
        scratch_shapes=[pltpu.VMEM((k, tn), BF16), pltpu.VMEM((k, tn), BF16)],
        compiler_params=_params(2),
    )(a, w1, w2)


def _proj_decay(a, w_ba, alog_row, dtb_row, tm, n_heads):
    m, k = a.shape
    n = w_ba.shape[1]
    return pl.pallas_call(
        functools.partial(_proj_decay_kernel, n_heads=n_heads),
        grid=(1, m // tm),
        in_specs=[pl.BlockSpec((tm, k), lambda j, i: (i, 0)),
                  pl.BlockSpec((k, n), lambda j, i: (0, 0)),
                  pl.BlockSpec((1, n), lambda j, i: (0, 0)),
                  pl.BlockSpec((1, n), lambda j, i: (0, 0))],
        out_specs=pl.BlockSpec((tm, n), lambda j, i: (i, 0)),
        out_shape=jax.ShapeDtypeStruct((m, n), F32),
        scratch_shapes=[pltpu.VMEM((k, n), BF16)],
        compiler_params=_params(2),
    )(a, w_ba, alog_row, dtb_row)


def _proj_residual(a1, a2, w, resid, alpha, tm, tn):
    m, k1 = a1.shape
    k2 = a2.shape[1]
    n = w.shape[1]
    return pl.pallas_call(
        functools.partial(_proj_residual_kernel, alpha=alpha, k1=k1),
        grid=(n // tn, m // tm),
        in_specs=[pl.BlockSpec((tm, k1), lambda j, i: (i, 0)),
                  pl.BlockSpec((tm, k2), lambda j, i: (i, 0)),
                  pl.BlockSpec((k1 + k2, tn), lambda j, i: (0, j)),
                  pl.BlockSpec((tm, tn), lambda j, i: (i, j))],
        out_specs=pl.BlockSpec((tm, tn), lambda j, i: (i, j)),
        out_shape=jax.ShapeDtypeStruct((m, n), F32),
        scratch_shapes=[pltpu.VMEM((k1 + k2, tn), BF16)],
        compiler_params=_params(2),
    )(a1, a2, w, resid)


def _conv_module_kernel(c_ref, cm_ref, w_ref, b_ref, g_ref, be_ref, o_ref, buf_ref, *, ts, kw):
    i = pl.program_id(0)
    ch = c_ref.shape[1]

    @pl.when(i == 0)
    def _():
        buf_ref[0:CONV_HALO - N_META, :] = jnp.zeros((CONV_HALO - N_META, ch), F32)
        buf_ref[CONV_HALO - N_META:CONV_HALO, :] = cm_ref[...]

    @pl.when(i > 0)
    def _():
        buf_ref[0:CONV_HALO, :] = buf_ref[ts:ts + CONV_HALO, :]

    buf_ref[CONV_HALO:CONV_HALO + ts, :] = c_ref[...]
    base = CONV_HALO - (kw - 1)
    acc = jnp.zeros((ts, ch), F32) + b_ref[...]
    for j in range(kw):
        acc = acc + w_ref[j:j + 1, :] * buf_ref[base + j:base + j + ts, :]
    y = _ln_rows(acc, g_ref[...], be_ref[...])
    o_ref[...] = _silu(y).astype(o_ref.dtype)


def _conv_module(c, c_meta, w, b, g, be, ts):
    t, ch = c.shape
    kw = w.shape[0]
    assert kw - 1 <= CONV_HALO and N_META <= CONV_HALO and ts >= CONV_HALO
    return pl.pallas_call(
        functools.partial(_conv_module_kernel, ts=ts, kw=kw),
        grid=(t // ts,),
        in_specs=[pl.BlockSpec((ts, ch), lambda i: (i, 0)),
                  pl.BlockSpec((N_META, ch), lambda i: (0, 0)),
                  pl.BlockSpec((kw, ch), lambda i: (0, 0)),
                  pl.BlockSpec((1, ch), lambda i: (0, 0)),
                  pl.BlockSpec((1, ch), lambda i: (0, 0)),
                  pl.BlockSpec((1, ch), lambda i: (0, 0))],
        out_specs=pl.BlockSpec((ts, ch), lambda i: (i, 0)),
        out_shape=jax.ShapeDtypeStruct((t, ch), BF16),
        scratch_shapes=[pltpu.VMEM((CONV_HALO + ts, ch), F32)],
        compiler_params=_params(1),
    )(c, c_meta, w, b, g, be)


def _gdn_kernel(x_ref, m_ref, bg_ref, gt_ref, cw_ref, gn_ref, o_ref, cbuf_ref, act_ref, s_ref,
                *, n_heads):
    c = pl.program_id(0)
    gw = n_heads * HEAD_DIM
    C = CHUNK

    @pl.when(c == 0)
    def _():
        cbuf_ref[0:8, :] = jnp.zeros((8, 3 * gw), F32)
        s_ref[...] = jnp.zeros(s_ref.shape, F32)

    @pl.when(c > 0)
    def _():
        cbuf_ref[0:8, :] = cbuf_ref[C:C + 8, :]

    raw = jnp.where(c == 0, m_ref[...], x_ref[...])
    cbuf_ref[8:8 + C, :] = raw[:, 0:3 * gw]
    kw = cw_ref.shape[0]
    conv = jnp.zeros((C, 3 * gw), F32)
    for j in range(kw):
        conv = conv + cw_ref[j:j + 1, :] * cbuf_ref[9 - kw + j:9 - kw + j + C, :]
    act_ref[...] = _silu(conv)

    bg = bg_ref[...]
    beta_all = bg[:, 0:n_heads]
    g_all = bg[:, n_heads:2 * n_heads]
    row = lax.broadcasted_iota(I32, (C, C), 0)
    col = lax.broadcasted_iota(I32, (C, C), 1)
    incl = row >= col
    strict = row > col
    tri_incl = jnp.where(incl, 1.0, 0.0).astype(F32)
    tri_upper = jnp.where(row <= col, 1.0, 0.0).astype(F32)
    hp = lax.Precision.HIGHEST
    gc_cols = jnp.dot(tri_incl, g_all, precision=hp, preferred_element_type=F32)
    gc_rows = jnp.dot(gt_ref[0], tri_upper, precision=hp, preferred_element_type=F32)
    eye = jnp.where(row == col, 1.0, 0.0).astype(F32)
    right_half = lax.broadcasted_iota(I32, (C, 2 * C), 1) >= C
    gn = gn_ref[...]
    z_all = raw[:, 3 * gw:4 * gw]

    for h in range(n_heads):
        sl = slice(h * HEAD_DIM, (h + 1) * HEAD_DIM)
        q = act_ref[:, h * HEAD_DIM:(h + 1) * HEAD_DIM]
        k = act_ref[:, gw + h * HEAD_DIM:gw + (h + 1) * HEAD_DIM]
        v = act_ref[:, 2 * gw + h * HEAD_DIM:2 * gw + (h + 1) * HEAD_DIM]
        q = q * lax.rsqrt(jnp.sum(q * q, -1, keepdims=True) + RMS_EPS) * (HEAD_DIM ** -0.5)
        k = k * lax.rsqrt(jnp.sum(k * k, -1, keepdims=True) + RMS_EPS)
        beta = beta_all[:, h:h + 1]
        gcol = gc_cols[:, h:h + 1]
        grow = gc_rows[h:h + 1, :]
        g_last = gc_cols[C - 1:C, h:h + 1]
        decay = jnp.where(incl, jnp.exp(jnp.where(incl, gcol - grow, 0.0)), 0.0)
        egc = jnp.exp(gcol)
        kb = k * beta
        kbf = k.astype(BF16)
        a1 = _dot_nt(jnp.concatenate([kb, q], axis=0).astype(BF16), kbf)
        lmat = jnp.where(strict, a1[0:C] * decay, 0.0)
        attn = a1[C:2 * C] * decay
        wmat = jnp.concatenate([-lmat, eye], axis=1)
        for _ in range(6):
            mn = wmat[:, 0:C].astype(BF16)
            r = _dot(mn, wmat.astype(BF16))
            wmat = r + jnp.where(right_half, wmat, 0.0)
        tinv = jnp.where(right_half, wmat, 0.0).astype(BF16)
        rhs = jnp.concatenate([v * beta, kb * egc], axis=1).astype(BF16)
        sol = _dot(tinv, jnp.concatenate([rhs, rhs], axis=0))
        u = sol[:, 0:HEAD_DIM]
        w = sol[:, HEAD_DIM:2 * HEAD_DIM]
        q_dec = q * egc
        k_dec = k * jnp.exp(g_last - gcol)
        s_old = s_ref[h]
        ws_qs = _dot(jnp.concatenate([w, q_dec], axis=0).astype(BF16), s_old.astype(BF16))
        v_new = u - ws_qs[0:C]
        vnb = v_new.astype(BF16)
        o = ws_qs[C:2 * C] + _dot(attn.astype(BF16), vnb)
        s_ref[h] = s_old * jnp.exp(g_last) + _dot_tn(k_dec.astype(BF16), vnb)
        o = o * lax.rsqrt(jnp.mean(o * o, -1, keepdims=True) + RMS_EPS) * gn
        o = o * _silu(z_all[:, sl])
        o_ref[:, sl] = o.astype(o_ref.dtype)


def _gdn(qkvz, qkvz_meta, bg_all, gt3, conv_w, gn_row, n_heads):
    t = qkvz.shape[0]
    gw = n_heads * HEAD_DIM
    n_chunks = t // CHUNK + 1
    kw = conv_w.shape[0]
    assert kw <= 8
    xmap = lambda c: (jnp.maximum(c - 1, 0), 0)
    return pl.pallas_call(
        functools.partial(_gdn_kernel, n_heads=n_heads),
        grid=(n_chunks,),
        in_specs=[pl.BlockSpec((CHUNK, 4 * gw), xmap),
                  pl.BlockSpec((CHUNK, 4 * gw), lambda c: (0, 0)),
                  pl.BlockSpec((CHUNK, 2 * n_heads), lambda c: (c, 0)),
                  pl.BlockSpec((1, n_heads, CHUNK), lambda c: (c, 0, 0)),
                  pl.BlockSpec((kw, 3 * gw), lambda c: (0, 0)),
                  pl.BlockSpec((1, HEAD_DIM), lambda c: (0, 0))],
        out_specs=pl.BlockSpec((CHUNK, gw), xmap),
        out_shape=jax.ShapeDtypeStruct((t, gw), BF16),
        scratch_shapes=[pltpu.VMEM((CHUNK + 8, 3 * gw), F32),
                        pltpu.VMEM((CHUNK, 3 * gw), F32),
                        pltpu.VMEM((n_heads, HEAD_DIM, HEAD_DIM), F32)],
        compiler_params=_params(1),
    )(qkvz, qkvz_meta, bg_all, gt3, conv_w, gn_row)


def _router_kernel(r_ref, g_ref, b_ref, rwt_ref, bias_ref, h1_ref, h1b_ref, h1p_ref,
                   eidx_ref, wsel_ref, pos_ref, cnt_ref, carry_ref, *, tm, n_exp):
    i = pl.program_id(0)
    d = r_ref.shape[1]
    per = n_exp // N_EXPERT_GROUPS

    @pl.when(i == 0)
    def _():
        carry_ref[...] = jnp.zeros(carry_ref.shape, F32)

    h1 = _ln_rows(r_ref[...], g_ref[...], b_ref[...])
    h1_ref[...] = h1
    hb = h1.astype(BF16)
    h1b_ref[...] = hb
    hi = lax.bitcast_convert_type(hb[:, 0:d // 2].astype(F32), U32)
    lo = lax.bitcast_convert_type(hb[:, d // 2:].astype(F32), U32)
    h1p_ref[...] = hi | (lo >> 16)

    logits = _dot_nt(rwt_ref[...], h1, precision=lax.Precision.HIGHEST)
    scores = _sigmoid(logits)
    biased = scores + bias_ref[...]
    neg_inf = F32(-jnp.inf)

    gs_rows = []
    sub = lax.broadcasted_iota(I32, (per, tm), 0)
    for g in range(N_EXPERT_GROUPS):
        xg = biased[g * per:(g + 1) * per, :]
        m1 = jnp.max(xg, axis=0, keepdims=True)
        first = jnp.min(jnp.where(xg == m1, sub, per), axis=0, keepdims=True)
        m2 = jnp.max(jnp.where(sub == first, neg_inf, xg), axis=0, keepdims=True)
        gs_rows.append(m1 + m2)
    gs = jnp.concatenate(gs_rows, axis=0)
    gid = lax.broadcasted_iota(I32, (N_EXPERT_GROUPS, tm), 0)
    grank = jnp.zeros((N_EXPERT_GROUPS, tm), F32)
    for g in range(N_EXPERT_GROUPS):
        rowv = gs[g:g + 1, :]
        grank = grank + jnp.where(rowv > gs, 1.0, jnp.where((rowv == gs) & (gid > g), 1.0, 0.0))
    gsel = jnp.where(grank < TOPK_GROUPS, 1.0, 0.0)
    emask = jnp.concatenate(
        [jnp.broadcast_to(gsel[g:g + 1, :], (per, tm)) for g in range(N_EXPERT_GROUPS)], axis=0)
    masked = jnp.where(emask > 0.0, biased, neg_inf)
    eid = lax.broadcasted_iota(I32, (n_exp, tm), 0)
    erank = jnp.zeros((n_exp, tm), F32)
    for e in range(n_exp):
        rowv = masked[e:e + 1, :]
        erank = erank + jnp.where(rowv > masked, 1.0, jnp.where((rowv == masked) & (eid > e), 1.0, 0.0))
    sel = jnp.where(erank < TOP_K, emask, 0.0)
    wdense = scores * sel
    gate = wdense / jnp.sum(wdense, axis=0, keepdims=True) * ROUTED_SCALE

    er = lax.broadcasted_iota(I32, (n_exp, n_exp), 0)
    ec = lax.broadcasted_iota(I32, (n_exp, n_exp), 1)
    lower = jnp.where(er > ec, 1.0, 0.0).astype(BF16)
    selb = sel.astype(BF16)
    slot = _dot(lower, selb)
    tr = lax.broadcasted_iota(I32, (tm, tm), 0)
    tc = lax.broadcasted_iota(I32, (tm, tm), 1)
    upper = jnp.where(tr < tc, 1.0, 0.0).astype(BF16)
    carry = carry_ref[:, 0:1]
    pos = _dot(selb, upper) + carry
    eid_f = eid.astype(F32)
    e_rows, w_rows, p_rows = [], [], []
    for k in range(TOP_K):
        mk = jnp.where(slot == float(k), sel, 0.0)
        e_rows.append(jnp.sum(mk * eid_f, axis=0, keepdims=True))
        w_rows.append(jnp.sum(mk * gate, axis=0, keepdims=True))
        p_rows.append(jnp.sum(mk * pos, axis=0, keepdims=True))
    eidx_ref[...] = jnp.concatenate(e_rows, axis=0).astype(I32)
    wsel_ref[...] = jnp.concatenate(w_rows, axis=0)
    pos_ref[...] = jnp.concatenate(p_rows, axis=0).astype(I32)
    new_carry = carry + jnp.sum(sel, axis=1, keepdims=True)
    carry_ref[...] = jnp.broadcast_to(new_carry, carry_ref.shape)
    cnt_ref[...] = jnp.broadcast_to(new_carry, cnt_ref.shape).astype(I32)


def _router(r, g, b, rwt, bias_col, tm):
    t, d = r.shape
    n_exp = rwt.shape[0]
    row = lambda i: (i, 0)
    colb = lambda i: (0, i)
    fixed = lambda i: (0, 0)
    return pl.pallas_call(
        functools.partial(_router_kernel, tm=tm, n_exp=n_exp),
        grid=(t // tm,),
        in_specs=[pl.BlockSpec((tm, d), row),
                  pl.BlockSpec((1, d), fixed),
                  pl.BlockSpec((1, d), fixed),
                  pl.BlockSpec((n_exp, d), fixed),
                  pl.BlockSpec((n_exp, 1), fixed)],
        out_specs=[pl.BlockSpec((tm, d), row),
                   pl.BlockSpec((tm, d), row),
                   pl.BlockSpec((tm, d // 2), row),
                   pl.BlockSpec((TOP_K, tm), colb),
                   pl.BlockSpec((TOP_K, tm), colb),
                   pl.BlockSpec((TOP_K, tm), colb),
                   pl.BlockSpec((n_exp, 128), fixed)],
        out_shape=[jax.ShapeDtypeStruct((t, d), F32),
                   jax.ShapeDtypeStruct((t, d), BF16),
                   jax.ShapeDtypeStruct((t, d // 2), U32),
                   jax.ShapeDtypeStruct((TOP_K, t), I32),
                   jax.ShapeDtypeStruct((TOP_K, t), F32),
                   jax.ShapeDtypeStruct((TOP_K, t), I32),
                   jax.ShapeDtypeStruct((n_exp, 128), I32)],
        scratch_shapes=[pltpu.VMEM((n_exp, 128), F32)],
        compiler_params=_params(1),
    )(r, g, b, rwt, bias_col)


def _dest_kernel(gstart_ref, eidx_ref, pos_ref, dest_ref, *, n_exp):
    eidx = eidx_ref[...]
    acc = pos_ref[...]
    for e in range(n_exp):
        acc = acc + jnp.where(eidx == e, gstart_ref[e], 0)
    dest_ref[...] = acc


def _dest_rows(gstart, eidx, pos):
    n_exp = gstart.shape[0]
    return pl.pallas_call(
        functools.partial(_dest_kernel, n_exp=n_exp),
        grid_spec=pltpu.PrefetchScalarGridSpec(
            num_scalar_prefetch=1,
            grid=(1,),
            in_specs=[pl.BlockSpec(eidx.shape, lambda i, gs: (0, 0)),
                      pl.BlockSpec(pos.shape, lambda i, gs: (0, 0))],
            out_specs=pl.BlockSpec(eidx.shape, lambda i, gs: (0, 0))),
        out_shape=jax.ShapeDtypeStruct(eidx.shape, I32),
        compiler_params=_params(1),
    )(gstart, eidx, pos)


def _dispatch_kernel(dest_ref, hp_ref, xs_ref, sem, *, tm):
    def issue(t, carry):
        for k in range(TOP_K):
            d = dest_ref[0, 0, t * TOP_K + k]
            pltpu.make_async_copy(hp_ref.at[pl.ds(t, 1)], xs_ref.at[pl.ds(d, 1)], sem).start()
        return carry
    lax.fori_loop(0, tm, issue, 0)
    pltpu.make_async_copy(xs_ref.at[pl.ds(0, tm * TOP_K)], xs_ref.at[pl.ds(0, tm * TOP_K)], sem).wait()


def _dispatch(dest_tiles, h1p, tm):
    t, dw = h1p.shape
    return pl.pallas_call(
        functools.partial(_dispatch_kernel, tm=tm),
        grid=(t // tm,),
        in_specs=[pl.BlockSpec((1, 1, tm * TOP_K), lambda i: (i, 0, 0), memory_space=pltpu.SMEM),
                  pl.BlockSpec((tm, dw), lambda i: (i, 0))],
        out_specs=pl.BlockSpec(memory_space=pl.ANY),
        out_shape=jax.ShapeDtypeStruct((t * TOP_K, dw), U32),
        scratch_shapes=[pltpu.SemaphoreType.DMA(())],
        compiler_params=_params(1),
    )(dest_tiles, h1p)


def _row_mask(vt_ref, ve_ref, gs_ref, ge_ref, v, tm, width):
    e = ve_ref[v]
    rows = vt_ref[v] * tm + lax.broadcasted_iota(I32, (tm, width), 0)
    return (rows >= gs_ref[e]) & (rows < ge_ref[e])


def _unpack_rows(xu):
    hi = lax.bitcast_convert_type(xu & jnp.uint32(0xFFFF0000), F32).astype(BF16)
    lo = lax.bitcast_convert_type(xu << 16, F32).astype(BF16)
    return hi, lo


def _gmm_up_kernel(vt_ref, ve_ref, vft_ref, vfe_ref, gs_ref, ge_ref, nv_ref,
                   xs_ref, wg_ref, wu_ref, o_ref, wgb_ref, wub_ref, *, tm):
    v = pl.program_id(0)

    @pl.when(v < nv_ref[0])
    def _():
        @pl.when(vfe_ref[v] == 1)
        def _():
            wgb_ref[...] = wg_ref[0].astype(BF16)
            wub_ref[...] = wu_ref[0].astype(BF16)
        hi, lo = _unpack_rows(xs_ref[...])
        kh = hi.shape[1]
        hg = _dot(hi, wgb_ref[0:kh, :]) + _dot(lo, wgb_ref[kh:, :])
        hu = _dot(hi, wub_ref[0:kh, :]) + _dot(lo, wub_ref[kh:, :])
        a = (_silu(hg) * hu).astype(o_ref.dtype)
        mask = _row_mask(vt_ref, ve_ref, gs_ref, ge_ref, v, tm, a.shape[1])

        @pl.when(vft_ref[v] == 1)
        def _():
            o_ref[...] = jnp.where(mask, a, jnp.zeros_like(a))

        @pl.when(vft_ref[v] == 0)
        def _():
            o_ref[...] = jnp.where(mask, a, o_ref[...])


def _gmm_down_kernel(vt_ref, ve_ref, vft_ref, vfe_ref, gs_ref, ge_ref, nv_ref,
                     a_ref, wd_ref, o_ref, wdb_ref, *, tm):
    v = pl.program_id(0)

    @pl.when(v < nv_ref[0])
    def _():
        @pl.when(vfe_ref[v] == 1)
        def _():
            wdb_ref[...] = wd_ref[0].astype(BF16)
        y = _dot(a_ref[...], wdb_ref[...])
        mask = _row_mask(vt_ref, ve_ref, gs_ref, ge_ref, v, tm, y.shape[1])

        @pl.when(vft_ref[v] == 1)
        def _():
            o_ref[...] = jnp.where(mask, y, 0.0)

        @pl.when(vft_ref[v] == 0)
        def _():
            o_ref[...] = jnp.where(mask, y, o_ref[...])


def _visit_schedule(counts, n_rows, tm):
    n_exp = counts.shape[0]
    n_tiles = n_rows // tm
    n_visits = n_tiles + n_exp - 1
    ends = jnp.cumsum(counts)
    starts = ends - counts
    first_tile = starts // tm
    last_tile = jnp.maximum(ends - 1, 0) // tm
    tiles_e = jnp.where(counts > 0, last_tile - first_tile + 1, 0)
    vend = jnp.cumsum(tiles_e)
    vstart = vend - tiles_e
    total = vend[-1]
    v = jnp.arange(n_visits, dtype=I32)
    vc = jnp.minimum(v, total - 1)
    ve = jnp.searchsorted(vend, vc, side="right").astype(I32)
    vt = (vc - vstart[ve] + first_tile[ve]).astype(I32)
    prev_t = jnp.concatenate([jnp.full((1,), -1, I32), vt[:-1]])
    prev_e = jnp.concatenate([jnp.full((1,), -1, I32), ve[:-1]])
    vft = (vt != prev_t).astype(I32)
    vfe = (ve != prev_e).astype(I32)
    return (vt, ve, vft, vfe, starts.astype(I32), ends.astype(I32), total.reshape(1).astype(I32)), n_visits


def _gmm_up(sched, n_visits, xs, wg, wu, tm):
    n, dw = xs.shape
    n_exp, d, f = wg.shape
    return pl.pallas_call(
        functools.partial(_gmm_up_kernel, tm=tm),
        grid_spec=pltpu.PrefetchScalarGridSpec(
            num_scalar_prefetch=7,
            grid=(n_visits,),
            in_specs=[pl.BlockSpec((tm, dw), lambda v, vt, ve, *_: (vt[v], 0)),
                      pl.BlockSpec((1, d, f), lambda v, vt, ve, *_: (ve[v], 0, 0)),
                      pl.BlockSpec((1, d, f), lambda v, vt, ve, *_: (ve[v], 0, 0))],
            out_specs=pl.BlockSpec((tm, f), lambda v, vt, ve, *_: (vt[v], 0)),
            scratch_shapes=[pltpu.VMEM((d, f), BF16), pltpu.VMEM((d, f), BF16)]),
        out_shape=jax.ShapeDtypeStruct((n, f), BF16),
        compiler_params=_params(1),
    )(*sched, xs, wg, wu)


def _gmm_down(sched, n_visits, act, wd, tm):
    n, f = act.shape
    n_exp, _, d = wd.shape
    return pl.pallas_call(
        functools.partial(_gmm_down_kernel, tm=tm),
        grid_spec=pltpu.PrefetchScalarGridSpec(
            num_scalar_prefetch=7,
            grid=(n_visits,),
            in_specs=[pl.BlockSpec((tm, f), lambda v, vt, ve, *_: (vt[v], 0)),
                      pl.BlockSpec((1, f, d), lambda v, vt, ve, *_: (ve[v], 0, 0))],
            out_specs=pl.BlockSpec((tm, d), lambda v, vt, ve, *_: (vt[v], 0)),
            scratch_shapes=[pltpu.VMEM((f, d), BF16)]),
        out_shape=jax.ShapeDtypeStruct((n, d), F32),
        compiler_params=_params(1),
    )(*sched, act, wd)


def _combine_kernel(dest_ref, w_ref, h1_ref, as_ref, wsd_ref, g_ref, b_ref, ys_ref, o_ref,
                    gbuf_ref, wsdb_ref, sem, *, tm, alpha):
    i = pl.program_id(0)

    def issue(t, carry):
        for k in range(TOP_K):
            d = dest_ref[0, 0, t * TOP_K + k]
            pltpu.make_async_copy(ys_ref.at[pl.ds(d, 1)], gbuf_ref.at[pl.ds(k * tm + t, 1)], sem).start()
        return carry
    lax.fori_loop(0, tm, issue, 0)

    @pl.when(i == 0)
    def _():
        wsdb_ref[...] = wsd_ref[...].astype(BF16)

    y = alpha * h1_ref[...] + _dot(as_ref[...], wsdb_ref[...])
    pltpu.make_async_copy(ys_ref.at[pl.ds(0, tm * TOP_K)], gbuf_ref, sem).wait()
    w = w_ref[...]
    for k in range(TOP_K):
        y = y + w[:, k:k + 1] * gbuf_ref[k * tm:(k + 1) * tm, :]
    o_ref[...] = _ln_rows(y, g_ref[...], b_ref[...])


def _combine(dest_tiles, w_tok, h1, act_s, wsd, g, b, ys, tm, alpha):
    t, d = h1.shape
    f = act_s.shape[1]
    row = lambda i: (i, 0)
    fixed = lambda i: (0, 0)
    return pl.pallas_call(
        functools.partial(_combine_kernel, tm=tm, alpha=alpha),
        grid=(t // tm,),
        in_specs=[pl.BlockSpec((1, 1, tm * TOP_K), lambda i: (i, 0, 0), memory_space=pltpu.SMEM),
                  pl.BlockSpec((tm, TOP_K), row),
                  pl.BlockSpec((tm, d), row),
                  pl.BlockSpec((tm, f), row),
                  pl.BlockSpec((f, d), fixed),
                  pl.BlockSpec((1, d), fixed),
                  pl.BlockSpec((1, d), fixed),
                  pl.BlockSpec(memory_space=pl.ANY)],
        out_specs=pl.BlockSpec((tm, d), row),
        out_shape=jax.ShapeDtypeStruct((t, d), F32),
        scratch_shapes=[pltpu.VMEM((tm * TOP_K, d), F32),
                        pltpu.VMEM((f, d), BF16),
                        pltpu.SemaphoreType.DMA(())],
        compiler_params=_params(1),
    )(dest_tiles, w_tok, h1, act_s, wsd, g, b, ys)


def _tile(n, pref):
    t = min(n, pref)
    assert n % t == 0, (n, pref)
    return t


def kernel(x, meta_tokens, emb_ln_g, emb_ln_b, w_in, conv_w, conv_b, conv_ln_g, conv_ln_b, short_conv_w, a_log, dt_bias, gdn_norm_g, w_out, ln1_g, ln1_b, router_w, router_bias, expert_w_gate, expert_w_up, expert_w_down, shared_w_gate, shared_w_up, shared_w_down, ln2_g, ln2_b):
    depth = w_in.shape[0]
    assert depth == 1 and x.shape[0] == 1, "single layer, single sequence"
    t, d = x.shape[1], x.shape[2]
    cw = conv_w.shape[2]
    n_heads = a_log.shape[1]
    gw = n_heads * HEAD_DIM
    n_exp = router_w.shape[2]
    alpha = (2.0 * depth) ** 0.25
    assert meta_tokens.shape[0] == N_META and t % CHUNK == 0 and w_in.shape[2] == 2 * cw + 4 * gw + 2 * n_heads
    row = lambda a: a.reshape(1, -1)
    w_in0 = w_in[0]

    h_f, h_b = _embed_ln(x[0], row(emb_ln_g), row(emb_ln_b), _tile(t, 256))
    _, hm_b = _embed_ln(meta_tokens, row(emb_ln_g), row(emb_ln_b), N_META)

    tm = _tile(t, 1024)
    w_ba = w_in0[:, 2 * cw + 4 * gw:]
    zeros_h = jnp.zeros((n_heads,), F32)
    alog_row = row(jnp.concatenate([zeros_h, a_log[0]]))
    dtb_row = row(jnp.concatenate([zeros_h, dt_bias[0]]))
    tn_glu = _tile(cw, 256)
    tn_q = _tile(4 * gw, 512)
    c_x = _proj_pair(h_b, w_in0, w_in0, 0, cw, cw, tm, tn_glu, "glu", F32)
    c_m = _proj_pair(hm_b, w_in0, w_in0, 0, cw, cw, N_META, tn_glu, "glu", F32)
    qkvz_x = _proj_plain(h_b, w_in0, 2 * cw, 4 * gw, tm, tn_q)
    qkvz_m = _proj_plain(hm_b, w_in0, 2 * cw, 4 * gw, N_META, tn_q)
    bg_x = _proj_decay(h_b, w_ba, alog_row, dtb_row, tm, n_heads)
    bg_m = _proj_decay(hm_b, w_ba, alog_row, dtb_row, N_META, n_heads)

    y_conv = _conv_module(c_x, c_m, conv_w[0], row(conv_b[0]), row(conv_ln_g[0]), row(conv_ln_b[0]),
                          _tile(t, 128))

    pad = CHUNK - N_META
    qkvz_m64 = jnp.pad(qkvz_m, ((pad, 0), (0, 0)))
    bg_all = jnp.concatenate([jnp.pad(bg_m, ((pad, 0), (0, 0))), bg_x], axis=0)
    gt3 = bg_all[:, n_heads:].reshape(-1, CHUNK, n_heads).transpose(0, 2, 1)
    y_gdn = _gdn(qkvz_x, qkvz_m64, bg_all, gt3, short_conv_w[0], row(gdn_norm_g[0]), n_heads)

    r1 = _proj_residual(y_conv, y_gdn, w_out[0], h_f, alpha, tm, _tile(d, 512))
    tmr = _tile(t, 256)
    h1, h1_b, h1_p, eidx, wsel, pos, cnt = _router(
        r1, row(ln1_g[0]), row(ln1_b[0]), router_w[0].T, router_bias[0].reshape(-1, 1), tmr)
    counts = cnt[:, 0]
    gstart = jnp.cumsum(counts) - counts
    dest = _dest_rows(gstart.astype(I32), eidx, pos)

    tmd = _tile(t, 256)
    dest_d = dest.T.reshape(t // tmd, 1, tmd * TOP_K)
    xs = _dispatch(dest_d, h1_p, tmd)
    tmg = _tile(t * TOP_K, 256)
    sched, n_visits = _visit_schedule(counts, t * TOP_K, tmg)
    act = _gmm_up(sched, n_visits, xs, expert_w_gate[0], expert_w_up[0], tmg)
    ys = _gmm_down(sched, n_visits, act, expert_w_down[0], tmg)

    f_s = shared_w_gate.shape[2]
    act_s = _proj_pair(h1_b, shared_w_gate[0], shared_w_up[0], 0, 0, f_s, tm, _tile(f_s, 256), "swiglu", BF16)
    tmc = _tile(t, 64)
    dest_c = dest.T.reshape(t // tmc, 1, tmc * TOP_K)
    out = _combine(dest_c, wsel.T, h1, act_s, shared_w_down[0], row(ln2_g[0]), row(ln2_b[0]), ys, tmc, alpha)
    return out[None]
```

```python
import functools

import jax
import jax.numpy as jnp
from jax import lax
from jax.experimental import pallas as pl
from jax.experimental.pallas import tpu as pltpu

F32 = jnp.float32
BF16 = jnp.bfloat16
I32 = jnp.int32
U32 = jnp.uint32

N_META = 16
CHUNK = 64
HEAD_DIM = 128
HEAD_GROUP = 8
TOP_K = 8
N_EXPERT_GROUPS = 8
TOPK_GROUPS = 4
ROUTED_SCALE = 2.5
LN_EPS = 1e-5
RMS_EPS = 1e-6
CONV_HALO = 32
SUBLANES = 8
VMEM_LIMIT = 56 * 1024 * 1024


def _params(n_axes, vmem=VMEM_LIMIT):
    return pltpu.CompilerParams(dimension_semantics=("arbitrary",) * n_axes, vmem_limit_bytes=vmem)


def _ln_rows(x, g, b):
    mu = jnp.mean(x, -1, keepdims=True)
    xc = x - mu
    var = jnp.mean(xc * xc, -1, keepdims=True)
    return xc * lax.rsqrt(var + LN_EPS) * g + b


def _sigmoid(x):
    return 1.0 / (1.0 + jnp.exp(-x))


def _silu(x):
    return x * _sigmoid(x)


def _dot(a, b):
    return jnp.dot(a, b, preferred_element_type=F32)


def _dot_nt(a, b, precision=None):
    return lax.dot_general(a, b, (((1,), (1,)), ((), ())), precision=precision, preferred_element_type=F32)


def _dot_tn(a, b):
    return lax.dot_general(a, b, (((0,), (0,)), ((), ())), preferred_element_type=F32)


def _embed_ln_kernel(x_ref, g_ref, b_ref, hf_ref, hb_ref):
    h = _ln_rows(x_ref[...], g_ref[...], b_ref[...])
    hf_ref[...] = h
    hb_ref[...] = h.astype(BF16)


def _embed_ln(x, g, b, tm, name):
    m, d = x.shape
    return pl.pallas_call(
        _embed_ln_kernel,
        grid=(m // tm,),
        in_specs=[pl.BlockSpec((tm, d), lambda i: (i, 0)),
                  pl.BlockSpec((1, d), lambda i: (0, 0)),
                  pl.BlockSpec((1, d), lambda i: (0, 0))],
        out_specs=[pl.BlockSpec((tm, d), lambda i: (i, 0)),
                   pl.BlockSpec((tm, d), lambda i: (i, 0))],
        out_shape=[jax.ShapeDtypeStruct((m, d), F32), jax.ShapeDtypeStruct((m, d), BF16)],
        compiler_params=_params(1),
        name=name,
    )(x, g, b)


def _inproj_plain_kernel(a_ref, am_ref, w_ref, o_ref, om_ref, wb_ref):
    @pl.when(pl.program_id(1) == 0)
    def _():
        wb_ref[...] = w_ref[...].astype(BF16)
        om_ref[...] = _dot_nt(am_ref[...], wb_ref[...])
    o_ref[...] = _dot_nt(a_ref[...], wb_ref[...])


def _inproj_glu_kernel(a_ref, am_ref, w1_ref, w2_ref, o_ref, om_ref, w1b_ref, w2b_ref):
    @pl.when(pl.program_id(1) == 0)
    def _():
        w1b_ref[...] = w1_ref[...].astype(BF16)
        w2b_ref[...] = w2_ref[...].astype(BF16)
        am = am_ref[...]
        om_ref[...] = _dot_nt(am, w1b_ref[...]) * _sigmoid(_dot_nt(am, w2b_ref[...]))
    a = a_ref[...]
    o_ref[...] = _dot_nt(a, w1b_ref[...]) * _sigmoid(_dot_nt(a, w2b_ref[...]))


def _decay_epilogue(p, alog, dtb, n_heads):
    x = p + dtb
    softplus = jnp.maximum(x, 0.0) + jnp.log1p(jnp.exp(-jnp.abs(x)))
    g = -jnp.exp(alog) * softplus
    lane = lax.broadcasted_iota(I32, p.shape, 1)
    return jnp.where(lane < n_heads, _sigmoid(p), g)


def _inproj_decay_kernel(a_ref, am_ref, w_ref, alog_ref, dtb_ref, o_ref, om_ref, wb_ref, *, n_heads):
    @pl.when(pl.program_id(1) == 0)
    def _():
        wb_ref[...] = w_ref[...].astype(BF16)
        om_ref[...] = _decay_epilogue(_dot_nt(am_ref[...], wb_ref[...]), alog_ref[...], dtb_ref[...], n_heads)
    o_ref[...] = _decay_epilogue(_dot_nt(a_ref[...], wb_ref[...]), alog_ref[...], dtb_ref[...], n_heads)


def _swiglu_kernel(a_ref, w1_ref, w2_ref, o_ref, w1b_ref, w2b_ref):
    @pl.when(pl.program_id(1) == 0)
    def _():
        w1b_ref[...] = w1_ref[...].astype(BF16)
        w2b_ref[...] = w2_ref[...].astype(BF16)
    a = a_ref[...]
    o_ref[...] = (_silu(_dot(a, w1b_ref[...])) * _dot(a, w2b_ref[...])).astype(o_ref.dtype)


def _proj_residual_kernel(a1_ref, a2_ref, w_ref, r_ref, o_ref, wb_ref, *, alpha, k1):
    @pl.when(pl.program_id(1) == 0)
    def _():
        wb_ref[...] = w_ref[...].astype(BF16)
    acc = _dot(a1_ref[...], wb_ref[0:k1, :]) + _dot(a2_ref[...], wb_ref[k1:, :])
    o_ref[...] = alpha * r_ref[...] + acc


def _inproj_plain(a, am, w_t, row0, n, tm, tn):
    m, k = a.shape
    mm = am.shape[0]
    rb = row0 // tn
    return pl.pallas_call(
        _inproj_plain_kernel,
        grid=(n // tn, m // tm),
        in_specs=[pl.BlockSpec((tm, k), lambda j, i: (i, 0)),
                  pl.BlockSpec((mm, k), lambda j, i: (0, 0)),
                  pl.BlockSpec((tn, k), lambda j, i: (j + rb, 0))],
        out_specs=[pl.BlockSpec((tm, tn), lambda j, i: (i, j)),
                   pl.BlockSpec((mm, tn), lambda j, i: (0, j))],
        out_shape=[jax.ShapeDtypeStruct((m, n), F32), jax.ShapeDtypeStruct((mm, n), F32)],
        scratch_shapes=[pltpu.VMEM((tn, k), BF16)],
        compiler_params=_params(2),
        name="inproj_qkvz",
    )(a, am, w_t)


def _inproj_glu(a, am, w_t, row1, row2, n, tm, tn):
    m, k = a.shape
    mm = am.shape[0]
    r1, r2 = row1 // tn, row2 // tn
    return pl.pallas_call(
        _inproj_glu_kernel,
        grid=(n // tn, m // tm),
        in_specs=[pl.BlockSpec((tm, k), lambda j, i: (i, 0)),
                  pl.BlockSpec((mm, k), lambda j, i: (0, 0)),
                  pl.BlockSpec((tn, k), lambda j, i: (j + r1, 0)),
                  pl.BlockSpec((tn, k), lambda j, i: (j + r2, 0))],
        out_specs=[pl.BlockSpec((tm, tn), lambda j, i: (i, j)),
                   pl.BlockSpec((mm, tn), lambda j, i: (0, j))],
        out_shape=[jax.ShapeDtypeStruct((m, n), F32), jax.ShapeDtypeStruct((mm, n), F32)],
        scratch_shapes=[pltpu.VMEM((tn, k), BF16), pltpu.VMEM((tn, k), BF16)],
        compiler_params=_params(2),
        name="inproj_glu",
    )(a, am, w_t, w_t)


def _inproj_decay(a, am, w_ba_t, alog_row, dtb_row, tm, n_heads):
    m, k = a.shape
    mm = am.shape[0]
    n = w_ba_t.shape[0]
    return pl.pallas_call(
        functools.partial(_inproj_decay_kernel, n_heads=n_heads),
        grid=(1, m // tm),
        in_specs=[pl.BlockSpec((tm, k), lambda j, i: (i, 0)),
                  pl.BlockSpec((mm, k), lambda j, i: (0, 0)),
                  pl.BlockSpec((n, k), lambda j, i: (0, 0)),
                  pl.BlockSpec((1, n), lambda j, i: (0, 0)),
                  pl.BlockSpec((1, n), lambda j, i: (0, 0))],
        out_specs=[pl.BlockSpec((tm, n), lambda j, i: (i, 0)),
                   pl.BlockSpec((mm, n), lambda j, i: (0, 0))],
        out_shape=[jax.ShapeDtypeStruct((m, n), F32), jax.ShapeDtypeStruct((mm, n), F32)],
        scratch_shapes=[pltpu.VMEM((n, k), BF16)],
        compiler_params=_params(2),
        name="inproj_decay",
    )(a, am, w_ba_t, alog_row, dtb_row)


def _swiglu(a, w1, w2, tm, tn, out_dtype):
    m, k = a.shape
    n = w1.shape[1]
    return pl.pallas_call(
        _swiglu_kernel,
        grid=(n // tn, m // tm),
        in_specs=[pl.BlockSpec((tm, k), lambda j, i: (i, 0)),
                  pl.BlockSpec((k, tn), lambda j, i: (0, j)),
                  pl.BlockSpec((k, tn), lambda j, i: (0, j))],
        out_specs=pl.BlockSpec((tm, tn), lambda j, i: (i, j)),
        out_shape=jax.ShapeDtypeStruct((m, n), out_dtype),
        scratch_shapes=[pltpu.VMEM((k, tn), BF16), pltpu.VMEM((k, tn), BF16)],
        compiler_params=_params(2),
        name="shared_swiglu",
    )(a, w1, w2)


def _proj_residual(a1, a2, w, resid, alpha, tm, tn):
    m, k1 = a1.shape
    k2 = a2.shape[1]
    n = w.shape[1]
    return pl.pallas_call(
        functools.partial(_proj_residual_kernel, alpha=alpha, k1=k1),
        grid=(n // tn, m // tm),
        in_specs=[pl.BlockSpec((tm, k1), lambda j, i: (i, 0)),
                  pl.BlockSpec((tm, k2), lambda j, i: (i, 0)),
                  pl.BlockSpec((k1 + k2, tn), lambda j, i: (0, j)),
                  pl.BlockSpec((tm, tn), lambda j, i: (i, j))],
        out_specs=pl.BlockSpec((tm, tn), lambda j, i: (i, j)),
        out_shape=jax.ShapeDtypeStruct((m, n), F32),
        scratch_shapes=[pltpu.VMEM((k1 + k2, tn), BF16)],
        compiler_params=_params(2),
        name="outproj_residual",
    )(a1, a2, w, resid)


def _conv_module_kernel(c_ref, cm_ref, w_ref, b_ref, g_ref, be_ref, o_ref, buf_ref, *, ts, kw):
    i = pl.program_id(0)
    ch = c_ref.shape[1]

    @pl.when(i == 0)
    def _():
        buf_ref[0:CONV_HALO - N_META, :] = jnp.zeros((CONV_HALO - N_META, ch), F32)
        buf_ref[CONV_HALO - N_META:CONV_HALO, :] = cm_ref[...]

    @pl.when(i > 0)
    def _():
        buf_ref[0:CONV_HALO, :] = buf_ref[ts:ts + CONV_HALO, :]

    buf_ref[CONV_HALO:CONV_HALO + ts, :] = c_ref[...]
    base = CONV_HALO - (kw - 1)
    acc = jnp.zeros((ts, ch), F32) + b_ref[...]
    for j in range(kw):
        acc = acc + w_ref[j:j + 1, :] * buf_ref[base + j:base + j + ts, :]
    y = _ln_rows(acc, g_ref[...], be_ref[...])
    o_ref[...] = _silu(y).astype(o_ref.dtype)


def _conv_module(c, c_meta, w, b, g, be, ts):
    t, ch = c.shape
    kw = w.shape[0]
    assert kw - 1 <= CONV_HALO and N_META <= CONV_HALO and ts >= CONV_HALO
    return pl.pallas_call(
        functools.partial(_conv_module_kernel, ts=ts, kw=kw),
        grid=(t // ts,),
        in_specs=[pl.BlockSpec((ts, ch), lambda i: (i, 0)),
                  pl.BlockSpec((N_META, ch), lambda i: (0, 0)),
                  pl.BlockSpec((kw, ch), lambda i: (0, 0)),
                  pl.BlockSpec((1, ch), lambda i: (0, 0)),
                  pl.BlockSpec((1, ch), lambda i: (0, 0)),
                  pl.BlockSpec((1, ch), lambda i: (0, 0))],
        out_specs=pl.BlockSpec((ts, ch), lambda i: (i, 0)),
        out_shape=jax.ShapeDtypeStruct((t, ch), BF16),
        scratch_shapes=[pltpu.VMEM((CONV_HALO + ts, ch), F32)],
        compiler_params=_params(1),
        name="conv_module",
    )(c, c_meta, w, b, g, be)


def _gdn_kernel(x_ref, m_ref, bg_ref, gt_ref, cw_ref, gn_ref, o_ref, cbuf_ref, act_ref, s_ref,
                *, n_heads):
    c = pl.program_id(0)
    gw = n_heads * HEAD_DIM
    C = CHUNK

    @pl.when(c == 0)
    def _():
        cbuf_ref[0:8, :] = jnp.zeros((8, 3 * gw), F32)
        s_ref[...] = jnp.zeros(s_ref.shape, F32)

    @pl.when(c > 0)
    def _():
        cbuf_ref[0:8, :] = cbuf_ref[C:C + 8, :]

    raw = jnp.where(c == 0, m_ref[...], x_ref[...])
    cbuf_ref[8:8 + C, :] = raw[:, 0:3 * gw]
    kw = cw_ref.shape[0]
    conv = jnp.zeros((C, 3 * gw), F32)
    for j in range(kw):
        conv = conv + cw_ref[j:j + 1, :] * cbuf_ref[9 - kw + j:9 - kw + j + C, :]
    act_ref[...] = _silu(conv)

    bg = bg_ref[...]
    beta_all = bg[:, 0:n_heads]
    g_all = bg[:, n_heads:2 * n_heads]
    row = lax.broadcasted_iota(I32, (C, C), 0)
    col = lax.broadcasted_iota(I32, (C, C), 1)
    incl = row >= col
    strict = row > col
    tri_incl = jnp.where(incl, 1.0, 0.0).astype(F32)
    tri_upper = jnp.where(row <= col, 1.0, 0.0).astype(F32)
    hp = lax.Precision.HIGHEST
    gc_cols = jnp.dot(tri_incl, g_all, precision=hp, preferred_element_type=F32)
    gc_rows = jnp.dot(gt_ref[0], tri_upper, precision=hp, preferred_element_type=F32)
    eye = jnp.where(row == col, 1.0, 0.0).astype(F32)
    right_half = lax.broadcasted_iota(I32, (C, 2 * C), 1) >= C
    gn = gn_ref[...]

    for h0 in range(0, n_heads, HEAD_GROUP):
        heads = range(h0, min(h0 + HEAD_GROUP, n_heads))
        st = {h: {} for h in heads}

        for h in heads:
            d = st[h]
            q = act_ref[:, h * HEAD_DIM:(h + 1) * HEAD_DIM]
            k = act_ref[:, gw + h * HEAD_DIM:gw + (h + 1) * HEAD_DIM]
            q = q * lax.rsqrt(jnp.sum(q * q, -1, keepdims=True) + RMS_EPS) * (HEAD_DIM ** -0.5)
            k = k * lax.rsqrt(jnp.sum(k * k, -1, keepdims=True) + RMS_EPS)
            beta = beta_all[:, h:h + 1]
            gcol = gc_cols[:, h:h + 1]
            grow = gc_rows[h:h + 1, :]
            g_last = gc_cols[C - 1:C, h:h + 1]
            d["decay"] = jnp.where(incl, jnp.exp(jnp.where(incl, gcol - grow, 0.0)), 0.0)
            egc = jnp.exp(gcol)
            kb = k * beta
            d["a1"] = _dot_nt(jnp.concatenate([kb, q], axis=0).astype(BF16), k.astype(BF16))
            v = act_ref[:, 2 * gw + h * HEAD_DIM:2 * gw + (h + 1) * HEAD_DIM]
            d["rhs"] = jnp.concatenate([v * beta, kb * egc], axis=1).astype(BF16)
            d["q_dec"] = q * egc
            d["k_dec"] = (k * jnp.exp(g_last - gcol)).astype(BF16)
            d["s_scale"] = jnp.exp(g_last)

        for h in heads:
            d = st[h]
            lmat = jnp.where(strict, d["a1"][0:C] * d["decay"], 0.0)
            d["attn"] = (d["a1"][C:2 * C] * d["decay"]).astype(BF16)
            d["wmat"] = jnp.concatenate([-lmat, eye], axis=1)

        for _ in range(6):
            for h in heads:
                d = st[h]
                wmat = d["wmat"]
                r = _dot(wmat[:, 0:C].astype(BF16), wmat.astype(BF16))
                d["wmat"] = r + jnp.where(right_half, wmat, 0.0)

        for h in heads:
            d = st[h]
            tinv = jnp.where(right_half, d["wmat"], 0.0).astype(BF16)
            d["sol"] = _dot(tinv, jnp.concatenate([d["rhs"], d["rhs"]], axis=0))

        for h in heads:
            d = st[h]
            w = d["sol"][:, HEAD_DIM:2 * HEAD_DIM]
            d["ws_qs"] = _dot(jnp.concatenate([w, d["q_dec"]], axis=0).astype(BF16), s_ref[h].astype(BF16))

        for h in heads:
            d = st[h]
            vnb = (d["sol"][:, 0:HEAD_DIM] - d["ws_qs"][0:C]).astype(BF16)
            d["o"] = d["ws_qs"][C:2 * C] + _dot(d["attn"], vnb)
            s_ref[h] = s_ref[h] * d["s_scale"] + _dot_tn(d["k_dec"], vnb)

        for h in heads:
            o = st[h]["o"]
            sl = slice(h * HEAD_DIM, (h + 1) * HEAD_DIM)
            o = o * lax.rsqrt(jnp.mean(o * o, -1, keepdims=True) + RMS_EPS) * gn
            z = jnp.where(c == 0, m_ref[:, 3 * gw + h * HEAD_DIM:3 * gw + (h + 1) * HEAD_DIM],
                          x_ref[:, 3 * gw + h * HEAD_DIM:3 * gw + (h + 1) * HEAD_DIM])
            o_ref[:, sl] = (o * _silu(z)).astype(o_ref.dtype)


def _gdn(qkvz, qkvz_meta, bg_all, gt3, conv_w, gn_row, n_heads):
    t = qkvz.shape[0]
    gw = n_heads * HEAD_DIM
    n_chunks = t // CHUNK + 1
    kw = conv_w.shape[0]
    assert kw <= 8
    xmap = lambda c: (jnp.maximum(c - 1, 0), 0)
    return pl.pallas_call(
        functools.partial(_gdn_kernel, n_heads=n_heads),
        grid=(n_chunks,),
        in_specs=[pl.BlockSpec((CHUNK, 4 * gw), xmap),
                  pl.BlockSpec((CHUNK, 4 * gw), lambda c: (0, 0)),
                  pl.BlockSpec((CHUNK, 2 * n_heads), lambda c: (c, 0)),
                  pl.BlockSpec((1, n_heads, CHUNK), lambda c: (c, 0, 0)),
                  pl.BlockSpec((kw, 3 * gw), lambda c: (0, 0)),
                  pl.BlockSpec((1, HEAD_DIM), lambda c: (0, 0))],
        out_specs=pl.BlockSpec((CHUNK, gw), xmap),
        out_shape=jax.ShapeDtypeStruct((t, gw), BF16),
        scratch_shapes=[pltpu.VMEM((CHUNK + 8, 3 * gw), F32),
                        pltpu.VMEM((CHUNK, 3 * gw), F32),
                        pltpu.VMEM((n_heads, HEAD_DIM, HEAD_DIM), F32)],
        compiler_params=_params(1),
        name="gated_delta_rule",
    )(qkvz, qkvz_meta, bg_all, gt3, conv_w, gn_row)


def _router_kernel(r_ref, g_ref, b_ref, rwt_ref, bias_ref, h1_ref, h1b_ref, h1p_ref,
                   eidx_ref, wsel_ref, pos_ref, cnt_ref, carry_ref, *, tm, n_exp):
    i = pl.program_id(0)
    d = r_ref.shape[1]
    per = n_exp // N_EXPERT_GROUPS

    @pl.when(i == 0)
    def _():
        carry_ref[...] = jnp.zeros(carry_ref.shape, F32)

    h1 = _ln_rows(r_ref[...], g_ref[...], b_ref[...])
    h1_ref[...] = h1
    hb = h1.astype(BF16)
    h1b_ref[...] = hb
    hi = lax.bitcast_convert_type(hb[:, 0:d // 2].astype(F32), U32)
    lo = lax.bitcast_convert_type(hb[:, d // 2:].astype(F32), U32)
    h1p_ref[...] = hi | (lo >> 16)

    logits = _dot_nt(rwt_ref[...], h1, precision=lax.Precision.HIGHEST)
    scores = _sigmoid(logits)
    biased = scores + bias_ref[...]
    neg_inf = F32(-jnp.inf)

    gs_rows = []
    sub = lax.broadcasted_iota(I32, (per, tm), 0)
    for g in range(N_EXPERT_GROUPS):
        xg = biased[g * per:(g + 1) * per, :]
        m1 = jnp.max(xg, axis=0, keepdims=True)
        first = jnp.min(jnp.where(xg == m1, sub, per), axis=0, keepdims=True)
        m2 = jnp.max(jnp.where(sub == first, neg_inf, xg), axis=0, keepdims=True)
        gs_rows.append(m1 + m2)
    gs = jnp.concatenate(gs_rows, axis=0)
    gid = lax.broadcasted_iota(I32, (N_EXPERT_GROUPS, tm), 0)
    grank = jnp.zeros((N_EXPERT_GROUPS, tm), F32)
    for g in range(N_EXPERT_GROUPS):
        rowv = gs[g:g + 1, :]
        grank = grank + jnp.where(rowv > gs, 1.0, jnp.where((rowv == gs) & (gid > g), 1.0, 0.0))
    gsel = jnp.where(grank < TOPK_GROUPS, 1.0, 0.0)
    emask = jnp.concatenate(
        [jnp.broadcast_to(gsel[g:g + 1, :], (per, tm)) for g in range(N_EXPERT_GROUPS)], axis=0)
    masked = jnp.where(emask > 0.0, biased, neg_inf)
    eid = lax.broadcasted_iota(I32, (n_exp, tm), 0)
    erank = jnp.zeros((n_exp, tm), F32)
    for e in range(n_exp):
        rowv = masked[e:e + 1, :]
        erank = erank + jnp.where(rowv > masked, 1.0, jnp.where((rowv == masked) & (eid > e), 1.0, 0.0))
    sel = jnp.where(erank < TOP_K, emask, 0.0)
    wdense = scores * sel
    gate = wdense / jnp.sum(wdense, axis=0, keepdims=True) * ROUTED_SCALE

    er = lax.broadcasted_iota(I32, (n_exp, n_exp), 0)
    ec = lax.broadcasted_iota(I32, (n_exp, n_exp), 1)
    lower = jnp.where(er > ec, 1.0, 0.0).astype(BF16)
    selb = sel.astype(BF16)
    slot = _dot(lower, selb)
    tr = lax.broadcasted_iota(I32, (tm, tm), 0)
    tc = lax.broadcasted_iota(I32, (tm, tm), 1)
    upper = jnp.where(tr < tc, 1.0, 0.0).astype(BF16)
    carry = carry_ref[:, 0:1]
    pos = _dot(selb, upper) + carry
    eid_f = eid.astype(F32)
    e_rows, w_rows, p_rows = [], [], []
    for k in range(TOP_K):
        mk = jnp.where(slot == float(k), sel, 0.0)
        e_rows.append(jnp.sum(mk * eid_f, axis=0, keepdims=True))
        w_rows.append(jnp.sum(mk * gate, axis=0, keepdims=True))
        p_rows.append(jnp.sum(mk * pos, axis=0, keepdims=True))
    eidx_ref[...] = jnp.concatenate(e_rows, axis=0).astype(I32)
    wsel_ref[...] = jnp.concatenate(w_rows, axis=0)
    pos_ref[...] = jnp.concatenate(p_rows, axis=0).astype(I32)
    new_carry = carry + jnp.sum(sel, axis=1, keepdims=True)
    carry_ref[...] = jnp.broadcast_to(new_carry, carry_ref.shape)
    cnt_ref[...] = jnp.broadcast_to(new_carry, cnt_ref.shape).astype(I32)


def _router(r, g, b, rwt, bias_col, tm):
    t, d = r.shape
    n_exp = rwt.shape[0]
    row = lambda i: (i, 0)
    colb = lambda i: (0, i)
    fixed = lambda i: (0, 0)
    return pl.pallas_call(
        functools.partial(_router_kernel, tm=tm, n_exp=n_exp),
        grid=(t // tm,),
        in_specs=[pl.BlockSpec((tm, d), row),
                  pl.BlockSpec((1, d), fixed),
                  pl.BlockSpec((1, d), fixed),
                  pl.BlockSpec((n_exp, d), fixed),
                  pl.BlockSpec((n_exp, 1), fixed)],
        out_specs=[pl.BlockSpec((tm, d), row),
                   pl.BlockSpec((tm, d), row),
                   pl.BlockSpec((tm, d // 2), row),
                   pl.BlockSpec((TOP_K, tm), colb),
                   pl.BlockSpec((TOP_K, tm), colb),
                   pl.BlockSpec((TOP_K, tm), colb),
                   pl.BlockSpec((n_exp, 128), fixed)],
        out_shape=[jax.ShapeDtypeStruct((t, d), F32),
                   jax.ShapeDtypeStruct((t, d), BF16),
                   jax.ShapeDtypeStruct((t, d // 2), U32),
                   jax.ShapeDtypeStruct((TOP_K, t), I32),
                   jax.ShapeDtypeStruct((TOP_K, t), F32),
                   jax.ShapeDtypeStruct((TOP_K, t), I32),
                   jax.ShapeDtypeStruct((n_exp, 128), I32)],
        scratch_shapes=[pltpu.VMEM((n_exp, 128), F32)],
        compiler_params=_params(1),
        name="ln1_router",
    )(r, g, b, rwt, bias_col)


def _dest_kernel(gstart_ref, eidx_ref, pos_ref, dest_ref, *, n_exp):
    eidx = eidx_ref[...]
    acc = pos_ref[...]
    for e in range(n_exp):
        acc = acc + jnp.where(eidx == e, gstart_ref[e], 0)
    dest_ref[...] = acc


def _dest_rows(gstart, eidx, pos):
    n_exp = gstart.shape[0]
    return pl.pallas_call(
        functools.partial(_dest_kernel, n_exp=n_exp),
        grid_spec=pltpu.PrefetchScalarGridSpec(
            num_scalar_prefetch=1,
            grid=(1,),
            in_specs=[pl.BlockSpec(eidx.shape, lambda i, gs: (0, 0)),
                      pl.BlockSpec(pos.shape, lambda i, gs: (0, 0))],
            out_specs=pl.BlockSpec(eidx.shape, lambda i, gs: (0, 0))),
        out_shape=jax.ShapeDtypeStruct(eidx.shape, I32),
        compiler_params=_params(1),
        name="dest_rows",
    )(gstart, eidx, pos)


def _dispatch_kernel(dest_ref, hp_ref, xs_ref, sem, *, tm):
    def issue(t, carry):
        for k in range(TOP_K):
            d = dest_ref[0, 0, t * TOP_K + k]
            pltpu.make_async_copy(hp_ref.at[pl.ds(t, 1)], xs_ref.at[pl.ds(d, 1)], sem).start()
        return carry
    lax.fori_loop(0, tm, issue, 0)
    pltpu.make_async_copy(xs_ref.at[pl.ds(0, tm * TOP_K)], xs_ref.at[pl.ds(0, tm * TOP_K)], sem).wait()


def _dispatch(dest_tiles, h1p, tm):
    t, dw = h1p.shape
    return pl.pallas_call(
        functools.partial(_dispatch_kernel, tm=tm),
        grid=(t // tm,),
        in_specs=[pl.BlockSpec((1, 1, tm * TOP_K), lambda i: (i, 0, 0), memory_space=pltpu.SMEM),
                  pl.BlockSpec((tm, dw), lambda i: (i, 0))],
        out_specs=pl.BlockSpec(memory_space=pl.ANY),
        out_shape=jax.ShapeDtypeStruct((t * TOP_K, dw), U32),
        scratch_shapes=[pltpu.SemaphoreType.DMA(())],
        compiler_params=_params(1),
        name="dispatch_rows",
    )(dest_tiles, h1p)


def _unpack_rows(xu):
    hi = lax.bitcast_convert_type(xu & jnp.uint32(0xFFFF0000), F32).astype(BF16)
    lo = lax.bitcast_convert_type(xu << 16, F32).astype(BF16)
    return hi, lo


def _visit_subblocks(vt_ref, ve_ref, vft_ref, gs_ref, ge_ref, v, tm, sb, o_ref, compute):
    e = ve_ref[v]
    gs, ge = gs_ref[e], ge_ref[e]
    first = vft_ref[v] == 1
    width = o_ref.shape[1]
    for s in range(tm // sb):
        r0 = vt_ref[v] * tm + s * sb
        hit = (r0 < ge) & (r0 + sb > gs)
        rows = pl.ds(s * sb, sb)

        @pl.when(hit)
        def _():
            y = compute(s).astype(o_ref.dtype)
            ridx = r0 + lax.broadcasted_iota(I32, (sb, width), 0)
            mask = (ridx >= gs) & (ridx < ge)

            @pl.when(first)
            def _():
                o_ref[rows, :] = jnp.where(mask, y, jnp.zeros_like(y))

            @pl.when(jnp.logical_not(first))
            def _():
                o_ref[rows, :] = jnp.where(mask, y, o_ref[rows, :])

        @pl.when(jnp.logical_not(hit) & first)
        def _():
            o_ref[rows, :] = jnp.zeros((sb, width), o_ref.dtype)


def _gmm_up_kernel(vt_ref, ve_ref, vft_ref, vfe_ref, gs_ref, ge_ref, nv_ref,
                   xs_ref, wg_ref, wu_ref, o_ref, wgb_ref, wub_ref, *, tm, sb):
    v = pl.program_id(0)

    @pl.when(v < nv_ref[0])
    def _():
        @pl.when(vfe_ref[v] == 1)
        def _():
            wgb_ref[...] = wg_ref[0].astype(BF16)
            wub_ref[...] = wu_ref[0].astype(BF16)

        def compute(s):
            hi, lo = _unpack_rows(xs_ref[s * sb:(s + 1) * sb, :])
            kh = hi.shape[1]
            hg = _dot(hi, wgb_ref[0:kh, :]) + _dot(lo, wgb_ref[kh:, :])
            hu = _dot(hi, wub_ref[0:kh, :]) + _dot(lo, wub_ref[kh:, :])
            return _silu(hg) * hu

        _visit_subblocks(vt_ref, ve_ref, vft_ref, gs_ref, ge_ref, v, tm, sb, o_ref, compute)


def _gmm_down_kernel(vt_ref, ve_ref, vft_ref, vfe_ref, gs_ref, ge_ref, nv_ref,
                     a_ref, wd_ref, o_ref, wdb_ref, *, tm, sb):
    v = pl.program_id(0)

    @pl.when(v < nv_ref[0])
    def _():
        @pl.when(vfe_ref[v] == 1)
        def _():
            wdb_ref[...] = wd_ref[0].astype(BF16)

        def compute(s):
            return _dot(a_ref[s * sb:(s + 1) * sb, :], wdb_ref[...])

        _visit_subblocks(vt_ref, ve_ref, vft_ref, gs_ref, ge_ref, v, tm, sb, o_ref, compute)


def _visit_schedule(counts, n_rows, tm):
    n_exp = counts.shape[0]
    n_tiles = n_rows // tm
    n_visits = n_tiles + n_exp - 1
    ends = jnp.cumsum(counts)
    starts = ends - counts
    first_tile = starts // tm
    last_tile = jnp.maximum(ends - 1, 0) // tm
    tiles_e = jnp.where(counts > 0, last_tile - first_tile + 1, 0)
    vend = jnp.cumsum(tiles_e)
    vstart = vend - tiles_e
    total = vend[-1]
    v = jnp.arange(n_visits, dtype=I32)
    vc = jnp.minimum(v, total - 1)
    ve = jnp.searchsorted(vend, vc, side="right").astype(I32)
    vt = (vc - vstart[ve] + first_tile[ve]).astype(I32)
    prev_t = jnp.concatenate([jnp.full((1,), -1, I32), vt[:-1]])
    prev_e = jnp.concatenate([jnp.full((1,), -1, I32), ve[:-1]])
    vft = (vt != prev_t).astype(I32)
    vfe = (ve != prev_e).astype(I32)
    return (vt, ve, vft, vfe, starts.astype(I32), ends.astype(I32), total.reshape(1).astype(I32)), n_visits


def _gmm_up(sched, n_visits, xs, wg, wu, tm, sb):
    n, dw = xs.shape
    n_exp, d, f = wg.shape
    return pl.pallas_call(
        functools.partial(_gmm_up_kernel, tm=tm, sb=sb),
        grid_spec=pltpu.PrefetchScalarGridSpec(
            num_scalar_prefetch=7,
            grid=(n_visits,),
            in_specs=[pl.BlockSpec((tm, dw), lambda v, vt, ve, *_: (vt[v], 0)),
                      pl.BlockSpec((1, d, f), lambda v, vt, ve, *_: (ve[v], 0, 0)),
                      pl.BlockSpec((1, d, f), lambda v, vt, ve, *_: (ve[v], 0, 0))],
            out_specs=pl.BlockSpec((tm, f), lambda v, vt, ve, *_: (vt[v], 0)),
            scratch_shapes=[pltpu.VMEM((d, f), BF16), pltpu.VMEM((d, f), BF16)]),
        out_shape=jax.ShapeDtypeStruct((n, f), BF16),
        compiler_params=_params(1),
        name="experts_gate_up",
    )(*sched, xs, wg, wu)


def _gmm_down(sched, n_visits, act, wd, tm, sb):
    n, f = act.shape
    n_exp, _, d = wd.shape
    return pl.pallas_call(
        functools.partial(_gmm_down_kernel, tm=tm, sb=sb),
        grid_spec=pltpu.PrefetchScalarGridSpec(
            num_scalar_prefetch=7,
            grid=(n_visits,),
            in_specs=[pl.BlockSpec((tm, f), lambda v, vt, ve, *_: (vt[v], 0)),
                      pl.BlockSpec((1, f, d), lambda v, vt, ve, *_: (ve[v], 0, 0))],
            out_specs=pl.BlockSpec((tm, d), lambda v, vt, ve, *_: (vt[v], 0)),
            scratch_shapes=[pltpu.VMEM((f, d), BF16)]),
        out_shape=jax.ShapeDtypeStruct((n, d), F32),
        compiler_params=_params(1),
        name="experts_down",
    )(*sched, act, wd)


def _combine_kernel(dcur_ref, dnxt_ref, w_ref, h1_ref, as_ref, wsd_ref, g_ref, b_ref, ys_ref, o_ref,
                    gbuf_ref, wsdb_ref, ybuf_ref, sem, *, tm, alpha):
    i = pl.program_id(0)
    n = pl.num_programs(0)
    slot = i % 2
    groups = tm // SUBLANES

    def issue_group(dref, dst_slot, r):
        for tt in range(SUBLANES):
            t = r * SUBLANES + tt
            for k in range(TOP_K):
                d = dref[0, 0, t * TOP_K + k]
                pltpu.make_async_copy(ys_ref.at[pl.ds(d, 1)], gbuf_ref.at[dst_slot, pl.ds(k * tm + t, 1)],
                                      sem.at[dst_slot]).start()

    @pl.when(i == 0)
    def _():
        wsdb_ref[...] = wsd_ref[...].astype(BF16)

        def first(r, carry):
            issue_group(dcur_ref, 0, r)
            return carry
        lax.fori_loop(0, groups, first, 0)

    ybuf_ref[...] = alpha * h1_ref[...] + _dot(as_ref[...], wsdb_ref[...])
    pltpu.make_async_copy(ys_ref.at[pl.ds(0, tm * TOP_K)], gbuf_ref.at[slot], sem.at[slot]).wait()
    has_next = i + 1 < n
    gam, bet = g_ref[...], b_ref[...]

    def reduce_group(r, carry):
        @pl.when(has_next)
        def _():
            issue_group(dnxt_ref, 1 - slot, r)
        r8 = pl.multiple_of(r * SUBLANES, SUBLANES)
        rows = pl.ds(r8, SUBLANES)
        y = ybuf_ref[rows, :]
        w = w_ref[rows, :]
        for k in range(TOP_K):
            y = y + w[:, k:k + 1] * gbuf_ref[slot, pl.ds(k * tm + r8, SUBLANES), :]
        o_ref[rows, :] = _ln_rows(y, gam, bet)
        return carry
    lax.fori_loop(0, groups, reduce_group, 0)


def _combine(dest_tiles, w_tok, h1, act_s, wsd, g, b, ys, tm, alpha):
    t, d = h1.shape
    f = act_s.shape[1]
    n = t // tm
    row = lambda i: (i, 0)
    fixed = lambda i: (0, 0)
    return pl.pallas_call(
        functools.partial(_combine_kernel, tm=tm, alpha=alpha),
        grid=(n,),
        in_specs=[pl.BlockSpec((1, 1, tm * TOP_K), lambda i: (i, 0, 0), memory_space=pltpu.SMEM),
                  pl.BlockSpec((1, 1, tm * TOP_K), lambda i: (jnp.minimum(i + 1, n - 1), 0, 0),
                               memory_space=pltpu.SMEM),
                  pl.BlockSpec((tm, TOP_K), row),
                  pl.BlockSpec((tm, d), row),
                  pl.BlockSpec((tm, f), row),
                  pl.BlockSpec((f, d), fixed),
                  pl.BlockSpec((1, d), fixed),
                  pl.BlockSpec((1, d), fixed),
                  pl.BlockSpec(memory_space=pl.ANY)],
        out_specs=pl.BlockSpec((tm, d), row),
        out_shape=jax.ShapeDtypeStruct((t, d), F32),
        scratch_shapes=[pltpu.VMEM((2, tm * TOP_K, d), F32),
                        pltpu.VMEM((f, d), BF16),
                        pltpu.VMEM((tm, d), F32),
                        pltpu.SemaphoreType.DMA((2,))],
        compiler_params=_params(1),
        name="combine_ln2",
    )(dest_tiles, dest_tiles, w_tok, h1, act_s, wsd, g, b, ys)


def _tile(n, pref):
    t = min(n, pref)
    assert n % t == 0, (n, pref)
    return t


def kernel(x, meta_tokens, emb_ln_g, emb_ln_b, w_in, conv_w, conv_b, conv_ln_g, conv_ln_b, short_conv_w, a_log, dt_bias, gdn_norm_g, w_out, ln1_g, ln1_b, router_w, router_bias, expert_w_gate, expert_w_up, expert_w_down, shared_w_gate, shared_w_up, shared_w_down, ln2_g, ln2_b):
    depth = w_in.shape[0]
    assert depth == 1 and x.shape[0] == 1, "single layer, single sequence"
    t, d = x.shape[1], x.shape[2]
    cw = conv_w.shape[2]
    n_heads = a_log.shape[1]
    gw = n_heads * HEAD_DIM
    alpha = (2.0 * depth) ** 0.25
    assert meta_tokens.shape[0] == N_META and t % CHUNK == 0 and w_in.shape[2] == 2 * cw + 4 * gw + 2 * n_heads
    row = lambda a: a.reshape(1, -1)

    h_f, h_b = _embed_ln(x[0], row(emb_ln_g), row(emb_ln_b), _tile(t, 256), "embed_ln")
    _, hm_b = _embed_ln(meta_tokens, row(emb_ln_g), row(emb_ln_b), N_META, "embed_ln_meta")

    w_t = jnp.transpose(w_in[0])
    tm = _tile(t, 1024)
    zeros_h = jnp.zeros((n_heads,), F32)
    alog_row = row(jnp.concatenate([zeros_h, a_log[0]]))
    dtb_row = row(jnp.concatenate([zeros_h, dt_bias[0]]))
    c_x, c_m = _inproj_glu(h_b, hm_b, w_t, 0, cw, cw, tm, _tile(cw, 256))
    qkvz_x, qkvz_m = _inproj_plain(h_b, hm_b, w_t, 2 * cw, 4 * gw, tm, _tile(4 * gw, 512))
    bg_x, bg_m = _inproj_decay(h_b, hm_b, w_t[2 * cw + 4 * gw:], alog_row, dtb_row, tm, n_heads)

    y_conv = _conv_module(c_x, c_m, conv_w[0], row(conv_b[0]), row(conv_ln_g[0]), row(conv_ln_b[0]),
                          _tile(t, 128))

    pad = CHUNK - N_META
    qkvz_m64 = jnp.pad(qkvz_m, ((pad, 0), (0, 0)))
    bg_all = jnp.concatenate([jnp.pad(bg_m, ((pad, 0), (0, 0))), bg_x], axis=0)
    gt3 = bg_all[:, n_heads:].reshape(-1, CHUNK, n_heads).transpose(0, 2, 1)
    y_gdn = _gdn(qkvz_x, qkvz_m64, bg_all, gt3, short_conv_w[0], row(gdn_norm_g[0]), n_heads)

    r1 = _proj_residual(y_conv, y_gdn, w_out[0], h_f, alpha, tm, _tile(d, 512))
    h1, h1_b, h1_p, eidx, wsel, pos, cnt = _router(
        r1, row(ln1_g[0]), row(ln1_b[0]), router_w[0].T, router_bias[0].reshape(-1, 1), _tile(t, 256))
    counts = cnt[:, 0]
    gstart = jnp.cumsum(counts) - counts
    dest = _dest_rows(gstart.astype(I32), eidx, pos)

    tmd = _tile(t, 256)
    xs = _dispatch(dest.T.reshape(t // tmd, 1, tmd * TOP_K), h1_p, tmd)
    tmg = _tile(t * TOP_K, 512)
    sbg = _tile(tmg, 128)
    sched, n_visits = _visit_schedule(counts, t * TOP_K, tmg)
    act = _gmm_up(sched, n_visits, xs, expert_w_gate[0], expert_w_up[0], tmg, sbg)
    ys = _gmm_down(sched, n_visits, act, expert_w_down[0], tmg, sbg)

    act_s = _swiglu(h1_b, shared_w_gate[0], shared_w_up[0], tm, _tile(shared_w_gate.shape[2], 256), BF16)
    tmc = _tile(t, 64)
    out = _combine(dest.T.reshape(t // tmc, 1, tmc * TOP_K), wsel.T, h1, act_s, shared_w_down[0],
                   row(ln2_g[0]), row(ln2_b[0]), ys, tmc, alpha)
    return out[None]
```

```python
import functools

import jax
import jax.numpy as jnp
from jax import lax
from jax.experimental import pallas as pl
from jax.experimental.pallas import tpu as pltpu

F32 = jnp.float32
BF16 = jnp.bfloat16
I32 = jnp.int32
U32 = jnp.uint32

N_META = 16
CHUNK = 64
HEAD_DIM = 128
HEAD_GROUP = 8
TOP_K = 8
N_EXPERT_GROUPS = 8
TOPK_GROUPS = 4
ROUTED_SCALE = 2.5
LN_EPS = 1e-5
RMS_EPS = 1e-6
CONV_HALO = 32
CONV_ROWS = 32
CONV_LANES = 512
SUBLANES = 8
VMEM_LIMIT = 56 * 1024 * 1024


def _params(n_axes, vmem=VMEM_LIMIT):
    return pltpu.CompilerParams(dimension_semantics=("arbitrary",) * n_axes, vmem_limit_bytes=vmem)


def _ln_rows(x, g, b):
    mu = jnp.mean(x, -1, keepdims=True)
    xc = x - mu
    var = jnp.mean(xc * xc, -1, keepdims=True)
    return xc * lax.rsqrt(var + LN_EPS) * g + b


def _sigmoid(x):
    return 1.0 / (1.0 + jnp.exp(-x))


def _silu(x):
    return x * _sigmoid(x)


def _dot(a, b):
    return jnp.dot(a, b, preferred_element_type=F32)


def _dot_nt(a, b, precision=None):
    return lax.dot_general(a, b, (((1,), (1,)), ((), ())), precision=precision, preferred_element_type=F32)


def _dot_tn(a, b):
    return lax.dot_general(a, b, (((0,), (0,)), ((), ())), preferred_element_type=F32)


def _embed_ln_kernel(x_ref, g_ref, b_ref, hf_ref, hb_ref):
    h = _ln_rows(x_ref[...], g_ref[...], b_ref[...])
    hf_ref[...] = h
    hb_ref[...] = h.astype(BF16)


def _embed_ln(x, g, b, tm, name):
    m, d = x.shape
    return pl.pallas_call(
        _embed_ln_kernel,
        grid=(m // tm,),
        in_specs=[pl.BlockSpec((tm, d), lambda i: (i, 0)),
                  pl.BlockSpec((1, d), lambda i: (0, 0)),
                  pl.BlockSpec((1, d), lambda i: (0, 0))],
        out_specs=[pl.BlockSpec((tm, d), lambda i: (i, 0)),
                   pl.BlockSpec((tm, d), lambda i: (i, 0))],
        out_shape=[jax.ShapeDtypeStruct((m, d), F32), jax.ShapeDtypeStruct((m, d), BF16)],
        compiler_params=_params(1),
        name=name,
    )(x, g, b)


def _inproj_plain_kernel(a_ref, am_ref, w_ref, o_ref, om_ref, wb_ref):
    @pl.when(pl.program_id(1) == 0)
    def _():
        wb_ref[...] = w_ref[...].astype(BF16)
        om_ref[...] = _dot_nt(am_ref[...], wb_ref[...])
    o_ref[...] = _dot_nt(a_ref[...], wb_ref[...])


def _inproj_glu_kernel(a_ref, am_ref, w1_ref, w2_ref, o_ref, om_ref, w1b_ref, w2b_ref):
    @pl.when(pl.program_id(1) == 0)
    def _():
        w1b_ref[...] = w1_ref[...].astype(BF16)
        w2b_ref[...] = w2_ref[...].astype(BF16)
        am = am_ref[...]
        om_ref[...] = _dot_nt(am, w1b_ref[...]) * _sigmoid(_dot_nt(am, w2b_ref[...]))
    a = a_ref[...]
    o_ref[...] = _dot_nt(a, w1b_ref[...]) * _sigmoid(_dot_nt(a, w2b_ref[...]))


def _decay_epilogue(p, alog, dtb, n_heads):
    x = p + dtb
    softplus = jnp.maximum(x, 0.0) + jnp.log1p(jnp.exp(-jnp.abs(x)))
    g = -jnp.exp(alog) * softplus
    lane = lax.broadcasted_iota(I32, p.shape, 1)
    return jnp.where(lane < n_heads, _sigmoid(p), g)


def _inproj_decay_kernel(a_ref, am_ref, w_ref, alog_ref, dtb_ref, o_ref, om_ref, wb_ref, *, n_heads):
    @pl.when(pl.program_id(1) == 0)
    def _():
        wb_ref[...] = w_ref[...].astype(BF16)
        om_ref[...] = _decay_epilogue(_dot_nt(am_ref[...], wb_ref[...]), alog_ref[...], dtb_ref[...], n_heads)
    o_ref[...] = _decay_epilogue(_dot_nt(a_ref[...], wb_ref[...]), alog_ref[...], dtb_ref[...], n_heads)


def _swiglu_kernel(a_ref, w1_ref, w2_ref, o_ref, w1b_ref, w2b_ref):
    @pl.when(pl.program_id(1) == 0)
    def _():
        w1b_ref[...] = w1_ref[...].astype(BF16)
        w2b_ref[...] = w2_ref[...].astype(BF16)
    a = a_ref[...]
    o_ref[...] = (_silu(_dot(a, w1b_ref[...])) * _dot(a, w2b_ref[...])).astype(o_ref.dtype)


def _proj_residual_kernel(a1_ref, a2_ref, w_ref, r_ref, o_ref, wb_ref, *, alpha, k1):
    @pl.when(pl.program_id(1) == 0)
    def _():
        wb_ref[...] = w_ref[...].astype(BF16)
    acc = _dot(a1_ref[...], wb_ref[0:k1, :]) + _dot(a2_ref[...], wb_ref[k1:, :])
    o_ref[...] = alpha * r_ref[...] + acc


def _inproj_plain(a, am, w_t, row0, n, tm, tn):
    m, k = a.shape
    mm = am.shape[0]
    rb = row0 // tn
    return pl.pallas_call(
        _inproj_plain_kernel,
        grid=(n // tn, m // tm),
        in_specs=[pl.BlockSpec((tm, k), lambda j, i: (i, 0)),
                  pl.BlockSpec((mm, k), lambda j, i: (0, 0)),
                  pl.BlockSpec((tn, k), lambda j, i: (j + rb, 0))],
        out_specs=[pl.BlockSpec((tm, tn), lambda j, i: (i, j)),
                   pl.BlockSpec((mm, tn), lambda j, i: (0, j))],
        out_shape=[jax.ShapeDtypeStruct((m, n), F32), jax.ShapeDtypeStruct((mm, n), F32)],
        scratch_shapes=[pltpu.VMEM((tn, k), BF16)],
        compiler_params=_params(2),
        name="inproj_qkvz",
    )(a, am, w_t)


def _inproj_glu(a, am, w_t, row1, row2, n, tm, tn):
    m, k = a.shape
    mm = am.shape[0]
    r1, r2 = row1 // tn, row2 // tn
    return pl.pallas_call(
        _inproj_glu_kernel,
        grid=(n // tn, m // tm),
        in_specs=[pl.BlockSpec((tm, k), lambda j, i: (i, 0)),
                  pl.BlockSpec((mm, k), lambda j, i: (0, 0)),
                  pl.BlockSpec((tn, k), lambda j, i: (j + r1, 0)),
                  pl.BlockSpec((tn, k), lambda j, i: (j + r2, 0))],
        out_specs=[pl.BlockSpec((tm, tn), lambda j, i: (i, j)),
                   pl.BlockSpec((mm, tn), lambda j, i: (0, j))],
        out_shape=[jax.ShapeDtypeStruct((m, n), F32), jax.ShapeDtypeStruct((mm, n), F32)],
        scratch_shapes=[pltpu.VMEM((tn, k), BF16), pltpu.VMEM((tn, k), BF16)],
        compiler_params=_params(2),
        name="inproj_glu",
    )(a, am, w_t, w_t)


def _inproj_decay(a, am, w_ba_t, alog_row, dtb_row, tm, n_heads):
    m, k = a.shape
    mm = am.shape[0]
    n = w_ba_t.shape[0]
    return pl.pallas_call(
        functools.partial(_inproj_decay_kernel, n_heads=n_heads),
        grid=(1, m // tm),
        in_specs=[pl.BlockSpec((tm, k), lambda j, i: (i, 0)),
                  pl.BlockSpec((mm, k), lambda j, i: (0, 0)),
                  pl.BlockSpec((n, k), lambda j, i: (0, 0)),
                  pl.BlockSpec((1, n), lambda j, i: (0, 0)),
                  pl.BlockSpec((1, n), lambda j, i: (0, 0))],
        out_specs=[pl.BlockSpec((tm, n), lambda j, i: (i, 0)),
                   pl.BlockSpec((mm, n), lambda j, i: (0, 0))],
        out_shape=[jax.ShapeDtypeStruct((m, n), F32), jax.ShapeDtypeStruct((mm, n), F32)],
        scratch_shapes=[pltpu.VMEM((n, k), BF16)],
        compiler_params=_params(2),
        name="inproj_decay",
    )(a, am, w_ba_t, alog_row, dtb_row)


def _swiglu(a, w1, w2, tm, tn, out_dtype):
    m, k = a.shape
    n = w1.shape[1]
    return pl.pallas_call(
        _swiglu_kernel,
        grid=(n // tn, m // tm),
        in_specs=[pl.BlockSpec((tm, k), lambda j, i: (i, 0)),
                  pl.BlockSpec((k, tn), lambda j, i: (0, j)),
                  pl.BlockSpec((k, tn), lambda j, i: (0, j))],
        out_specs=pl.BlockSpec((tm, tn), lambda j, i: (i, j)),
        out_shape=jax.ShapeDtypeStruct((m, n), out_dtype),
        scratch_shapes=[pltpu.VMEM((k, tn), BF16), pltpu.VMEM((k, tn), BF16)],
        compiler_params=_params(2),
        name="shared_swiglu",
    )(a, w1, w2)


def _proj_residual(a1, a2, w, resid, alpha, tm, tn):
    m, k1 = a1.shape
    k2 = a2.shape[1]
    n = w.shape[1]
    return pl.pallas_call(
        functools.partial(_proj_residual_kernel, alpha=alpha, k1=k1),
        grid=(n // tn, m // tm),
        in_specs=[pl.BlockSpec((tm, k1), lambda j, i: (i, 0)),
                  pl.BlockSpec((tm, k2), lambda j, i: (i, 0)),
                  pl.BlockSpec((k1 + k2, tn), lambda j, i: (0, j)),
                  pl.BlockSpec((tm, tn), lambda j, i: (i, j))],
        out_specs=pl.BlockSpec((tm, tn), lambda j, i: (i, j)),
        out_shape=jax.ShapeDtypeStruct((m, n), F32),
        scratch_shapes=[pltpu.VMEM((k1 + k2, tn), BF16)],
        compiler_params=_params(2),
        name="outproj_residual",
    )(a1, a2, w, resid)


def _conv_module_kernel(c_ref, cm_ref, w_ref, b_ref, g_ref, be_ref, o_ref, buf_ref, sh_ref, acc_ref, *, ts, kw):
    i = pl.program_id(0)
    ch = c_ref.shape[1]

    @pl.when(i == 0)
    def _():
        buf_ref[0:CONV_HALO - N_META, :] = jnp.zeros((CONV_HALO - N_META, ch), F32)
        buf_ref[CONV_HALO - N_META:CONV_HALO, :] = cm_ref[...]

    @pl.when(i > 0)
    def _():
        buf_ref[0:CONV_HALO, :] = buf_ref[ts:ts + CONV_HALO, :]

    buf_ref[CONV_HALO:CONV_HALO + ts, :] = c_ref[...]
    base = CONV_HALO - (kw - 1)
    span = ts + CONV_HALO - SUBLANES
    for r in range(1, SUBLANES):
        sh_ref[r - 1] = buf_ref[r:r + span, :]

    def row_block(rb, carry):
        r0 = pl.multiple_of(rb * CONV_ROWS, CONV_ROWS)
        for cb in range(ch // CONV_LANES):
            lanes = slice(cb * CONV_LANES, (cb + 1) * CONV_LANES)
            acc = jnp.zeros((CONV_ROWS, CONV_LANES), F32) + b_ref[:, lanes]
            for j in range(kw):
                off = base + j
                r, q = off % SUBLANES, off - off % SUBLANES
                if r == 0:
                    src = buf_ref[pl.ds(r0 + q, CONV_ROWS), lanes]
                else:
                    src = sh_ref[r - 1, pl.ds(r0 + q, CONV_ROWS), lanes]
                acc = acc + w_ref[j:j + 1, lanes] * src
            acc_ref[pl.ds(r0, CONV_ROWS), lanes] = acc
        return carry
    lax.fori_loop(0, ts // CONV_ROWS, row_block, 0)
    y = _ln_rows(acc_ref[...], g_ref[...], be_ref[...])
    o_ref[...] = _silu(y).astype(o_ref.dtype)


def _conv_module(c, c_meta, w, b, g, be, ts):
    t, ch = c.shape
    kw = w.shape[0]
    assert kw - 1 <= CONV_HALO and N_META <= CONV_HALO and ts >= CONV_HALO
    assert ts % CONV_ROWS == 0 and ch % CONV_LANES == 0
    return pl.pallas_call(
        functools.partial(_conv_module_kernel, ts=ts, kw=kw),
        grid=(t // ts,),
        in_specs=[pl.BlockSpec((ts, ch), lambda i: (i, 0)),
                  pl.BlockSpec((N_META, ch), lambda i: (0, 0)),
                  pl.BlockSpec((kw, ch), lambda i: (0, 0)),
                  pl.BlockSpec((1, ch), lambda i: (0, 0)),
                  pl.BlockSpec((1, ch), lambda i: (0, 0)),
                  pl.BlockSpec((1, ch), lambda i: (0, 0))],
        out_specs=pl.BlockSpec((ts, ch), lambda i: (i, 0)),
        out_shape=jax.ShapeDtypeStruct((t, ch), BF16),
        scratch_shapes=[pltpu.VMEM((CONV_HALO + ts, ch), F32),
                        pltpu.VMEM((SUBLANES - 1, ts + CONV_HALO - SUBLANES, ch), F32),
                        pltpu.VMEM((ts, ch), F32)],
        compiler_params=_params(1),
        name="conv_module",
    )(c, c_meta, w, b, g, be)


def _gdn_kernel(x_ref, m_ref, bg_ref, gt_ref, cw_ref, gn_ref, o_ref, cbuf_ref, act_ref, s_ref,
                *, n_heads):
    c = pl.program_id(0)
    gw = n_heads * HEAD_DIM
    C = CHUNK

    @pl.when(c == 0)
    def _():
        cbuf_ref[0:8, :] = jnp.zeros((8, 3 * gw), F32)
        s_ref[...] = jnp.zeros(s_ref.shape, F32)

    @pl.when(c > 0)
    def _():
        cbuf_ref[0:8, :] = cbuf_ref[C:C + 8, :]

    raw = jnp.where(c == 0, m_ref[...], x_ref[...])
    cbuf_ref[8:8 + C, :] = raw[:, 0:3 * gw]
    kw = cw_ref.shape[0]
    conv = jnp.zeros((C, 3 * gw), F32)
    for j in range(kw):
        conv = conv + cw_ref[j:j + 1, :] * cbuf_ref[9 - kw + j:9 - kw + j + C, :]
    act_ref[...] = _silu(conv)

    bg = bg_ref[...]
    beta_all = bg[:, 0:n_heads]
    g_all = bg[:, n_heads:2 * n_heads]
    row = lax.broadcasted_iota(I32, (C, C), 0)
    col = lax.broadcasted_iota(I32, (C, C), 1)
    incl = row >= col
    strict = row > col
    tri_incl = jnp.where(incl, 1.0, 0.0).astype(F32)
    tri_upper = jnp.where(row <= col, 1.0, 0.0).astype(F32)
    hp = lax.Precision.HIGHEST
    gc_cols = jnp.dot(tri_incl, g_all, precision=hp, preferred_element_type=F32)
    gc_rows = jnp.dot(gt_ref[0], tri_upper, precision=hp, preferred_element_type=F32)
    eye = jnp.where(row == col, 1.0, 0.0).astype(F32)
    right_half = lax.broadcasted_iota(I32, (C, 2 * C), 1) >= C
    gn = gn_ref[...]

    for h0 in range(0, n_heads, HEAD_GROUP):
        heads = range(h0, min(h0 + HEAD_GROUP, n_heads))
        st = {h: {} for h in heads}

        for h in heads:
            d = st[h]
            q = act_ref[:, h * HEAD_DIM:(h + 1) * HEAD_DIM]
            k = act_ref[:, gw + h * HEAD_DIM:gw + (h + 1) * HEAD_DIM]
            q = q * lax.rsqrt(jnp.sum(q * q, -1, keepdims=True) + RMS_EPS) * (HEAD_DIM ** -0.5)
            k = k * lax.rsqrt(jnp.sum(k * k, -1, keepdims=True) + RMS_EPS)
            beta = beta_all[:, h:h + 1]
            gcol = gc_cols[:, h:h + 1]
            grow = gc_rows[h:h + 1, :]
            g_last = gc_cols[C - 1:C, h:h + 1]
            d["decay"] = jnp.where(incl, jnp.exp(jnp.where(incl, gcol - grow, 0.0)), 0.0)
            egc = jnp.exp(gcol)
            kb = k * beta
            d["a1"] = _dot_nt(jnp.concatenate([kb, q], axis=0).astype(BF16), k.astype(BF16))
            v = act_ref[:, 2 * gw + h * HEAD_DIM:2 * gw + (h + 1) * HEAD_DIM]
            d["rhs"] = jnp.concatenate([v * beta, kb * egc], axis=1).astype(BF16)
            d["q_dec"] = q * egc
            d["k_dec"] = (k * jnp.exp(g_last - gcol)).astype(BF16)
            d["s_scale"] = jnp.exp(g_last)

        for h in heads:
            d = st[h]
            lmat = jnp.where(strict, d["a1"][0:C] * d["decay"], 0.0)
            d["attn"] = (d["a1"][C:2 * C] * d["decay"]).astype(BF16)
            d["wmat"] = jnp.concatenate([-lmat, eye], axis=1)

        for _ in range(6):
            for h in heads:
                d = st[h]
                wmat = d["wmat"]
                r = _dot(wmat[:, 0:C].astype(BF16), wmat.astype(BF16))
                d["wmat"] = r + jnp.where(right_half, wmat, 0.0)

        for h in heads:
            d = st[h]
            tinv = jnp.where(right_half, d["wmat"], 0.0).astype(BF16)
            d["sol"] = _dot(tinv, jnp.concatenate([d["rhs"], d["rhs"]], axis=0))

        for h in heads:
            d = st[h]
            w = d["sol"][:, HEAD_DIM:2 * HEAD_DIM]
            d["ws_qs"] = _dot(jnp.concatenate([w, d["q_dec"]], axis=0).astype(BF16), s_ref[h].astype(BF16))

        for h in heads:
            d = st[h]
            vnb = (d["sol"][:, 0:HEAD_DIM] - d["ws_qs"][0:C]).astype(BF16)
            d["o"] = d["ws_qs"][C:2 * C] + _dot(d["attn"], vnb)
            s_ref[h] = s_ref[h] * d["s_scale"] + _dot_tn(d["k_dec"], vnb)

        for h in heads:
            o = st[h]["o"]
            sl = slice(h * HEAD_DIM, (h + 1) * HEAD_DIM)
            o = o * lax.rsqrt(jnp.mean(o * o, -1, keepdims=True) + RMS_EPS) * gn
            z = jnp.where(c == 0, m_ref[:, 3 * gw + h * HEAD_DIM:3 * gw + (h + 1) * HEAD_DIM],
                          x_ref[:, 3 * gw + h * HEAD_DIM:3 * gw + (h + 1) * HEAD_DIM])
            o_ref[:, sl] = (o * _silu(z)).astype(o_ref.dtype)


def _gdn(qkvz, qkvz_meta, bg_all, gt3, conv_w, gn_row, n_heads):
    t = qkvz.shape[0]
    gw = n_heads * HEAD_DIM
    n_chunks = t // CHUNK + 1
    kw = conv_w.shape[0]
    assert kw <= 8
    xmap = lambda c: (jnp.maximum(c - 1, 0), 0)
    return pl.pallas_call(
        functools.partial(_gdn_kernel, n_heads=n_heads),
        grid=(n_chunks,),
        in_specs=[pl.BlockSpec((CHUNK, 4 * gw), xmap),
                  pl.BlockSpec((CHUNK, 4 * gw), lambda c: (0, 0)),
                  pl.BlockSpec((CHUNK, 2 * n_heads), lambda c: (c, 0)),
                  pl.BlockSpec((1, n_heads, CHUNK), lambda c: (c, 0, 0)),
                  pl.BlockSpec((kw, 3 * gw), lambda c: (0, 0)),
                  pl.BlockSpec((1, HEAD_DIM), lambda c: (0, 0))],
        out_specs=pl.BlockSpec((CHUNK, gw), xmap),
        out_shape=jax.ShapeDtypeStruct((t, gw), BF16),
        scratch_shapes=[pltpu.VMEM((CHUNK + 8, 3 * gw), F32),
                        pltpu.VMEM((CHUNK, 3 * gw), F32),
                        pltpu.VMEM((n_heads, HEAD_DIM, HEAD_DIM), F32)],
        compiler_params=_params(1),
        name="gated_delta_rule",
    )(qkvz, qkvz_meta, bg_all, gt3, conv_w, gn_row)


def _router_kernel(r_ref, g_ref, b_ref, rwt_ref, bias_ref, h1_ref, h1b_ref, h1p_ref,
                   eidx_ref, wsel_ref, pos_ref, cnt_ref, carry_ref, *, tm, n_exp):
    i = pl.program_id(0)
    per = n_exp // N_EXPERT_GROUPS

    @pl.when(i == 0)
    def _():
        carry_ref[...] = jnp.zeros(carry_ref.shape, F32)

    h1 = _ln_rows(r_ref[...], g_ref[...], b_ref[...])
    h1_ref[...] = h1
    h1b_ref[...] = h1.astype(BF16)
    h1p_ref[...] = _pack_rows(h1)

    logits = _dot_nt(rwt_ref[...], h1, precision=lax.Precision.HIGHEST)
    scores = _sigmoid(logits)
    biased = scores + bias_ref[...]
    neg_inf = F32(-jnp.inf)

    gs_rows = []
    sub = lax.broadcasted_iota(I32, (per, tm), 0)
    for g in range(N_EXPERT_GROUPS):
        xg = biased[g * per:(g + 1) * per, :]
        m1 = jnp.max(xg, axis=0, keepdims=True)
        first = jnp.min(jnp.where(xg == m1, sub, per), axis=0, keepdims=True)
        m2 = jnp.max(jnp.where(sub == first, neg_inf, xg), axis=0, keepdims=True)
        gs_rows.append(m1 + m2)
    gs = jnp.concatenate(gs_rows, axis=0)
    gid = lax.broadcasted_iota(I32, (N_EXPERT_GROUPS, tm), 0)
    grank = jnp.zeros((N_EXPERT_GROUPS, tm), F32)
    for g in range(N_EXPERT_GROUPS):
        rowv = gs[g:g + 1, :]
        grank = grank + jnp.where(rowv > gs, 1.0, jnp.where((rowv == gs) & (gid > g), 1.0, 0.0))
    gsel = jnp.where(grank < TOPK_GROUPS, 1.0, 0.0)
    emask = jnp.concatenate(
        [jnp.broadcast_to(gsel[g:g + 1, :], (per, tm)) for g in range(N_EXPERT_GROUPS)], axis=0)
    masked = jnp.where(emask > 0.0, biased, neg_inf)
    eid = lax.broadcasted_iota(I32, (n_exp, tm), 0)
    erank = jnp.zeros((n_exp, tm), F32)
    for e in range(n_exp):
        rowv = masked[e:e + 1, :]
        erank = erank + jnp.where(rowv > masked, 1.0, jnp.where((rowv == masked) & (eid > e), 1.0, 0.0))
    sel = jnp.where(erank < TOP_K, emask, 0.0)
    wdense = scores * sel
    gate = wdense / jnp.sum(wdense, axis=0, keepdims=True) * ROUTED_SCALE

    er = lax.broadcasted_iota(I32, (n_exp, n_exp), 0)
    ec = lax.broadcasted_iota(I32, (n_exp, n_exp), 1)
    lower = jnp.where(er > ec, 1.0, 0.0).astype(BF16)
    selb = sel.astype(BF16)
    slot = _dot(lower, selb)
    tr = lax.broadcasted_iota(I32, (tm, tm), 0)
    tc = lax.broadcasted_iota(I32, (tm, tm), 1)
    upper = jnp.where(tr < tc, 1.0, 0.0).astype(BF16)
    carry = carry_ref[:, 0:1]
    pos = _dot(selb, upper) + carry
    eid_f = eid.astype(F32)
    e_rows, w_rows, p_rows = [], [], []
    for k in range(TOP_K):
        mk = jnp.where(slot == float(k), sel, 0.0)
        e_rows.append(jnp.sum(mk * eid_f, axis=0, keepdims=True))
        w_rows.append(jnp.sum(mk * gate, axis=0, keepdims=True))
        p_rows.append(jnp.sum(mk * pos, axis=0, keepdims=True))
    eidx_ref[...] = jnp.concatenate(e_rows, axis=0).astype(I32)
    wsel_ref[...] = jnp.concatenate(w_rows, axis=0)
    pos_ref[...] = jnp.concatenate(p_rows, axis=0).astype(I32)
    new_carry = carry + jnp.sum(sel, axis=1, keepdims=True)
    carry_ref[...] = jnp.broadcast_to(new_carry, carry_ref.shape)
    cnt_ref[...] = jnp.broadcast_to(new_carry, cnt_ref.shape).astype(I32)


def _router(r, g, b, rwt, bias_col, tm):
    t, d = r.shape
    n_exp = rwt.shape[0]
    row = lambda i: (i, 0)
    colb = lambda i: (0, i)
    fixed = lambda i: (0, 0)
    return pl.pallas_call(
        functools.partial(_router_kernel, tm=tm, n_exp=n_exp),
        grid=(t // tm,),
        in_specs=[pl.BlockSpec((tm, d), row),
                  pl.BlockSpec((1, d), fixed),
                  pl.BlockSpec((1, d), fixed),
                  pl.BlockSpec((n_exp, d), fixed),
                  pl.BlockSpec((n_exp, 1), fixed)],
        out_specs=[pl.BlockSpec((tm, d), row),
                   pl.BlockSpec((tm, d), row),
                   pl.BlockSpec((tm, d // 2), row),
                   pl.BlockSpec((TOP_K, tm), colb),
                   pl.BlockSpec((TOP_K, tm), colb),
                   pl.BlockSpec((TOP_K, tm), colb),
                   pl.BlockSpec((n_exp, 128), fixed)],
        out_shape=[jax.ShapeDtypeStruct((t, d), F32),
                   jax.ShapeDtypeStruct((t, d), BF16),
                   jax.ShapeDtypeStruct((t, d // 2), U32),
                   jax.ShapeDtypeStruct((TOP_K, t), I32),
                   jax.ShapeDtypeStruct((TOP_K, t), F32),
                   jax.ShapeDtypeStruct((TOP_K, t), I32),
                   jax.ShapeDtypeStruct((n_exp, 128), I32)],
        scratch_shapes=[pltpu.VMEM((n_exp, 128), F32)],
        compiler_params=_params(1),
        name="ln1_router",
    )(r, g, b, rwt, bias_col)


def _dest_kernel(gstart_ref, eidx_ref, pos_ref, dest_ref, *, n_exp):
    eidx = eidx_ref[...]
    acc = pos_ref[...]
    for e in range(n_exp):
        acc = acc + jnp.where(eidx == e, gstart_ref[e], 0)
    dest_ref[...] = acc


def _dest_rows(gstart, eidx, pos):
    n_exp = gstart.shape[0]
    return pl.pallas_call(
        functools.partial(_dest_kernel, n_exp=n_exp),
        grid_spec=pltpu.PrefetchScalarGridSpec(
            num_scalar_prefetch=1,
            grid=(1,),
            in_specs=[pl.BlockSpec(eidx.shape, lambda i, gs: (0, 0)),
                      pl.BlockSpec(pos.shape, lambda i, gs: (0, 0))],
            out_specs=pl.BlockSpec(eidx.shape, lambda i, gs: (0, 0))),
        out_shape=jax.ShapeDtypeStruct(eidx.shape, I32),
        compiler_params=_params(1),
        name="dest_rows",
    )(gstart, eidx, pos)


def _dispatch_kernel(dest_ref, hp_ref, xs_ref, sem, *, tm):
    def issue(t, carry):
        for k in range(TOP_K):
            d = dest_ref[0, 0, t * TOP_K + k]
            pltpu.make_async_copy(hp_ref.at[pl.ds(t, 1)], xs_ref.at[pl.ds(d, 1)], sem).start()
        return carry
    lax.fori_loop(0, tm, issue, 0)
    pltpu.make_async_copy(xs_ref.at[pl.ds(0, tm * TOP_K)], xs_ref.at[pl.ds(0, tm * TOP_K)], sem).wait()


def _dispatch(dest_tiles, h1p, tm):
    t, dw = h1p.shape
    return pl.pallas_call(
        functools.partial(_dispatch_kernel, tm=tm),
        grid=(t // tm,),
        in_specs=[pl.BlockSpec((1, 1, tm * TOP_K), lambda i: (i, 0, 0), memory_space=pltpu.SMEM),
                  pl.BlockSpec((tm, dw), lambda i: (i, 0))],
        out_specs=pl.BlockSpec(memory_space=pl.ANY),
        out_shape=jax.ShapeDtypeStruct((t * TOP_K, dw), U32),
        scratch_shapes=[pltpu.SemaphoreType.DMA(())],
        compiler_params=_params(1),
        name="dispatch_rows",
    )(dest_tiles, h1p)


def _pack_rows(y):
    n = y.shape[1] // 2
    yb = y.astype(BF16)
    hi = lax.bitcast_convert_type(yb[:, 0:n].astype(F32), U32)
    lo = lax.bitcast_convert_type(yb[:, n:].astype(F32), U32)
    return hi | (lo >> 16)


def _unpack_rows_f32(xu):
    hi = lax.bitcast_convert_type(xu & jnp.uint32(0xFFFF0000), F32)
    lo = lax.bitcast_convert_type(xu << 16, F32)
    return hi, lo


def _unpack_rows(xu):
    hi, lo = _unpack_rows_f32(xu)
    return hi.astype(BF16), lo.astype(BF16)


def _expert_weights(v, ve_ref, vfe_ref, eo_ref, vne_ref, hbm_refs, f32_refs, bf16_refs, sems):
    @pl.when(vfe_ref[v] == 1)
    def _():
        slot = eo_ref[v] % 2
        e = ve_ref[v]

        def copies(expert, s):
            return [pltpu.make_async_copy(h.at[expert], f.at[s], sem.at[s])
                    for h, f, sem in zip(hbm_refs, f32_refs, sems)]

        @pl.when(v == 0)
        def _():
            for cp in copies(e, slot):
                cp.start()

        for cp in copies(e, slot):
            cp.wait()
        ne = vne_ref[v]

        @pl.when(ne >= 0)
        def _():
            for cp in copies(ne, 1 - slot):
                cp.start()

        for f, bf in zip(f32_refs, bf16_refs):
            bf[...] = f[slot].astype(BF16)


def _visit_subblocks(vt_ref, ve_ref, vft_ref, gs_ref, ge_ref, v, tm, sb, o_ref, compute):
    e = ve_ref[v]
    gs, ge = gs_ref[e], ge_ref[e]
    first = vft_ref[v] == 1
    width = o_ref.shape[1]
    for s in range(tm // sb):
        r0 = vt_ref[v] * tm + s * sb
        hit = (r0 < ge) & (r0 + sb > gs)
        rows = pl.ds(s * sb, sb)

        @pl.when(hit)
        def _():
            y = compute(s).astype(o_ref.dtype)
            owned = (r0 >= gs) & (r0 + sb <= ge)

            @pl.when(owned)
            def _():
                o_ref[rows, :] = y

            @pl.when(jnp.logical_not(owned))
            def _():
                ridx = r0 + lax.broadcasted_iota(I32, (sb, width), 0)
                mask = (ridx >= gs) & (ridx < ge)

                @pl.when(first)
                def _():
                    o_ref[rows, :] = jnp.where(mask, y, jnp.zeros_like(y))

                @pl.when(jnp.logical_not(first))
                def _():
                    o_ref[rows, :] = jnp.where(mask, y, o_ref[rows, :])

        @pl.when(jnp.logical_not(hit) & first)
        def _():
            o_ref[rows, :] = jnp.zeros((sb, width), o_ref.dtype)


def _gmm_up_kernel(vt_ref, ve_ref, vft_ref, vfe_ref, eo_ref, vne_ref, gs_ref, ge_ref, nv_ref,
                   xs_ref, wg_hbm, wu_hbm, o_ref, wgf_ref, wuf_ref, wgb_ref, wub_ref, semg, semu, *, tm, sb):
    v = pl.program_id(0)

    @pl.when(v < nv_ref[0])
    def _():
        _expert_weights(v, ve_ref, vfe_ref, eo_ref, vne_ref, (wg_hbm, wu_hbm), (wgf_ref, wuf_ref),
                        (wgb_ref, wub_ref), (semg, semu))

        def compute(s):
            hi, lo = _unpack_rows(xs_ref[s * sb:(s + 1) * sb, :])
            kh = hi.shape[1]
            hg = _dot(hi, wgb_ref[0:kh, :]) + _dot(lo, wgb_ref[kh:, :])
            hu = _dot(hi, wub_ref[0:kh, :]) + _dot(lo, wub_ref[kh:, :])
            return _silu(hg) * hu

        _visit_subblocks(vt_ref, ve_ref, vft_ref, gs_ref, ge_ref, v, tm, sb, o_ref, compute)


def _gmm_down_kernel(vt_ref, ve_ref, vft_ref, vfe_ref, eo_ref, vne_ref, gs_ref, ge_ref, nv_ref,
                     a_ref, wd_hbm, o_ref, wdf_ref, wdb_ref, semd, *, tm, sb):
    v = pl.program_id(0)

    @pl.when(v < nv_ref[0])
    def _():
        _expert_weights(v, ve_ref, vfe_ref, eo_ref, vne_ref, (wd_hbm,), (wdf_ref,), (wdb_ref,), (semd,))

        def compute(s):
            return _pack_rows(_dot(a_ref[s * sb:(s + 1) * sb, :], wdb_ref[...]))

        _visit_subblocks(vt_ref, ve_ref, vft_ref, gs_ref, ge_ref, v, tm, sb, o_ref, compute)


def _visit_schedule(counts, n_rows, tm):
    n_exp = counts.shape[0]
    n_tiles = n_rows // tm
    n_visits = n_tiles + n_exp - 1
    ends = jnp.cumsum(counts)
    starts = ends - counts
    first_tile = starts // tm
    last_tile = jnp.maximum(ends - 1, 0) // tm
    tiles_e = jnp.where(counts > 0, last_tile - first_tile + 1, 0)
    vend = jnp.cumsum(tiles_e)
    vstart = vend - tiles_e
    total = vend[-1]
    v = jnp.arange(n_visits, dtype=I32)
    vc = jnp.minimum(v, total - 1)
    ve = jnp.searchsorted(vend, vc, side="right").astype(I32)
    vt = (vc - vstart[ve] + first_tile[ve]).astype(I32)
    prev_t = jnp.concatenate([jnp.full((1,), -1, I32), vt[:-1]])
    prev_e = jnp.concatenate([jnp.full((1,), -1, I32), ve[:-1]])
    vft = (vt != prev_t).astype(I32)
    vfe = (ve != prev_e).astype(I32)
    eo = (jnp.cumsum(vfe) - 1).astype(I32)
    eidx = jnp.arange(n_exp, dtype=I32)
    later = lax.cummin(jnp.where(counts > 0, eidx, n_exp), axis=0, reverse=True)
    nxt = jnp.concatenate([later[1:], jnp.full((1,), n_exp, I32)])
    vne = jnp.where(nxt < n_exp, nxt, -1)[ve].astype(I32)
    sched = (vt, ve, vft, vfe, eo, vne, starts.astype(I32), ends.astype(I32), total.reshape(1).astype(I32))
    return sched, n_visits


def _gmm_up(sched, n_visits, xs, wg, wu, tm, sb):
    n, dw = xs.shape
    n_exp, d, f = wg.shape
    return pl.pallas_call(
        functools.partial(_gmm_up_kernel, tm=tm, sb=sb),
        grid_spec=pltpu.PrefetchScalarGridSpec(
            num_scalar_prefetch=len(sched),
            grid=(n_visits,),
            in_specs=[pl.BlockSpec((tm, dw), lambda v, vt, *_: (vt[v], 0)),
                      pl.BlockSpec(memory_space=pl.ANY),
                      pl.BlockSpec(memory_space=pl.ANY)],
            out_specs=pl.BlockSpec((tm, f), lambda v, vt, *_: (vt[v], 0)),
            scratch_shapes=[pltpu.VMEM((2, d, f), F32), pltpu.VMEM((2, d, f), F32),
                            pltpu.VMEM((d, f), BF16), pltpu.VMEM((d, f), BF16),
                            pltpu.SemaphoreType.DMA((2,)), pltpu.SemaphoreType.DMA((2,))]),
        out_shape=jax.ShapeDtypeStruct((n, f), BF16),
        compiler_params=_params(1),
        name="experts_gate_up",
    )(*sched, xs, wg, wu)


def _gmm_down(sched, n_visits, act, wd, tm, sb):
    n, f = act.shape
    n_exp, _, d = wd.shape
    return pl.pallas_call(
        functools.partial(_gmm_down_kernel, tm=tm, sb=sb),
        grid_spec=pltpu.PrefetchScalarGridSpec(
            num_scalar_prefetch=len(sched),
            grid=(n_visits,),
            in_specs=[pl.BlockSpec((tm, f), lambda v, vt, *_: (vt[v], 0)),
                      pl.BlockSpec(memory_space=pl.ANY)],
            out_specs=pl.BlockSpec((tm, d // 2), lambda v, vt, *_: (vt[v], 0)),
            scratch_shapes=[pltpu.VMEM((2, f, d), F32), pltpu.VMEM((f, d), BF16),
                            pltpu.SemaphoreType.DMA((2,))]),
        out_shape=jax.ShapeDtypeStruct((n, d // 2), U32),
        compiler_params=_params(1),
        name="experts_down",
    )(*sched, act, wd)


def _combine_kernel(dcur_ref, dnxt_ref, w_ref, h1_ref, as_ref, wsd_ref, g_ref, b_ref, ys_ref, o_ref,
                    gbuf_ref, ybuf_ref, sem, *, tm, alpha):
    i = pl.program_id(0)
    n = pl.num_programs(0)
    slot = i % 2
    groups = tm // SUBLANES
    d = o_ref.shape[1]
    half = d // 2

    def issue_token(dref, dst_slot, t):
        for k in range(TOP_K):
            src = dref[0, 0, t * TOP_K + k]
            pltpu.make_async_copy(ys_ref.at[pl.ds(src, 1)], gbuf_ref.at[dst_slot, pl.ds(k * tm + t, 1)],
                                  sem.at[dst_slot]).start()

    def wait_slot(s):
        pltpu.make_async_copy(ys_ref.at[pl.ds(0, tm * TOP_K)], gbuf_ref.at[s], sem.at[s]).wait()

    @pl.when(i == 0)
    def _():
        def first(t, carry):
            issue_token(dcur_ref, 0, t)
            return carry
        lax.fori_loop(0, tm, first, 0)

    ybuf_ref[...] = alpha * h1_ref[...] + _dot(as_ref[...], wsd_ref[...])
    wait_slot(slot)

    def reduce_group(r, carry):
        r8 = pl.multiple_of(r * SUBLANES, SUBLANES)
        rows = pl.ds(r8, SUBLANES)
        y_hi = ybuf_ref[rows, 0:half]
        y_lo = ybuf_ref[rows, half:]
        w = w_ref[rows, :]
        for j in range(SUBLANES):
            issue_token(dnxt_ref, 1 - slot, r8 + j)
            for k in range(j * TOP_K // SUBLANES, (j + 1) * TOP_K // SUBLANES):
                e_hi, e_lo = _unpack_rows_f32(gbuf_ref[slot, pl.ds(k * tm + r8, SUBLANES), :])
                wk = w[:, k:k + 1]
                y_hi = y_hi + wk * e_hi
                y_lo = y_lo + wk * e_lo
        ybuf_ref[rows, 0:half] = y_hi
        ybuf_ref[rows, half:] = y_lo
        return carry
    lax.fori_loop(0, groups, reduce_group, 0)
    o_ref[...] = _ln_rows(ybuf_ref[...], g_ref[...], b_ref[...])

    @pl.when(i == n - 1)
    def _():
        wait_slot(1 - slot)


def _combine(dest_tiles, w_tok, h1, act_s, wsd, g, b, ys, tm, alpha):
    t, d = h1.shape
    f = act_s.shape[1]
    n = t // tm
    row = lambda i: (i, 0)
    fixed = lambda i: (0, 0)
    return pl.pallas_call(
        functools.partial(_combine_kernel, tm=tm, alpha=alpha),
        grid=(n,),
        in_specs=[pl.BlockSpec((1, 1, tm * TOP_K), lambda i: (i, 0, 0), memory_space=pltpu.SMEM),
                  pl.BlockSpec((1, 1, tm * TOP_K), lambda i: (jnp.minimum(i + 1, n - 1), 0, 0),
                               memory_space=pltpu.SMEM),
                  pl.BlockSpec((tm, TOP_K), row),
                  pl.BlockSpec((tm, d), row),
                  pl.BlockSpec((tm, f), row),
                  pl.BlockSpec((f, d), fixed),
                  pl.BlockSpec((1, d), fixed),
                  pl.BlockSpec((1, d), fixed),
                  pl.BlockSpec(memory_space=pl.ANY)],
        out_specs=pl.BlockSpec((tm, d), row),
        out_shape=jax.ShapeDtypeStruct((t, d), F32),
        scratch_shapes=[pltpu.VMEM((2, tm * TOP_K, d // 2), U32),
                        pltpu.VMEM((tm, d), F32),
                        pltpu.SemaphoreType.DMA((2,))],
        compiler_params=_params(1),
        name="combine_ln2",
    )(dest_tiles, dest_tiles, w_tok, h1, act_s, wsd, g, b, ys)


def _tile(n, pref):
    t = min(n, pref)
    assert n % t == 0, (n, pref)
    return t


def kernel(x, meta_tokens, emb_ln_g, emb_ln_b, w_in, conv_w, conv_b, conv_ln_g, conv_ln_b, short_conv_w, a_log, dt_bias, gdn_norm_g, w_out, ln1_g, ln1_b, router_w, router_bias, expert_w_gate, expert_w_up, expert_w_down, shared_w_gate, shared_w_up, shared_w_down, ln2_g, ln2_b):
    depth = w_in.shape[0]
    assert depth == 1 and x.shape[0] == 1, "single layer, single sequence"
    t, d = x.shape[1], x.shape[2]
    cw = conv_w.shape[2]
    n_heads = a_log.shape[1]
    gw = n_heads * HEAD_DIM
    alpha = (2.0 * depth) ** 0.25
    assert meta_tokens.shape[0] == N_META and t % CHUNK == 0 and w_in.shape[2] == 2 * cw + 4 * gw + 2 * n_heads
    row = lambda a: a.reshape(1, -1)

    h_f, h_b = _embed_ln(x[0], row(emb_ln_g), row(emb_ln_b), _tile(t, 256), "embed_ln")
    _, hm_b = _embed_ln(meta_tokens, row(emb_ln_g), row(emb_ln_b), N_META, "embed_ln_meta")

    w_t = jnp.transpose(w_in[0])
    tm = _tile(t, 1024)
    zeros_h = jnp.zeros((n_heads,), F32)
    alog_row = row(jnp.concatenate([zeros_h, a_log[0]]))
    dtb_row = row(jnp.concatenate([zeros_h, dt_bias[0]]))
    c_x, c_m = _inproj_glu(h_b, hm_b, w_t, 0, cw, cw, tm, _tile(cw, 256))
    qkvz_x, qkvz_m = _inproj_plain(h_b, hm_b, w_t, 2 * cw, 4 * gw, tm, _tile(4 * gw, 512))
    bg_x, bg_m = _inproj_decay(h_b, hm_b, w_t[2 * cw + 4 * gw:], alog_row, dtb_row, tm, n_heads)

    y_conv = _conv_module(c_x, c_m, conv_w[0], row(conv_b[0]), row(conv_ln_g[0]), row(conv_ln_b[0]),
                          _tile(t, 256))

    pad = CHUNK - N_META
    qkvz_m64 = jnp.pad(qkvz_m, ((pad, 0), (0, 0)))
    bg_all = jnp.concatenate([jnp.pad(bg_m, ((pad, 0), (0, 0))), bg_x], axis=0)
    gt3 = bg_all[:, n_heads:].reshape(-1, CHUNK, n_heads).transpose(0, 2, 1)
    y_gdn = _gdn(qkvz_x, qkvz_m64, bg_all, gt3, short_conv_w[0], row(gdn_norm_g[0]), n_heads)

    r1 = _proj_residual(y_conv, y_gdn, w_out[0], h_f, alpha, tm, _tile(d, 512))
    h1, h1_b, h1_p, eidx, wsel, pos, cnt = _router(
        r1, row(ln1_g[0]), row(ln1_b[0]), router_w[0].T, router_bias[0].reshape(-1, 1), _tile(t, 256))
    counts = cnt[:, 0]
    gstart = jnp.cumsum(counts) - counts
    dest = _dest_rows(gstart.astype(I32), eidx, pos)

    tmd = _tile(t, 256)
    xs = _dispatch(dest.T.reshape(t // tmd, 1, tmd * TOP_K), h1_p, tmd)
    tmg = _tile(t * TOP_K, 512)
    sbg = _tile(tmg, 128)
    sched, n_visits = _visit_schedule(counts, t * TOP_K, tmg)
    act = _gmm_up(sched, n_visits, xs, expert_w_gate[0], expert_w_up[0], tmg, sbg)
    ys = _gmm_down(sched, n_visits, act, expert_w_down[0], tmg, _tile(tmg, 256))

    act_s = _swiglu(h1_b, shared_w_gate[0], shared_w_up[0], tm, _tile(shared_w_gate.shape[2], 256), BF16)
    tmc = _tile(t, 128)
    out = _combine(dest.T.reshape(t // tmc, 1, tmc * TOP_K), wsel.T, h1, act_s, shared_w_down[0].astype(BF16),
                   row(ln2_g[0]), row(ln2_b[0]), ys, tmc, alpha)
    return out[None]
```

```python
import functools

import jax
import jax.numpy as jnp
from jax import lax
from jax.experimental import pallas as pl
from jax.experimental.pallas import tpu as pltpu

F32 = jnp.float32
BF16 = jnp.bfloat16
I32 = jnp.int32
U32 = jnp.uint32

N_META = 16
CHUNK = 64
HEAD_DIM = 128
HEAD_GROUP = 16
TOP_K = 8
N_EXPERT_GROUPS = 8
TOPK_GROUPS = 4
ROUTED_SCALE = 2.5
LN_EPS = 1e-5
RMS_EPS = 1e-6
CONV_HALO = 32
CONV_ROWS = 32
CONV_LANES = 512
SUBLANES = 8
LANES = 128
VMEM_LIMIT = 56 * 1024 * 1024


def _params(n_axes, vmem=VMEM_LIMIT):
    return pltpu.CompilerParams(dimension_semantics=("arbitrary",) * n_axes, vmem_limit_bytes=vmem)


def _ln_rows(x, g, b):
    mu = jnp.mean(x, -1, keepdims=True)
    xc = x - mu
    var = jnp.mean(xc * xc, -1, keepdims=True)
    return xc * lax.rsqrt(var + LN_EPS) * g + b


def _sigmoid(x):
    return 1.0 / (1.0 + jnp.exp(-x))


def _silu(x):
    return x * _sigmoid(x)


def _dot(a, b):
    return jnp.dot(a, b, preferred_element_type=F32)


def _dot_nt(a, b, precision=None):
    return lax.dot_general(a, b, (((1,), (1,)), ((), ())), precision=precision, preferred_element_type=F32)


def _dot_tn(a, b):
    return lax.dot_general(a, b, (((0,), (0,)), ((), ())), preferred_element_type=F32)


def _embed_ln_kernel(x_ref, g_ref, b_ref, hb_ref, st_ref):
    x = x_ref[...]
    mu = jnp.mean(x, -1, keepdims=True)
    xc = x - mu
    rs = lax.rsqrt(jnp.mean(xc * xc, -1, keepdims=True) + LN_EPS)
    hb_ref[...] = (xc * rs * g_ref[...] + b_ref[...]).astype(BF16)
    lane = lax.broadcasted_iota(I32, st_ref.shape, 1)
    st_ref[...] = jnp.where(lane == 0, mu, jnp.where(lane == 1, rs, 0.0))


def _embed_ln(x, g, b, tm, name):
    m, d = x.shape
    return pl.pallas_call(
        _embed_ln_kernel,
        grid=(m // tm,),
        in_specs=[pl.BlockSpec((tm, d), lambda i: (i, 0)),
                  pl.BlockSpec((1, d), lambda i: (0, 0)),
                  pl.BlockSpec((1, d), lambda i: (0, 0))],
        out_specs=[pl.BlockSpec((tm, d), lambda i: (i, 0)),
                   pl.BlockSpec((tm, LANES), lambda i: (i, 0))],
        out_shape=[jax.ShapeDtypeStruct((m, d), BF16), jax.ShapeDtypeStruct((m, LANES), F32)],
        compiler_params=_params(1),
        name=name,
    )(x, g, b)


def _inproj_plain_kernel(a_ref, am_ref, w_ref, o_ref, om_ref, wb_ref):
    @pl.when(pl.program_id(1) == 0)
    def _():
        wb_ref[...] = w_ref[...].astype(BF16)
        om_ref[...] = _dot_nt(am_ref[...], wb_ref[...])
    o_ref[...] = _dot_nt(a_ref[...], wb_ref[...])


def _inproj_conv_kernel(a_ref, am_ref, w_ref, cw_ref, o_ref, om_ref, wb_ref, hbuf_ref):
    i = pl.program_id(1)
    tm = a_ref.shape[0]
    kw = cw_ref.shape[0]

    def conv_silu(rows):
        acc = cw_ref[kw - 1:kw, :] * hbuf_ref[SUBLANES:SUBLANES + rows, :]
        for j in range(kw - 1):
            off = SUBLANES - (kw - 1) + j
            acc = acc + cw_ref[j:j + 1, :] * hbuf_ref[off:off + rows, :]
        return _silu(acc)

    @pl.when(i == 0)
    def _():
        wb_ref[...] = w_ref[...].astype(BF16)
        hbuf_ref[0:SUBLANES, :] = jnp.zeros((SUBLANES, hbuf_ref.shape[1]), F32)
        hbuf_ref[SUBLANES:SUBLANES + N_META, :] = _dot_nt(am_ref[...], wb_ref[...])
        om_ref[...] = conv_silu(N_META)
        hbuf_ref[0:SUBLANES, :] = hbuf_ref[N_META:N_META + SUBLANES, :]

    @pl.when(i > 0)
    def _():
        hbuf_ref[0:SUBLANES, :] = hbuf_ref[tm:tm + SUBLANES, :]

    hbuf_ref[SUBLANES:SUBLANES + tm, :] = _dot_nt(a_ref[...], wb_ref[...])
    o_ref[...] = conv_silu(tm)


def _inproj_glu_kernel(a_ref, am_ref, w1_ref, w2_ref, o_ref, om_ref, w1b_ref, w2b_ref):
    @pl.when(pl.program_id(1) == 0)
    def _():
        w1b_ref[...] = w1_ref[...].astype(BF16)
        w2b_ref[...] = w2_ref[...].astype(BF16)
        am = am_ref[...]
        om_ref[...] = _dot_nt(am, w1b_ref[...]) * _sigmoid(_dot_nt(am, w2b_ref[...]))
    a = a_ref[...]
    o_ref[...] = _dot_nt(a, w1b_ref[...]) * _sigmoid(_dot_nt(a, w2b_ref[...]))


def _decay_epilogue(p, alog, dtb, n_heads):
    x = p + dtb
    softplus = jnp.maximum(x, 0.0) + jnp.log1p(jnp.exp(-jnp.abs(x)))
    g = -jnp.exp(alog) * softplus
    lane = lax.broadcasted_iota(I32, p.shape, 1)
    return jnp.where(lane < n_heads, _sigmoid(p), g)


def _inproj_decay_kernel(a_ref, am_ref, w_ref, alog_ref, dtb_ref, o_ref, om_ref, wb_ref, *, n_heads):
    @pl.when(pl.program_id(1) == 0)
    def _():
        wb_ref[...] = w_ref[...].astype(BF16)
        om_ref[...] = _decay_epilogue(_dot_nt(am_ref[...], wb_ref[...]), alog_ref[...], dtb_ref[...], n_heads)
    o_ref[...] = _decay_epilogue(_dot_nt(a_ref[...], wb_ref[...]), alog_ref[...], dtb_ref[...], n_heads)


def _swiglu_kernel(a_ref, w1_ref, w2_ref, o_ref, w1b_ref, w2b_ref):
    @pl.when(pl.program_id(1) == 0)
    def _():
        w1b_ref[...] = w1_ref[...].astype(BF16)
        w2b_ref[...] = w2_ref[...].astype(BF16)
    a = a_ref[...]
    o_ref[...] = (_silu(_dot(a, w1b_ref[...])) * _dot(a, w2b_ref[...])).astype(o_ref.dtype)


def _proj_residual_kernel(a1_ref, a2_ref, w_ref, x_ref, st_ref, g_ref, b_ref, o_ref, wb_ref, *, alpha, k1):
    @pl.when(pl.program_id(1) == 0)
    def _():
        wb_ref[...] = w_ref[...].astype(BF16)
    acc = _dot(a1_ref[...], wb_ref[0:k1, :]) + _dot(a2_ref[...], wb_ref[k1:, :])
    st = st_ref[...]
    resid = (x_ref[...] - st[:, 0:1]) * st[:, 1:2] * g_ref[...] + b_ref[...]
    o_ref[...] = alpha * resid + acc


def _inproj_plain(a, am, w_t, row0, n, tm, tn):
    m, k = a.shape
    mm = am.shape[0]
    rb = row0 // tn
    return pl.pallas_call(
        _inproj_plain_kernel,
        grid=(n // tn, m // tm),
        in_specs=[pl.BlockSpec((tm, k), lambda j, i: (i, 0)),
                  pl.BlockSpec((mm, k), lambda j, i: (0, 0)),
                  pl.BlockSpec((tn, k), lambda j, i: (j + rb, 0))],
        out_specs=[pl.BlockSpec((tm, tn), lambda j, i: (i, j)),
                   pl.BlockSpec((mm, tn), lambda j, i: (0, j))],
        out_shape=[jax.ShapeDtypeStruct((m, n), F32), jax.ShapeDtypeStruct((mm, n), F32)],
        scratch_shapes=[pltpu.VMEM((tn, k), BF16)],
        compiler_params=_params(2),
        name="inproj_z",
    )(a, am, w_t)


def _inproj_conv(a, am, w_t, conv_w, row0, n, tm, tn):
    m, k = a.shape
    mm = am.shape[0]
    kw = conv_w.shape[0]
    assert kw <= SUBLANES and mm == N_META and N_META % SUBLANES == 0
    rb = row0 // tn
    return pl.pallas_call(
        _inproj_conv_kernel,
        grid=(n // tn, m // tm),
        in_specs=[pl.BlockSpec((tm, k), lambda j, i: (i, 0)),
                  pl.BlockSpec((mm, k), lambda j, i: (0, 0)),
                  pl.BlockSpec((tn, k), lambda j, i: (j + rb, 0)),
                  pl.BlockSpec((kw, tn), lambda j, i: (0, j))],
        out_specs=[pl.BlockSpec((tm, tn), lambda j, i: (i, j)),
                   pl.BlockSpec((mm, tn), lambda j, i: (0, j))],
        out_shape=[jax.ShapeDtypeStruct((m, n), F32), jax.ShapeDtypeStruct((mm, n), F32)],
        scratch_shapes=[pltpu.VMEM((tn, k), BF16), pltpu.VMEM((SUBLANES + tm, tn), F32)],
        compiler_params=_params(2),
        name="inproj_qkv_conv",
    )(a, am, w_t, conv_w)


def _inproj_glu(a, am, w_t, row1, row2, n, tm, tn):
    m, k = a.shape
    mm = am.shape[0]
    r1, r2 = row1 // tn, row2 // tn
    return pl.pallas_call(
        _inproj_glu_kernel,
        grid=(n // tn, m // tm),
        in_specs=[pl.BlockSpec((tm, k), lambda j, i: (i, 0)),
                  pl.BlockSpec((mm, k), lambda j, i: (0, 0)),
                  pl.BlockSpec((tn, k), lambda j, i: (j + r1, 0)),
                  pl.BlockSpec((tn, k), lambda j, i: (j + r2, 0))],
        out_specs=[pl.BlockSpec((tm, tn), lambda j, i: (i, j)),
                   pl.BlockSpec((mm, tn), lambda j, i: (0, j))],
        out_shape=[jax.ShapeDtypeStruct((m, n), F32), jax.ShapeDtypeStruct((mm, n), F32)],
        scratch_shapes=[pltpu.VMEM((tn, k), BF16), pltpu.VMEM((tn, k), BF16)],
        compiler_params=_params(2),
        name="inproj_glu",
    )(a, am, w_t, w_t)


def _inproj_decay(a, am, w_ba_t, alog_row, dtb_row, tm, n_heads):
    m, k = a.shape
    mm = am.shape[0]
    n = w_ba_t.shape[0]
    return pl.pallas_call(
        functools.partial(_inproj_decay_kernel, n_heads=n_heads),
        grid=(1, m // tm),
        in_specs=[pl.BlockSpec((tm, k), lambda j, i: (i, 0)),
                  pl.BlockSpec((mm, k), lambda j, i: (0, 0)),
                  pl.BlockSpec((n, k), lambda j, i: (0, 0)),
                  pl.BlockSpec((1, n), lambda j, i: (0, 0)),
                  pl.BlockSpec((1, n), lambda j, i: (0, 0))],
        out_specs=[pl.BlockSpec((tm, n), lambda j, i: (i, 0)),
                   pl.BlockSpec((mm, n), lambda j, i: (0, 0))],
        out_shape=[jax.ShapeDtypeStruct((m, n), F32), jax.ShapeDtypeStruct((mm, n), F32)],
        scratch_shapes=[pltpu.VMEM((n, k), BF16)],
        compiler_params=_params(2),
        name="inproj_decay",
    )(a, am, w_ba_t, alog_row, dtb_row)


def _swiglu(a, w1, w2, tm, tn, out_dtype):
    m, k = a.shape
    n = w1.shape[1]
    return pl.pallas_call(
        _swiglu_kernel,
        grid=(n // tn, m // tm),
        in_specs=[pl.BlockSpec((tm, k), lambda j, i: (i, 0)),
                  pl.BlockSpec((k, tn), lambda j, i: (0, j)),
                  pl.BlockSpec((k, tn), lambda j, i: (0, j))],
        out_specs=pl.BlockSpec((tm, tn), lambda j, i: (i, j)),
        out_shape=jax.ShapeDtypeStruct((m, n), out_dtype),
        scratch_shapes=[pltpu.VMEM((k, tn), BF16), pltpu.VMEM((k, tn), BF16)],
        compiler_params=_params(2),
        name="shared_swiglu",
    )(a, w1, w2)


def _proj_residual(a1, a2, w, x, stats, g, b, alpha, tm, tn):
    m, k1 = a1.shape
    k2 = a2.shape[1]
    n = w.shape[1]
    return pl.pallas_call(
        functools.partial(_proj_residual_kernel, alpha=alpha, k1=k1),
        grid=(n // tn, m // tm),
        in_specs=[pl.BlockSpec((tm, k1), lambda j, i: (i, 0)),
                  pl.BlockSpec((tm, k2), lambda j, i: (i, 0)),
                  pl.BlockSpec((k1 + k2, tn), lambda j, i: (0, j)),
                  pl.BlockSpec((tm, tn), lambda j, i: (i, j)),
                  pl.BlockSpec((tm, LANES), lambda j, i: (i, 0)),
                  pl.BlockSpec((1, tn), lambda j, i: (0, j)),
                  pl.BlockSpec((1, tn), lambda j, i: (0, j))],
        out_specs=pl.BlockSpec((tm, tn), lambda j, i: (i, j)),
        out_shape=jax.ShapeDtypeStruct((m, n), F32),
        scratch_shapes=[pltpu.VMEM((k1 + k2, tn), BF16)],
        compiler_params=_params(2),
        name="outproj_residual",
    )(a1, a2, w, x, stats, g, b)


def _conv_module_kernel(c_ref, cm_ref, w_ref, b_ref, g_ref, be_ref, o_ref, buf_ref, sh_ref, acc_ref, *, ts, kw):
    i = pl.program_id(0)
    ch = c_ref.shape[1]

    @pl.when(i == 0)
    def _():
        buf_ref[0:CONV_HALO - N_META, :] = jnp.zeros((CONV_HALO - N_META, ch), F32)
        buf_ref[CONV_HALO - N_META:CONV_HALO, :] = cm_ref[...]

    @pl.when(i > 0)
    def _():
        buf_ref[0:CONV_HALO, :] = buf_ref[ts:ts + CONV_HALO, :]

    buf_ref[CONV_HALO:CONV_HALO + ts, :] = c_ref[...]
    base = CONV_HALO - (kw - 1)
    span = ts + CONV_HALO - SUBLANES
    for r in range(1, SUBLANES):
        sh_ref[r - 1] = buf_ref[r:r + span, :]

    def row_block(rb, carry):
        r0 = pl.multiple_of(rb * CONV_ROWS, CONV_ROWS)
        for cb in range(ch // CONV_LANES):
            lanes = slice(cb * CONV_LANES, (cb + 1) * CONV_LANES)
            acc = jnp.zeros((CONV_ROWS, CONV_LANES), F32) + b_ref[:, lanes]
            for j in range(kw):
                off = base + j
                r, q = off % SUBLANES, off - off % SUBLANES
                if r == 0:
                    src = buf_ref[pl.ds(r0 + q, CONV_ROWS), lanes]
                else:
                    src = sh_ref[r - 1, pl.ds(r0 + q, CONV_ROWS), lanes]
                acc = acc + w_ref[j:j + 1, lanes] * src
            acc_ref[pl.ds(r0, CONV_ROWS), lanes] = acc
        return carry
    lax.fori_loop(0, ts // CONV_ROWS, row_block, 0)
    y = _ln_rows(acc_ref[...], g_ref[...], be_ref[...])
    o_ref[...] = _silu(y).astype(o_ref.dtype)


def _conv_module(c, c_meta, w, b, g, be, ts):
    t, ch = c.shape
    kw = w.shape[0]
    assert kw - 1 <= CONV_HALO and N_META <= CONV_HALO and ts >= CONV_HALO
    assert ts % CONV_ROWS == 0 and ch % CONV_LANES == 0
    return pl.pallas_call(
        functools.partial(_conv_module_kernel, ts=ts, kw=kw),
        grid=(t // ts,),
        in_specs=[pl.BlockSpec((ts, ch), lambda i: (i, 0)),
                  pl.BlockSpec((N_META, ch), lambda i: (0, 0)),
                  pl.BlockSpec((kw, ch), lambda i: (0, 0)),
                  pl.BlockSpec((1, ch), lambda i: (0, 0)),
                  pl.BlockSpec((1, ch), lambda i: (0, 0)),
                  pl.BlockSpec((1, ch), lambda i: (0, 0))],
        out_specs=pl.BlockSpec((ts, ch), lambda i: (i, 0)),
        out_shape=jax.ShapeDtypeStruct((t, ch), BF16),
        scratch_shapes=[pltpu.VMEM((CONV_HALO + ts, ch), F32),
                        pltpu.VMEM((SUBLANES - 1, ts + CONV_HALO - SUBLANES, ch), F32),
                        pltpu.VMEM((ts, ch), F32)],
        compiler_params=_params(1),
        name="conv_module",
    )(c, c_meta, w, b, g, be)


def _gdn_kernel(ax_ref, am_ref, zx_ref, zm_ref, bg_ref, gt_ref, gn_ref, o_ref, s_ref, *, n_heads):
    c = pl.program_id(0)
    gw = n_heads * HEAD_DIM
    C = CHUNK

    @pl.when(c == 0)
    def _():
        s_ref[...] = jnp.zeros(s_ref.shape, F32)

    def act(col0):
        cols = slice(col0, col0 + HEAD_DIM)
        return jnp.where(c == 0, am_ref[:, cols], ax_ref[:, cols])

    bg = bg_ref[...]
    beta_all = bg[:, 0:n_heads]
    g_all = bg[:, n_heads:2 * n_heads]
    row = lax.broadcasted_iota(I32, (C, C), 0)
    col = lax.broadcasted_iota(I32, (C, C), 1)
    incl = row >= col
    strict = row > col
    tri_incl = jnp.where(incl, 1.0, 0.0).astype(F32)
    tri_upper = jnp.where(row <= col, 1.0, 0.0).astype(F32)
    hp = lax.Precision.HIGHEST
    gc_cols = jnp.dot(tri_incl, g_all, precision=hp, preferred_element_type=F32)
    gc_rows = jnp.dot(gt_ref[0], tri_upper, precision=hp, preferred_element_type=F32)
    eye = jnp.where(row == col, 1.0, 0.0).astype(F32)
    right_half = lax.broadcasted_iota(I32, (C, 2 * C), 1) >= C
    gn = gn_ref[...]

    for h0 in range(0, n_heads, HEAD_GROUP):
        heads = range(h0, min(h0 + HEAD_GROUP, n_heads))
        st = {h: {} for h in heads}

        for h in heads:
            d = st[h]
            q = act(h * HEAD_DIM)
            k = act(gw + h * HEAD_DIM)
            q = q * lax.rsqrt(jnp.sum(q * q, -1, keepdims=True) + RMS_EPS) * (HEAD_DIM ** -0.5)
            k = k * lax.rsqrt(jnp.sum(k * k, -1, keepdims=True) + RMS_EPS)
            beta = beta_all[:, h:h + 1]
            gcol = gc_cols[:, h:h + 1]
            grow = gc_rows[h:h + 1, :]
            g_last = gc_cols[C - 1:C, h:h + 1]
            d["decay"] = jnp.where(incl, jnp.exp(jnp.where(incl, gcol - grow, 0.0)), 0.0)
            egc = jnp.exp(gcol)
            kb = k * beta
            d["a1"] = _dot_nt(jnp.concatenate([kb, q], axis=0).astype(BF16), k.astype(BF16))
            v = act(2 * gw + h * HEAD_DIM)
            d["rhs"] = jnp.concatenate([v * beta, kb * egc], axis=1).astype(BF16)
            d["q_dec"] = q * egc
            d["k_dec"] = (k * jnp.exp(g_last - gcol)).astype(BF16)
            d["s_scale"] = jnp.exp(g_last)

        for h in heads:
            d = st[h]
            lmat = jnp.where(strict, d["a1"][0:C] * d["decay"], 0.0)
            d["attn"] = (d["a1"][C:2 * C] * d["decay"]).astype(BF16)
            d["wmat"] = jnp.concatenate([-lmat, eye], axis=1)

        for _ in range(6):
            for h in heads:
                d = st[h]
                wmat = d["wmat"]
                r = _dot(wmat[:, 0:C].astype(BF16), wmat.astype(BF16))
                d["wmat"] = r + jnp.where(right_half, wmat, 0.0)

        for h in heads:
            d = st[h]
            tinv = jnp.where(right_half, d["wmat"], 0.0).astype(BF16)
            d["sol"] = _dot(tinv, jnp.concatenate([d["rhs"], d["rhs"]], axis=0))

        for h in heads:
            d = st[h]
            w = d["sol"][:, HEAD_DIM:2 * HEAD_DIM]
            d["ws_qs"] = _dot(jnp.concatenate([w, d["q_dec"]], axis=0).astype(BF16), s_ref[h].astype(BF16))

        for h in heads:
            d = st[h]
            vnb = (d["sol"][:, 0:HEAD_DIM] - d["ws_qs"][0:C]).astype(BF16)
            d["o"] = d["ws_qs"][C:2 * C] + _dot(d["attn"], vnb)
            s_ref[h] = s_ref[h] * d["s_scale"] + _dot_tn(d["k_dec"], vnb)

        for h in heads:
            o = st[h]["o"]
            sl = slice(h * HEAD_DIM, (h + 1) * HEAD_DIM)
            o = o * lax.rsqrt(jnp.mean(o * o, -1, keepdims=True) + RMS_EPS) * gn
            z = jnp.where(c == 0, zm_ref[:, sl], zx_ref[:, sl])
            o_ref[:, sl] = (o * _silu(z)).astype(o_ref.dtype)


def _gdn(qkv, qkv_meta, z, z_meta, bg_all, gt3, gn_row, n_heads):
    t = qkv.shape[0]
    gw = n_heads * HEAD_DIM
    n_chunks = t // CHUNK + 1
    xmap = lambda c: (jnp.maximum(c - 1, 0), 0)
    fixed = lambda c: (0, 0)
    return pl.pallas_call(
        functools.partial(_gdn_kernel, n_heads=n_heads),
        grid=(n_chunks,),
        in_specs=[pl.BlockSpec((CHUNK, 3 * gw), xmap),
                  pl.BlockSpec((CHUNK, 3 * gw), fixed),
                  pl.BlockSpec((CHUNK, gw), xmap),
                  pl.BlockSpec((CHUNK, gw), fixed),
                  pl.BlockSpec((CHUNK, 2 * n_heads), lambda c: (c, 0)),
                  pl.BlockSpec((1, n_heads, CHUNK), lambda c: (c, 0, 0)),
                  pl.BlockSpec((1, HEAD_DIM), fixed)],
        out_specs=pl.BlockSpec((CHUNK, gw), xmap),
        out_shape=jax.ShapeDtypeStruct((t, gw), BF16),
        scratch_shapes=[pltpu.VMEM((n_heads, HEAD_DIM, HEAD_DIM), F32)],
        compiler_params=_params(1),
        name="gated_delta_rule",
    )(qkv, qkv_meta, z, z_meta, bg_all, gt3, gn_row)


def _router_kernel(r_ref, g_ref, b_ref, rwt_ref, bias_ref, h1_ref, h1b_ref, h1p_ref,
                   eidx_ref, wsel_ref, pos_ref, cnt_ref, carry_ref, *, tm, n_exp):
    i = pl.program_id(0)
    per = n_exp // N_EXPERT_GROUPS

    @pl.when(i == 0)
    def _():
        carry_ref[...] = jnp.zeros(carry_ref.shape, F32)

    h1 = _ln_rows(r_ref[...], g_ref[...], b_ref[...])
    h1_ref[...] = h1
    h1b_ref[...] = h1.astype(BF16)
    h1p_ref[...] = _pack_rows(h1)

    logits = _dot_nt(rwt_ref[...], h1, precision=lax.Precision.HIGHEST)
    scores = _sigmoid(logits)
    biased = scores + bias_ref[...]
    neg_inf = F32(-jnp.inf)

    gs_rows = []
    sub = lax.broadcasted_iota(I32, (per, tm), 0)
    for g in range(N_EXPERT_GROUPS):
        xg = biased[g * per:(g + 1) * per, :]
        m1 = jnp.max(xg, axis=0, keepdims=True)
        first = jnp.min(jnp.where(xg == m1, sub, per), axis=0, keepdims=True)
        m2 = jnp.max(jnp.where(sub == first, neg_inf, xg), axis=0, keepdims=True)
        gs_rows.append(m1 + m2)
    gs = jnp.concatenate(gs_rows, axis=0)
    gid = lax.broadcasted_iota(I32, (N_EXPERT_GROUPS, tm), 0)
    grank = jnp.zeros((N_EXPERT_GROUPS, tm), F32)
    for g in range(N_EXPERT_GROUPS):
        rowv = gs[g:g + 1, :]
        grank = grank + jnp.where(rowv > gs, 1.0, jnp.where((rowv == gs) & (gid > g), 1.0, 0.0))
    gsel = jnp.where(grank < TOPK_GROUPS, 1.0, 0.0)
    emask = jnp.concatenate(
        [jnp.broadcast_to(gsel[g:g + 1, :], (per, tm)) for g in range(N_EXPERT_GROUPS)], axis=0)
    masked = jnp.where(emask > 0.0, biased, neg_inf)
    eid = lax.broadcasted_iota(I32, (n_exp, tm), 0)
    erank = jnp.zeros((n_exp, tm), F32)
    for e in range(n_exp):
        rowv = masked[e:e + 1, :]
        erank = erank + jnp.where(rowv > masked, 1.0, jnp.where((rowv == masked) & (eid > e), 1.0, 0.0))
    sel = jnp.where(erank < TOP_K, emask, 0.0)
    wdense = scores * sel
    gate = wdense / jnp.sum(wdense, axis=0, keepdims=True) * ROUTED_SCALE

    er = lax.broadcasted_iota(I32, (n_exp, n_exp), 0)
    ec = lax.broadcasted_iota(I32, (n_exp, n_exp), 1)
    lower = jnp.where(er > ec, 1.0, 0.0).astype(BF16)
    selb = sel.astype(BF16)
    slot = _dot(lower, selb)
    tr = lax.broadcasted_iota(I32, (tm, tm), 0)
    tc = lax.broadcasted_iota(I32, (tm, tm), 1)
    upper = jnp.where(tr < tc, 1.0, 0.0).astype(BF16)
    carry = carry_ref[:, 0:1]
    pos = _dot(selb, upper) + carry
    eid_f = eid.astype(F32)
    e_rows, w_rows, p_rows = [], [], []
    for k in range(TOP_K):
        mk = jnp.where(slot == float(k), sel, 0.0)
        e_rows.append(jnp.sum(mk * eid_f, axis=0, keepdims=True))
        w_rows.append(jnp.sum(mk * gate, axis=0, keepdims=True))
        p_rows.append(jnp.sum(mk * pos, axis=0, keepdims=True))
    eidx_ref[...] = jnp.concatenate(e_rows, axis=0).astype(I32)
    wsel_ref[...] = jnp.concatenate(w_rows, axis=0)
    pos_ref[...] = jnp.concatenate(p_rows, axis=0).astype(I32)
    new_carry = carry + jnp.sum(sel, axis=1, keepdims=True)
    carry_ref[...] = jnp.broadcast_to(new_carry, carry_ref.shape)
    cnt_ref[...] = jnp.broadcast_to(new_carry, cnt_ref.shape).astype(I32)


def _router(r, g, b, rwt, bias_col, tm):
    t, d = r.shape
    n_exp = rwt.shape[0]
    row = lambda i: (i, 0)
    colb = lambda i: (0, i)
    fixed = lambda i: (0, 0)
    return pl.pallas_call(
        functools.partial(_router_kernel, tm=tm, n_exp=n_exp),
        grid=(t // tm,),
        in_specs=[pl.BlockSpec((tm, d), row),
                  pl.BlockSpec((1, d), fixed),
                  pl.BlockSpec((1, d), fixed),
                  pl.BlockSpec((n_exp, d), fixed),
                  pl.BlockSpec((n_exp, 1), fixed)],
        out_specs=[pl.BlockSpec((tm, d), row),
                   pl.BlockSpec((tm, d), row),
                   pl.BlockSpec((tm, d // 2), row),
                   pl.BlockSpec((TOP_K, tm), colb),
                   pl.BlockSpec((TOP_K, tm), colb),
                   pl.BlockSpec((TOP_K, tm), colb),
                   pl.BlockSpec((n_exp, 128), fixed)],
        out_shape=[jax.ShapeDtypeStruct((t, d), F32),
                   jax.ShapeDtypeStruct((t, d), BF16),
                   jax.ShapeDtypeStruct((t, d // 2), U32),
                   jax.ShapeDtypeStruct((TOP_K, t), I32),
                   jax.ShapeDtypeStruct((TOP_K, t), F32),
                   jax.ShapeDtypeStruct((TOP_K, t), I32),
                   jax.ShapeDtypeStruct((n_exp, 128), I32)],
        scratch_shapes=[pltpu.VMEM((n_exp, 128), F32)],
        compiler_params=_params(1),
        name="ln1_router",
    )(r, g, b, rwt, bias_col)


def _dest_kernel(gstart_ref, eidx_ref, pos_ref, dest_ref, *, n_exp):
    eidx = eidx_ref[...]
    acc = pos_ref[...]
    for e in range(n_exp):
        acc = acc + jnp.where(eidx == e, gstart_ref[e], 0)
    dest_ref[...] = acc


def _dest_rows(gstart, eidx, pos):
    n_exp = gstart.shape[0]
    return pl.pallas_call(
        functools.partial(_dest_kernel, n_exp=n_exp),
        grid_spec=pltpu.PrefetchScalarGridSpec(
            num_scalar_prefetch=1,
            grid=(1,),
            in_specs=[pl.BlockSpec(eidx.shape, lambda i, gs: (0, 0)),
                      pl.BlockSpec(pos.shape, lambda i, gs: (0, 0))],
            out_specs=pl.BlockSpec(eidx.shape, lambda i, gs: (0, 0))),
        out_shape=jax.ShapeDtypeStruct(eidx.shape, I32),
        compiler_params=_params(1),
        name="dest_rows",
    )(gstart, eidx, pos)


def _dispatch_kernel(dest_ref, hp_ref, xs_ref, sem, *, tm):
    def issue(t, carry):
        for k in range(TOP_K):
            d = dest_ref[0, 0, t * TOP_K + k]
            pltpu.make_async_copy(hp_ref.at[pl.ds(t, 1)], xs_ref.at[pl.ds(d, 1)], sem).start()
        return carry
    lax.fori_loop(0, tm, issue, 0)
    pltpu.make_async_copy(xs_ref.at[pl.ds(0, tm * TOP_K)], xs_ref.at[pl.ds(0, tm * TOP_K)], sem).wait()


def _dispatch(dest_tiles, h1p, tm):
    t, dw = h1p.shape
    return pl.pallas_call(
        functools.partial(_dispatch_kernel, tm=tm),
        grid=(t // tm,),
        in_specs=[pl.BlockSpec((1, 1, tm * TOP_K), lambda i: (i, 0, 0), memory_space=pltpu.SMEM),
                  pl.BlockSpec((tm, dw), lambda i: (i, 0))],
        out_specs=pl.BlockSpec(memory_space=pl.ANY),
        out_shape=jax.ShapeDtypeStruct((t * TOP_K, dw), U32),
        scratch_shapes=[pltpu.SemaphoreType.DMA(())],
        compiler_params=_params(1),
        name="dispatch_rows",
    )(dest_tiles, h1p)


def _pack_rows(y):
    n = y.shape[1] // 2
    yb = y.astype(BF16)
    hi = lax.bitcast_convert_type(yb[:, 0:n].astype(F32), U32)
    lo = lax.bitcast_convert_type(yb[:, n:].astype(F32), U32)
    return hi | (lo >> 16)


def _unpack_rows_f32(xu):
    hi = lax.bitcast_convert_type(xu & jnp.uint32(0xFFFF0000), F32)
    lo = lax.bitcast_convert_type(xu << 16, F32)
    return hi, lo


def _unpack_rows(xu):
    hi, lo = _unpack_rows_f32(xu)
    return hi.astype(BF16), lo.astype(BF16)


def _expert_weights(v, ve_ref, vfe_ref, eo_ref, vne_ref, hbm_refs, f32_refs, bf16_refs, sems):
    @pl.when(vfe_ref[v] == 1)
    def _():
        slot = eo_ref[v] % 2
        e = ve_ref[v]

        def copies(expert, s):
            return [pltpu.make_async_copy(h.at[expert], f.at[s], sem.at[s])
                    for h, f, sem in zip(hbm_refs, f32_refs, sems)]

        @pl.when(v == 0)
        def _():
            for cp in copies(e, slot):
                cp.start()

        for cp in copies(e, slot):
            cp.wait()
        ne = vne_ref[v]

        @pl.when(ne >= 0)
        def _():
            for cp in copies(ne, 1 - slot):
                cp.start()

        for f, bf in zip(f32_refs, bf16_refs):
            bf[...] = f[slot].astype(BF16)


def _visit_subblocks(vt_ref, ve_ref, vft_ref, gs_ref, ge_ref, v, tm, sb, o_ref, compute):
    e = ve_ref[v]
    gs, ge = gs_ref[e], ge_ref[e]
    first = vft_ref[v] == 1
    width = o_ref.shape[1]
    for s in range(tm // sb):
        r0 = vt_ref[v] * tm + s * sb
        hit = (r0 < ge) & (r0 + sb > gs)
        rows = pl.ds(s * sb, sb)

        @pl.when(hit)
        def _():
            y = compute(s).astype(o_ref.dtype)
            owned = (r0 >= gs) & (r0 + sb <= ge)

            @pl.when(owned)
            def _():
                o_ref[rows, :] = y

            @pl.when(jnp.logical_not(owned))
            def _():
                ridx = r0 + lax.broadcasted_iota(I32, (sb, width), 0)
                mask = (ridx >= gs) & (ridx < ge)

                @pl.when(first)
                def _():
                    o_ref[rows, :] = jnp.where(mask, y, jnp.zeros_like(y))

                @pl.when(jnp.logical_not(first))
                def _():
                    o_ref[rows, :] = jnp.where(mask, y, o_ref[rows, :])

        @pl.when(jnp.logical_not(hit) & first)
        def _():
            o_ref[rows, :] = jnp.zeros((sb, width), o_ref.dtype)


def _gmm_up_kernel(vt_ref, ve_ref, vft_ref, vfe_ref, eo_ref, vne_ref, gs_ref, ge_ref, nv_ref,
                   xs_ref, wg_hbm, wu_hbm, o_ref, wgf_ref, wuf_ref, wgb_ref, wub_ref, semg, semu, *, tm, sb):
    v = pl.program_id(0)

    @pl.when(v < nv_ref[0])
    def _():
        _expert_weights(v, ve_ref, vfe_ref, eo_ref, vne_ref, (wg_hbm, wu_hbm), (wgf_ref, wuf_ref),
                        (wgb_ref, wub_ref), (semg, semu))

        def compute(s):
            hi, lo = _unpack_rows(xs_ref[s * sb:(s + 1) * sb, :])
            kh = hi.shape[1]
            hg = _dot(hi, wgb_ref[0:kh, :]) + _dot(lo, wgb_ref[kh:, :])
            hu = _dot(hi, wub_ref[0:kh, :]) + _dot(lo, wub_ref[kh:, :])
            return _silu(hg) * hu

        _visit_subblocks(vt_ref, ve_ref, vft_ref, gs_ref, ge_ref, v, tm, sb, o_ref, compute)


def _gmm_down_kernel(vt_ref, ve_ref, vft_ref, vfe_ref, eo_ref, vne_ref, gs_ref, ge_ref, nv_ref,
                     a_ref, wd_hbm, o_ref, wdf_ref, wdb_ref, semd, *, tm, sb):
    v = pl.program_id(0)

    @pl.when(v < nv_ref[0])
    def _():
        _expert_weights(v, ve_ref, vfe_ref, eo_ref, vne_ref, (wd_hbm,), (wdf_ref,), (wdb_ref,), (semd,))

        def compute(s):
            return _pack_rows(_dot(a_ref[s * sb:(s + 1) * sb, :], wdb_ref[...]))

        _visit_subblocks(vt_ref, ve_ref, vft_ref, gs_ref, ge_ref, v, tm, sb, o_ref, compute)


def _visit_schedule(counts, n_rows, tm):
    n_exp = counts.shape[0]
    n_tiles = n_rows // tm
    n_visits = n_tiles + n_exp - 1
    ends = jnp.cumsum(counts)
    starts = ends - counts
    first_tile = starts // tm
    last_tile = jnp.maximum(ends - 1, 0) // tm
    tiles_e = jnp.where(counts > 0, last_tile - first_tile + 1, 0)
    vend = jnp.cumsum(tiles_e)
    vstart = vend - tiles_e
    total = vend[-1]
    v = jnp.arange(n_visits, dtype=I32)
    vc = jnp.minimum(v, total - 1)
    ve = jnp.searchsorted(vend, vc, side="right").astype(I32)
    vt = (vc - vstart[ve] + first_tile[ve]).astype(I32)
    prev_t = jnp.concatenate([jnp.full((1,), -1, I32), vt[:-1]])
    prev_e = jnp.concatenate([jnp.full((1,), -1, I32), ve[:-1]])
    vft = (vt != prev_t).astype(I32)
    vfe = (ve != prev_e).astype(I32)
    eo = (jnp.cumsum(vfe) - 1).astype(I32)
    eidx = jnp.arange(n_exp, dtype=I32)
    later = lax.cummin(jnp.where(counts > 0, eidx, n_exp), axis=0, reverse=True)
    nxt = jnp.concatenate([later[1:], jnp.full((1,), n_exp, I32)])
    vne = jnp.where(nxt < n_exp, nxt, -1)[ve].astype(I32)
    sched = (vt, ve, vft, vfe, eo, vne, starts.astype(I32), ends.astype(I32), total.reshape(1).astype(I32))
    return sched, n_visits


def _gmm_up(sched, n_visits, xs, wg, wu, tm, sb):
    n, dw = xs.shape
    n_exp, d, f = wg.shape
    return pl.pallas_call(
        functools.partial(_gmm_up_kernel, tm=tm, sb=sb),
        grid_spec=pltpu.PrefetchScalarGridSpec(
            num_scalar_prefetch=len(sched),
            grid=(n_visits,),
            in_specs=[pl.BlockSpec((tm, dw), lambda v, vt, *_: (vt[v], 0)),
                      pl.BlockSpec(memory_space=pl.ANY),
                      pl.BlockSpec(memory_space=pl.ANY)],
            out_specs=pl.BlockSpec((tm, f), lambda v, vt, *_: (vt[v], 0)),
            scratch_shapes=[pltpu.VMEM((2, d, f), F32), pltpu.VMEM((2, d, f), F32),
                            pltpu.VMEM((d, f), BF16), pltpu.VMEM((d, f), BF16),
                            pltpu.SemaphoreType.DMA((2,)), pltpu.SemaphoreType.DMA((2,))]),
        out_shape=jax.ShapeDtypeStruct((n, f), BF16),
        compiler_params=_params(1),
        name="experts_gate_up",
    )(*sched, xs, wg, wu)


def _gmm_down(sched, n_visits, act, wd, tm, sb):
    n, f = act.shape
    n_exp, _, d = wd.shape
    return pl.pallas_call(
        functools.partial(_gmm_down_kernel, tm=tm, sb=sb),
        grid_spec=pltpu.PrefetchScalarGridSpec(
            num_scalar_prefetch=len(sched),
            grid=(n_visits,),
            in_specs=[pl.BlockSpec((tm, f), lambda v, vt, *_: (vt[v], 0)),
                      pl.BlockSpec(memory_space=pl.ANY)],
            out_specs=pl.BlockSpec((tm, d // 2), lambda v, vt, *_: (vt[v], 0)),
            scratch_shapes=[pltpu.VMEM((2, f, d), F32), pltpu.VMEM((f, d), BF16),
                            pltpu.SemaphoreType.DMA((2,))]),
        out_shape=jax.ShapeDtypeStruct((n, d // 2), U32),
        compiler_params=_params(1),
        name="experts_down",
    )(*sched, act, wd)


def _combine_kernel(dcur_ref, dnxt_ref, w_ref, h1_ref, as_ref, wsd_ref, g_ref, b_ref, ys_ref, o_ref,
                    gbuf_ref, ybuf_ref, sem, *, tm, alpha):
    i = pl.program_id(0)
    n = pl.num_programs(0)
    slot = i % 2
    groups = tm // SUBLANES
    d = o_ref.shape[1]
    half = d // 2

    def issue_token(dref, dst_slot, t):
        for k in range(TOP_K):
            src = dref[0, 0, t * TOP_K + k]
            pltpu.make_async_copy(ys_ref.at[pl.ds(src, 1)], gbuf_ref.at[dst_slot, k, pl.ds(t, 1)],
                                  sem.at[dst_slot]).start()

    def wait_slot(s):
        for k in range(TOP_K):
            pltpu.make_async_copy(ys_ref.at[pl.ds(0, tm)], gbuf_ref.at[s, k], sem.at[s]).wait()

    @pl.when(i == 0)
    def _():
        def first(t, carry):
            issue_token(dcur_ref, 0, t)
            return carry
        lax.fori_loop(0, tm, first, 0)

    ybuf_ref[...] = alpha * h1_ref[...] + _dot(as_ref[...], wsd_ref[...])
    wait_slot(slot)

    def reduce_group(r, carry):
        r8 = pl.multiple_of(r * SUBLANES, SUBLANES)
        rows = pl.ds(r8, SUBLANES)
        y_hi = ybuf_ref[rows, 0:half]
        y_lo = ybuf_ref[rows, half:]
        w = w_ref[rows, :]
        for j in range(SUBLANES):
            issue_token(dnxt_ref, 1 - slot, r8 + j)
            for k in range(j * TOP_K // SUBLANES, (j + 1) * TOP_K // SUBLANES):
                e_hi, e_lo = _unpack_rows_f32(gbuf_ref[slot, k, rows, :])
                wk = w[:, k:k + 1]
                y_hi = y_hi + wk * e_hi
                y_lo = y_lo + wk * e_lo
        ybuf_ref[rows, 0:half] = y_hi
        ybuf_ref[rows, half:] = y_lo
        return carry
    lax.fori_loop(0, groups, reduce_group, 0)
    o_ref[...] = _ln_rows(ybuf_ref[...], g_ref[...], b_ref[...])

    @pl.when(i == n - 1)
    def _():
        wait_slot(1 - slot)


def _combine(dest_tiles, w_tok, h1, act_s, wsd, g, b, ys, tm, alpha):
    t, d = h1.shape
    f = act_s.shape[1]
    n = t // tm
    row = lambda i: (i, 0)
    fixed = lambda i: (0, 0)
    return pl.pallas_call(
        functools.partial(_combine_kernel, tm=tm, alpha=alpha),
        grid=(n,),
        in_specs=[pl.BlockSpec((1, 1, tm * TOP_K), lambda i: (i, 0, 0), memory_space=pltpu.SMEM),
                  pl.BlockSpec((1, 1, tm * TOP_K), lambda i: (jnp.minimum(i + 1, n - 1), 0, 0),
                               memory_space=pltpu.SMEM),
                  pl.BlockSpec((tm, TOP_K), row),
                  pl.BlockSpec((tm, d), row),
                  pl.BlockSpec((tm, f), row),
                  pl.BlockSpec((f, d), fixed),
                  pl.BlockSpec((1, d), fixed),
                  pl.BlockSpec((1, d), fixed),
                  pl.BlockSpec(memory_space=pl.ANY)],
        out_specs=pl.BlockSpec((tm, d), row),
        out_shape=jax.ShapeDtypeStruct((t, d), F32),
        scratch_shapes=[pltpu.VMEM((2, TOP_K, tm, d // 2), U32),
                        pltpu.VMEM((tm, d), F32),
                        pltpu.SemaphoreType.DMA((2,))],
        compiler_params=_params(1),
        name="combine_ln2",
    )(dest_tiles, dest_tiles, w_tok, h1, act_s, wsd, g, b, ys)


def _tile(n, pref):
    t = min(n, pref)
    assert n % t == 0, (n, pref)
    return t


def kernel(x, meta_tokens, emb_ln_g, emb_ln_b, w_in, conv_w, conv_b, conv_ln_g, conv_ln_b, short_conv_w, a_log, dt_bias, gdn_norm_g, w_out, ln1_g, ln1_b, router_w, router_bias, expert_w_gate, expert_w_up, expert_w_down, shared_w_gate, shared_w_up, shared_w_down, ln2_g, ln2_b):
    depth = w_in.shape[0]
    assert depth == 1 and x.shape[0] == 1, "single layer, single sequence"
    t, d = x.shape[1], x.shape[2]
    cw = conv_w.shape[2]
    n_heads = a_log.shape[1]
    gw = n_heads * HEAD_DIM
    alpha = (2.0 * depth) ** 0.25
    assert meta_tokens.shape[0] == N_META and t % CHUNK == 0 and w_in.shape[2] == 2 * cw + 4 * gw + 2 * n_heads
    row = lambda a: a.reshape(1, -1)

    h_b, h_stats = _embed_ln(x[0], row(emb_ln_g), row(emb_ln_b), _tile(t, 256), "embed_ln")
    hm_b, _ = _embed_ln(meta_tokens, row(emb_ln_g), row(emb_ln_b), N_META, "embed_ln_meta")

    w_t = jnp.transpose(w_in[0])
    tm = _tile(t, 1024)
    zeros_h = jnp.zeros((n_heads,), F32)
    alog_row = row(jnp.concatenate([zeros_h, a_log[0]]))
    dtb_row = row(jnp.concatenate([zeros_h, dt_bias[0]]))
    c_x, c_m = _inproj_glu(h_b, hm_b, w_t, 0, cw, cw, tm, _tile(cw, 256))
    qkv_x, qkv_m = _inproj_conv(h_b, hm_b, w_t, short_conv_w[0], 2 * cw, 3 * gw, tm, _tile(3 * gw, 512))
    z_x, z_m = _inproj_plain(h_b, hm_b, w_t, 2 * cw + 3 * gw, gw, tm, _tile(gw, 512))
    bg_x, bg_m = _inproj_decay(h_b, hm_b, w_t[2 * cw + 4 * gw:], alog_row, dtb_row, tm, n_heads)

    y_conv = _conv_module(c_x, c_m, conv_w[0], row(conv_b[0]), row(conv_ln_g[0]), row(conv_ln_b[0]),
                          _tile(t, 256))

    front = lambda a: jnp.pad(a, ((CHUNK - N_META, 0), (0, 0)))
    bg_all = jnp.concatenate([front(bg_m), bg_x], axis=0)
    gt3 = bg_all[:, n_heads:].reshape(-1, CHUNK, n_heads).transpose(0, 2, 1)
    y_gdn = _gdn(qkv_x, front(qkv_m), z_x, front(z_m), bg_all, gt3, row(gdn_norm_g[0]), n_heads)

    r1 = _proj_residual(y_conv, y_gdn, w_out[0], x[0], h_stats, row(emb_ln_g), row(emb_ln_b), alpha, tm,
                        _tile(d, 512))
    h1, h1_b, h1_p, eidx, wsel, pos, cnt = _router(
        r1, row(ln1_g[0]), row(ln1_b[0]), router_w[0].T, router_bias[0].reshape(-1, 1), _tile(t, 256))
    counts = cnt[:, 0]
    gstart = jnp.cumsum(counts) - counts
    dest = _dest_rows(gstart.astype(I32), eidx, pos)

    tmd = _tile(t, 256)
    xs = _dispatch(dest.T.reshape(t // tmd, 1, tmd * TOP_K), h1_p, tmd)
    tmg = _tile(t * TOP_K, 512)
    sbg = _tile(tmg, 128)
    sched, n_visits = _visit_schedule(counts, t * TOP_K, tmg)
    act = _gmm_up(sched, n_visits, xs, expert_w_gate[0], expert_w_up[0], tmg, sbg)
    ys = _gmm_down(sched, n_visits, act, expert_w_down[0], tmg, _tile(tmg, 256))

    act_s = _swiglu(h1_b, shared_w_gate[0], shared_w_up[0], tm, _tile(shared_w_gate.shape[2], 256), BF16)
    tmc = _tile(t, 128)
    out = _combine(dest.T.reshape(t // tmc, 1, tmc * TOP_K), wsel.T, h1, act_s, shared_w_down[0].astype(BF16),
                   row(ln2_g[0]), row(ln2_b[0]), ys, tmc, alpha)
    return out[None]
```

```python
import functools

import jax
import jax.numpy as jnp
from jax import lax
from jax.experimental import pallas as pl
from jax.experimental.pallas import tpu as pltpu

F32 = jnp.float32
BF16 = jnp.bfloat16
I32 = jnp.int32
U32 = jnp.uint32

N_META = 16
CHUNK = 64
HEAD_DIM = 128
HEAD_GROUP = 16
TOP_K = 8
N_EXPERT_GROUPS = 8
TOPK_GROUPS = 4
ROUTED_SCALE = 2.5
LN_EPS = 1e-5
RMS_EPS = 1e-6
CONV_HALO = 32
CONV_ROWS = 32
CONV_LANES = 512
SUBLANES = 8
LANES = 128
VMEM_LIMIT = 56 * 1024 * 1024


def _params(n_axes, vmem=VMEM_LIMIT):
    return pltpu.CompilerParams(dimension_semantics=("arbitrary",) * n_axes, vmem_limit_bytes=vmem)


def _ln_rows(x, g, b):
    mu = jnp.mean(x, -1, keepdims=True)
    xc = x - mu
    var = jnp.mean(xc * xc, -1, keepdims=True)
    return xc * lax.rsqrt(var + LN_EPS) * g + b


def _sigmoid(x):
    return 1.0 / (1.0 + jnp.exp(-x))


def _silu(x):
    return x * _sigmoid(x)


def _dot(a, b):
    return jnp.dot(a, b, preferred_element_type=F32)


def _dot_nt(a, b, precision=None):
    return lax.dot_general(a, b, (((1,), (1,)), ((), ())), precision=precision, preferred_element_type=F32)


def _dot_tn(a, b):
    return lax.dot_general(a, b, (((0,), (0,)), ((), ())), preferred_element_type=F32)


def _embed_ln_kernel(x_ref, g_ref, b_ref, hb_ref, st_ref):
    x = x_ref[...]
    mu = jnp.mean(x, -1, keepdims=True)
    xc = x - mu
    rs = lax.rsqrt(jnp.mean(xc * xc, -1, keepdims=True) + LN_EPS)
    hb_ref[...] = (xc * rs * g_ref[...] + b_ref[...]).astype(BF16)
    lane = lax.broadcasted_iota(I32, st_ref.shape, 1)
    st_ref[...] = jnp.where(lane == 0, mu, jnp.where(lane == 1, rs, 0.0))


def _embed_ln(x, g, b, tm, name):
    m, d = x.shape
    return pl.pallas_call(
        _embed_ln_kernel,
        grid=(m // tm,),
        in_specs=[pl.BlockSpec((tm, d), lambda i: (i, 0)),
                  pl.BlockSpec((1, d), lambda i: (0, 0)),
                  pl.BlockSpec((1, d), lambda i: (0, 0))],
        out_specs=[pl.BlockSpec((tm, d), lambda i: (i, 0)),
                   pl.BlockSpec((tm, LANES), lambda i: (i, 0))],
        out_shape=[jax.ShapeDtypeStruct((m, d), BF16), jax.ShapeDtypeStruct((m, LANES), F32)],
        compiler_params=_params(1),
        name=name,
    )(x, g, b)


def _inproj_plain_kernel(a_ref, am_ref, w_ref, o_ref, om_ref, wb_ref):
    @pl.when(pl.program_id(1) == 0)
    def _():
        wb_ref[...] = w_ref[...].astype(BF16)
        om_ref[...] = _dot_nt(am_ref[...], wb_ref[...])
    o_ref[...] = _dot_nt(a_ref[...], wb_ref[...])


def _inproj_conv_kernel(a_ref, am_ref, w_ref, cw_ref, o_ref, om_ref, wb_ref, hbuf_ref):
    i = pl.program_id(1)
    tm = a_ref.shape[0]
    kw = cw_ref.shape[0]

    def conv_silu(rows):
        acc = cw_ref[kw - 1:kw, :] * hbuf_ref[SUBLANES:SUBLANES + rows, :]
        for j in range(kw - 1):
            off = SUBLANES - (kw - 1) + j
            acc = acc + cw_ref[j:j + 1, :] * hbuf_ref[off:off + rows, :]
        return _silu(acc)

    @pl.when(i == 0)
    def _():
        wb_ref[...] = w_ref[...].astype(BF16)
        hbuf_ref[0:SUBLANES, :] = jnp.zeros((SUBLANES, hbuf_ref.shape[1]), F32)
        hbuf_ref[SUBLANES:SUBLANES + N_META, :] = _dot_nt(am_ref[...], wb_ref[...])
        om_ref[...] = conv_silu(N_META)
        hbuf_ref[0:SUBLANES, :] = hbuf_ref[N_META:N_META + SUBLANES, :]

    @pl.when(i > 0)
    def _():
        hbuf_ref[0:SUBLANES, :] = hbuf_ref[tm:tm + SUBLANES, :]

    hbuf_ref[SUBLANES:SUBLANES + tm, :] = _dot_nt(a_ref[...], wb_ref[...])
    o_ref[...] = conv_silu(tm)


def _inproj_glu_kernel(a_ref, am_ref, w1_ref, w2_ref, o_ref, om_ref, w1b_ref, w2b_ref):
    @pl.when(pl.program_id(1) == 0)
    def _():
        w1b_ref[...] = w1_ref[...].astype(BF16)
        w2b_ref[...] = w2_ref[...].astype(BF16)
        am = am_ref[...]
        om_ref[...] = _dot_nt(am, w1b_ref[...]) * _sigmoid(_dot_nt(am, w2b_ref[...]))
    a = a_ref[...]
    o_ref[...] = _dot_nt(a, w1b_ref[...]) * _sigmoid(_dot_nt(a, w2b_ref[...]))


def _decay_epilogue(p, alog, dtb, n_heads):
    x = p + dtb
    softplus = jnp.maximum(x, 0.0) + jnp.log1p(jnp.exp(-jnp.abs(x)))
    g = -jnp.exp(alog) * softplus
    lane = lax.broadcasted_iota(I32, p.shape, 1)
    return jnp.where(lane < n_heads, _sigmoid(p), g)


def _inproj_decay_kernel(a_ref, am_ref, w_ref, alog_ref, dtb_ref, o_ref, om_ref, wb_ref, *, n_heads):
    @pl.when(pl.program_id(1) == 0)
    def _():
        wb_ref[...] = w_ref[...].astype(BF16)
        om_ref[...] = _decay_epilogue(_dot_nt(am_ref[...], wb_ref[...]), alog_ref[...], dtb_ref[...], n_heads)
    o_ref[...] = _decay_epilogue(_dot_nt(a_ref[...], wb_ref[...]), alog_ref[...], dtb_ref[...], n_heads)


def _swiglu_kernel(a_ref, w1_ref, w2_ref, o_ref, w1b_ref, w2b_ref):
    @pl.when(pl.program_id(1) == 0)
    def _():
        w1b_ref[...] = w1_ref[...].astype(BF16)
        w2b_ref[...] = w2_ref[...].astype(BF16)
    a = a_ref[...]
    o_ref[...] = (_silu(_dot(a, w1b_ref[...])) * _dot(a, w2b_ref[...])).astype(o_ref.dtype)


def _proj_residual_kernel(a1_ref, a2_ref, w_ref, x_ref, st_ref, g_ref, b_ref, o_ref, wb_ref, *, alpha, k1):
    @pl.when(pl.program_id(1) == 0)
    def _():
        wb_ref[...] = w_ref[...].astype(BF16)
    acc = _dot(a1_ref[...], wb_ref[0:k1, :]) + _dot(a2_ref[...], wb_ref[k1:, :])
    st = st_ref[...]
    resid = (x_ref[...] - st[:, 0:1]) * st[:, 1:2] * g_ref[...] + b_ref[...]
    o_ref[...] = alpha * resid + acc


def _inproj_plain(a, am, w_t, row0, n, tm, tn):
    m, k = a.shape
    mm = am.shape[0]
    rb = row0 // tn
    return pl.pallas_call(
        _inproj_plain_kernel,
        grid=(n // tn, m // tm),
        in_specs=[pl.BlockSpec((tm, k), lambda j, i: (i, 0)),
                  pl.BlockSpec((mm, k), lambda j, i: (0, 0)),
                  pl.BlockSpec((tn, k), lambda j, i: (j + rb, 0))],
        out_specs=[pl.BlockSpec((tm, tn), lambda j, i: (i, j)),
                   pl.BlockSpec((mm, tn), lambda j, i: (0, j))],
        out_shape=[jax.ShapeDtypeStruct((m, n), F32), jax.ShapeDtypeStruct((mm, n), F32)],
        scratch_shapes=[pltpu.VMEM((tn, k), BF16)],
        compiler_params=_params(2),
        name="inproj_z",
    )(a, am, w_t)


def _inproj_conv(a, am, w_t, conv_w, row0, n, tm, tn):
    m, k = a.shape
    mm = am.shape[0]
    kw = conv_w.shape[0]
    assert kw <= SUBLANES and mm == N_META and N_META % SUBLANES == 0
    rb = row0 // tn
    return pl.pallas_call(
        _inproj_conv_kernel,
        grid=(n // tn, m // tm),
        in_specs=[pl.BlockSpec((tm, k), lambda j, i: (i, 0)),
                  pl.BlockSpec((mm, k), lambda j, i: (0, 0)),
                  pl.BlockSpec((tn, k), lambda j, i: (j + rb, 0)),
                  pl.BlockSpec((kw, tn), lambda j, i: (0, j))],
        out_specs=[pl.BlockSpec((tm, tn), lambda j, i: (i, j)),
                   pl.BlockSpec((mm, tn), lambda j, i: (0, j))],
        out_shape=[jax.ShapeDtypeStruct((m, n), F32), jax.ShapeDtypeStruct((mm, n), F32)],
        scratch_shapes=[pltpu.VMEM((tn, k), BF16), pltpu.VMEM((SUBLANES + tm, tn), F32)],
        compiler_params=_params(2),
        name="inproj_qkv_conv",
    )(a, am, w_t, conv_w)


def _inproj_glu(a, am, w_t, row1, row2, n, tm, tn):
    m, k = a.shape
    mm = am.shape[0]
    r1, r2 = row1 // tn, row2 // tn
    return pl.pallas_call(
        _inproj_glu_kernel,
        grid=(n // tn, m // tm),
        in_specs=[pl.BlockSpec((tm, k), lambda j, i: (i, 0)),
                  pl.BlockSpec((mm, k), lambda j, i: (0, 0)),
                  pl.BlockSpec((tn, k), lambda j, i: (j + r1, 0)),
                  pl.BlockSpec((tn, k), lambda j, i: (j + r2, 0))],
        out_specs=[pl.BlockSpec((tm, tn), lambda j, i: (i, j)),
                   pl.BlockSpec((mm, tn), lambda j, i: (0, j))],
        out_shape=[jax.ShapeDtypeStruct((m, n), F32), jax.ShapeDtypeStruct((mm, n), F32)],
        scratch_shapes=[pltpu.VMEM((tn, k), BF16), pltpu.VMEM((tn, k), BF16)],
        compiler_params=_params(2),
        name="inproj_glu",
    )(a, am, w_t, w_t)


def _inproj_decay(a, am, w_ba_t, alog_row, dtb_row, tm, n_heads):
    m, k = a.shape
    mm = am.shape[0]
    n = w_ba_t.shape[0]
    return pl.pallas_call(
        functools.partial(_inproj_decay_kernel, n_heads=n_heads),
        grid=(1, m // tm),
        in_specs=[pl.BlockSpec((tm, k), lambda j, i: (i, 0)),
                  pl.BlockSpec((mm, k), lambda j, i: (0, 0)),
                  pl.BlockSpec((n, k), lambda j, i: (0, 0)),
                  pl.BlockSpec((1, n), lambda j, i: (0, 0)),
                  pl.BlockSpec((1, n), lambda j, i: (0, 0))],
        out_specs=[pl.BlockSpec((tm, n), lambda j, i: (i, 0)),
                   pl.BlockSpec((mm, n), lambda j, i: (0, 0))],
        out_shape=[jax.ShapeDtypeStruct((m, n), F32), jax.ShapeDtypeStruct((mm, n), F32)],
        scratch_shapes=[pltpu.VMEM((n, k), BF16)],
        compiler_params=_params(2),
        name="inproj_decay",
    )(a, am, w_ba_t, alog_row, dtb_row)


def _swiglu(a, w1, w2, tm, tn, out_dtype):
    m, k = a.shape
    n = w1.shape[1]
    return pl.pallas_call(
        _swiglu_kernel,
        grid=(n // tn, m // tm),
        in_specs=[pl.BlockSpec((tm, k), lambda j, i: (i, 0)),
                  pl.BlockSpec((k, tn), lambda j, i: (0, j)),
                  pl.BlockSpec((k, tn), lambda j, i: (0, j))],
        out_specs=pl.BlockSpec((tm, tn), lambda j, i: (i, j)),
        out_shape=jax.ShapeDtypeStruct((m, n), out_dtype),
        scratch_shapes=[pltpu.VMEM((k, tn), BF16), pltpu.VMEM((k, tn), BF16)],
        compiler_params=_params(2),
        name="shared_swiglu",
    )(a, w1, w2)


def _proj_residual(a1, a2, w, x, stats, g, b, alpha, tm, tn):
    m, k1 = a1.shape
    k2 = a2.shape[1]
    n = w.shape[1]
    return pl.pallas_call(
        functools.partial(_proj_residual_kernel, alpha=alpha, k1=k1),
        grid=(n // tn, m // tm),
        in_specs=[pl.BlockSpec((tm, k1), lambda j, i: (i, 0)),
                  pl.BlockSpec((tm, k2), lambda j, i: (i, 0)),
                  pl.BlockSpec((k1 + k2, tn), lambda j, i: (0, j)),
                  pl.BlockSpec((tm, tn), lambda j, i: (i, j)),
                  pl.BlockSpec((tm, LANES), lambda j, i: (i, 0)),
                  pl.BlockSpec((1, tn), lambda j, i: (0, j)),
                  pl.BlockSpec((1, tn), lambda j, i: (0, j))],
        out_specs=pl.BlockSpec((tm, tn), lambda j, i: (i, j)),
        out_shape=jax.ShapeDtypeStruct((m, n), F32),
        scratch_shapes=[pltpu.VMEM((k1 + k2, tn), BF16)],
        compiler_params=_params(2),
        name="outproj_residual",
    )(a1, a2, w, x, stats, g, b)


def _conv_module_kernel(c_ref, cm_ref, w_ref, b_ref, g_ref, be_ref, o_ref, buf_ref, sh_ref, acc_ref, *, ts, kw):
    i = pl.program_id(0)
    ch = c_ref.shape[1]

    @pl.when(i == 0)
    def _():
        buf_ref[0:CONV_HALO - N_META, :] = jnp.zeros((CONV_HALO - N_META, ch), F32)
        buf_ref[CONV_HALO - N_META:CONV_HALO, :] = cm_ref[...]

    @pl.when(i > 0)
    def _():
        buf_ref[0:CONV_HALO, :] = buf_ref[ts:ts + CONV_HALO, :]

    buf_ref[CONV_HALO:CONV_HALO + ts, :] = c_ref[...]
    base = CONV_HALO - (kw - 1)
    span = ts + CONV_HALO - SUBLANES
    for r in range(1, SUBLANES):
        sh_ref[r - 1] = buf_ref[r:r + span, :]

    def row_block(rb, carry):
        r0 = pl.multiple_of(rb * CONV_ROWS, CONV_ROWS)
        for cb in range(ch // CONV_LANES):
            lanes = slice(cb * CONV_LANES, (cb + 1) * CONV_LANES)
            acc = jnp.zeros((CONV_ROWS, CONV_LANES), F32) + b_ref[:, lanes]
            for j in range(kw):
                off = base + j
                r, q = off % SUBLANES, off - off % SUBLANES
                if r == 0:
                    src = buf_ref[pl.ds(r0 + q, CONV_ROWS), lanes]
                else:
                    src = sh_ref[r - 1, pl.ds(r0 + q, CONV_ROWS), lanes]
                acc = acc + w_ref[j:j + 1, lanes] * src
            acc_ref[pl.ds(r0, CONV_ROWS), lanes] = acc
        return carry
    lax.fori_loop(0, ts // CONV_ROWS, row_block, 0)
    y = _ln_rows(acc_ref[...], g_ref[...], be_ref[...])
    o_ref[...] = _silu(y).astype(o_ref.dtype)


def _conv_module(c, c_meta, w, b, g, be, ts):
    t, ch = c.shape
    kw = w.shape[0]
    assert kw - 1 <= CONV_HALO and N_META <= CONV_HALO and ts >= CONV_HALO
    assert ts % CONV_ROWS == 0 and ch % CONV_LANES == 0
    return pl.pallas_call(
        functools.partial(_conv_module_kernel, ts=ts, kw=kw),
        grid=(t // ts,),
        in_specs=[pl.BlockSpec((ts, ch), lambda i: (i, 0)),
                  pl.BlockSpec((N_META, ch), lambda i: (0, 0)),
                  pl.BlockSpec((kw, ch), lambda i: (0, 0)),
                  pl.BlockSpec((1, ch), lambda i: (0, 0)),
                  pl.BlockSpec((1, ch), lambda i: (0, 0)),
                  pl.BlockSpec((1, ch), lambda i: (0, 0))],
        out_specs=pl.BlockSpec((ts, ch), lambda i: (i, 0)),
        out_shape=jax.ShapeDtypeStruct((t, ch), BF16),
        scratch_shapes=[pltpu.VMEM((CONV_HALO + ts, ch), F32),
                        pltpu.VMEM((SUBLANES - 1, ts + CONV_HALO - SUBLANES, ch), F32),
                        pltpu.VMEM((ts, ch), F32)],
        compiler_params=_params(1),
        name="conv_module",
    )(c, c_meta, w, b, g, be)


def _lane_sums(blocks):
    rows = blocks[0].shape[0]
    x = jnp.concatenate(blocks, axis=0)
    hi = x.astype(BF16)
    lo = (x - hi.astype(F32)).astype(BF16)
    ones = jnp.ones((HEAD_DIM, HEAD_DIM), BF16)
    s = _dot(hi, ones) + _dot(lo, ones)
    return [s[n * rows:(n + 1) * rows] for n in range(len(blocks))]


def _gdn_kernel(ax_ref, am_ref, zx_ref, zm_ref, bg_ref, gt_ref, gn_ref, o_ref, s_ref, *, n_heads):
    c = pl.program_id(0)
    gw = n_heads * HEAD_DIM
    C = CHUNK

    @pl.when(c == 0)
    def _():
        s_ref[...] = jnp.zeros(s_ref.shape, F32)

    def act(col0):
        cols = slice(col0, col0 + HEAD_DIM)
        return jnp.where(c == 0, am_ref[:, cols], ax_ref[:, cols])

    bg = bg_ref[...]
    beta_all = bg[:, 0:n_heads]
    g_all = bg[:, n_heads:2 * n_heads]
    row = lax.broadcasted_iota(I32, (C, C), 0)
    col = lax.broadcasted_iota(I32, (C, C), 1)
    incl = row >= col
    strict = row > col
    tri_incl = jnp.where(incl, 1.0, 0.0).astype(F32)
    tri_upper = jnp.where(row <= col, 1.0, 0.0).astype(F32)
    hp = lax.Precision.HIGHEST
    gc_cols = jnp.dot(tri_incl, g_all, precision=hp, preferred_element_type=F32)
    gc_rows = jnp.dot(gt_ref[0], tri_upper, precision=hp, preferred_element_type=F32)
    eye = jnp.where(row == col, 1.0, 0.0).astype(F32)
    right_half = lax.broadcasted_iota(I32, (C, 2 * C), 1) >= C
    gn = gn_ref[...]

    for h0 in range(0, n_heads, HEAD_GROUP):
        heads = range(h0, min(h0 + HEAD_GROUP, n_heads))
        st = {h: {} for h in heads}

        qs = [act(h * HEAD_DIM) for h in heads]
        ks = [act(gw + h * HEAD_DIM) for h in heads]
        ssq = _lane_sums([x * x for x in qs + ks])

        for n, h in enumerate(heads):
            d = st[h]
            q = qs[n] * lax.rsqrt(ssq[n] + RMS_EPS) * (HEAD_DIM ** -0.5)
            k = ks[n] * lax.rsqrt(ssq[len(qs) + n] + RMS_EPS)
            beta = beta_all[:, h:h + 1]
            gcol = gc_cols[:, h:h + 1]
            grow = gc_rows[h:h + 1, :]
            g_last = gc_cols[C - 1:C, h:h + 1]
            d["decay"] = jnp.where(incl, jnp.exp(jnp.where(incl, gcol - grow, 0.0)), 0.0)
            egc = jnp.exp(gcol)
            kb = k * beta
            d["a1"] = _dot_nt(jnp.concatenate([kb, q], axis=0).astype(BF16), k.astype(BF16))
            v = act(2 * gw + h * HEAD_DIM)
            d["rhs"] = jnp.concatenate([v * beta, kb * egc], axis=1).astype(BF16)
            d["q_dec"] = q * egc
            d["k_dec"] = (k * jnp.exp(g_last - gcol)).astype(BF16)
            d["s_scale"] = jnp.exp(g_last)

        for h in heads:
            d = st[h]
            lmat = jnp.where(strict, d["a1"][0:C] * d["decay"], 0.0)
            d["attn"] = (d["a1"][C:2 * C] * d["decay"]).astype(BF16)
            d["wmat"] = jnp.concatenate([-lmat, eye], axis=1)

        for _ in range(6):
            for h in heads:
                d = st[h]
                wmat = d["wmat"]
                r = _dot(wmat[:, 0:C].astype(BF16), wmat.astype(BF16))
                d["wmat"] = r + jnp.where(right_half, wmat, 0.0)

        for h in heads:
            d = st[h]
            tinv = jnp.where(right_half, d["wmat"], 0.0).astype(BF16)
            d["sol"] = _dot(tinv, jnp.concatenate([d["rhs"], d["rhs"]], axis=0))

        for h in heads:
            d = st[h]
            w = d["sol"][:, HEAD_DIM:2 * HEAD_DIM]
            d["ws_qs"] = _dot(jnp.concatenate([w, d["q_dec"]], axis=0).astype(BF16), s_ref[h].astype(BF16))

        for h in heads:
            d = st[h]
            vnb = (d["sol"][:, 0:HEAD_DIM] - d["ws_qs"][0:C]).astype(BF16)
            d["o"] = d["ws_qs"][C:2 * C] + _dot(d["attn"], vnb)
            s_ref[h] = s_ref[h] * d["s_scale"] + _dot_tn(d["k_dec"], vnb)

        osq = _lane_sums([st[h]["o"] * st[h]["o"] for h in heads])
        for n, h in enumerate(heads):
            o = st[h]["o"]
            sl = slice(h * HEAD_DIM, (h + 1) * HEAD_DIM)
            o = o * lax.rsqrt(osq[n] * (1.0 / HEAD_DIM) + RMS_EPS) * gn
            z = jnp.where(c == 0, zm_ref[:, sl], zx_ref[:, sl])
            o_ref[:, sl] = (o * _silu(z)).astype(o_ref.dtype)


def _gdn(qkv, qkv_meta, z, z_meta, bg_all, gt3, gn_row, n_heads):
    t = qkv.shape[0]
    gw = n_heads * HEAD_DIM
    n_chunks = t // CHUNK + 1
    xmap = lambda c: (jnp.maximum(c - 1, 0), 0)
    fixed = lambda c: (0, 0)
    return pl.pallas_call(
        functools.partial(_gdn_kernel, n_heads=n_heads),
        grid=(n_chunks,),
        in_specs=[pl.BlockSpec((CHUNK, 3 * gw), xmap),
                  pl.BlockSpec((CHUNK, 3 * gw), fixed),
                  pl.BlockSpec((CHUNK, gw), xmap),
                  pl.BlockSpec((CHUNK, gw), fixed),
                  pl.BlockSpec((CHUNK, 2 * n_heads), lambda c: (c, 0)),
                  pl.BlockSpec((1, n_heads, CHUNK), lambda c: (c, 0, 0)),
                  pl.BlockSpec((1, HEAD_DIM), fixed)],
        out_specs=pl.BlockSpec((CHUNK, gw), xmap),
        out_shape=jax.ShapeDtypeStruct((t, gw), BF16),
        scratch_shapes=[pltpu.VMEM((n_heads, HEAD_DIM, HEAD_DIM), F32)],
        compiler_params=_params(1),
        name="gated_delta_rule",
    )(qkv, qkv_meta, z, z_meta, bg_all, gt3, gn_row)


def _router_kernel(r_ref, g_ref, b_ref, rwh_ref, rwl_ref, bias_ref, h1_ref, h1b_ref, h1p_ref,
                   eidx_ref, wsel_ref, pos_ref, cnt_ref, carry_ref, *, tm, n_exp):
    i = pl.program_id(0)
    per = n_exp // N_EXPERT_GROUPS

    @pl.when(i == 0)
    def _():
        carry_ref[...] = jnp.zeros(carry_ref.shape, F32)

    h1 = _ln_rows(r_ref[...], g_ref[...], b_ref[...])
    h1_ref[...] = h1
    hb = h1.astype(BF16)
    h1b_ref[...] = hb
    h1p_ref[...] = _pack_rows(h1)

    hl = (h1 - hb.astype(F32)).astype(BF16)
    logits = _dot_nt(rwh_ref[...], hb) + (_dot_nt(rwl_ref[...], hb) + _dot_nt(rwh_ref[...], hl))
    scores = _sigmoid(logits)
    biased = scores + bias_ref[...]
    neg_inf = F32(-jnp.inf)

    gs_rows = []
    sub = lax.broadcasted_iota(I32, (per, tm), 0)
    for g in range(N_EXPERT_GROUPS):
        xg = biased[g * per:(g + 1) * per, :]
        m1 = jnp.max(xg, axis=0, keepdims=True)
        first = jnp.min(jnp.where(xg == m1, sub, per), axis=0, keepdims=True)
        m2 = jnp.max(jnp.where(sub == first, neg_inf, xg), axis=0, keepdims=True)
        gs_rows.append(m1 + m2)
    gs = jnp.concatenate(gs_rows, axis=0)
    gid = lax.broadcasted_iota(I32, (N_EXPERT_GROUPS, tm), 0)
    grank = jnp.zeros((N_EXPERT_GROUPS, tm), F32)
    for g in range(N_EXPERT_GROUPS):
        rowv = gs[g:g + 1, :]
        grank = grank + jnp.where(rowv > gs, 1.0, jnp.where((rowv == gs) & (gid > g), 1.0, 0.0))
    gsel = jnp.where(grank < TOPK_GROUPS, 1.0, 0.0)
    emask = jnp.concatenate(
        [jnp.broadcast_to(gsel[g:g + 1, :], (per, tm)) for g in range(N_EXPERT_GROUPS)], axis=0)
    masked = jnp.where(emask > 0.0, biased, neg_inf)
    eid = lax.broadcasted_iota(I32, (n_exp, tm), 0)
    erank = jnp.zeros((n_exp, tm), F32)
    for e in range(n_exp):
        rowv = masked[e:e + 1, :]
        erank = erank + jnp.where(rowv > masked, 1.0, jnp.where((rowv == masked) & (eid > e), 1.0, 0.0))
    sel = jnp.where(erank < TOP_K, emask, 0.0)
    wdense = scores * sel
    gate = wdense / jnp.sum(wdense, axis=0, keepdims=True) * ROUTED_SCALE

    er = lax.broadcasted_iota(I32, (n_exp, n_exp), 0)
    ec = lax.broadcasted_iota(I32, (n_exp, n_exp), 1)
    lower = jnp.where(er > ec, 1.0, 0.0).astype(BF16)
    selb = sel.astype(BF16)
    slot = _dot(lower, selb)
    tr = lax.broadcasted_iota(I32, (tm, tm), 0)
    tc = lax.broadcasted_iota(I32, (tm, tm), 1)
    upper = jnp.where(tr < tc, 1.0, 0.0).astype(BF16)
    carry = carry_ref[:, 0:1]
    pos = _dot(selb, upper) + carry
    eid_f = eid.astype(F32)
    e_rows, w_rows, p_rows = [], [], []
    for k in range(TOP_K):
        mk = jnp.where(slot == float(k), sel, 0.0)
        e_rows.append(jnp.sum(mk * eid_f, axis=0, keepdims=True))
        w_rows.append(jnp.sum(mk * gate, axis=0, keepdims=True))
        p_rows.append(jnp.sum(mk * pos, axis=0, keepdims=True))
    eidx_ref[...] = jnp.concatenate(e_rows, axis=0).astype(I32)
    wsel_ref[...] = jnp.concatenate(w_rows, axis=0)
    pos_ref[...] = jnp.concatenate(p_rows, axis=0).astype(I32)
    new_carry = carry + jnp.sum(sel, axis=1, keepdims=True)
    carry_ref[...] = jnp.broadcast_to(new_carry, carry_ref.shape)
    cnt_ref[...] = jnp.broadcast_to(new_carry, cnt_ref.shape).astype(I32)


def _router(r, g, b, rw_hi, rw_lo, bias_col, tm):
    t, d = r.shape
    n_exp = rw_hi.shape[0]
    row = lambda i: (i, 0)
    colb = lambda i: (0, i)
    fixed = lambda i: (0, 0)
    return pl.pallas_call(
        functools.partial(_router_kernel, tm=tm, n_exp=n_exp),
        grid=(t // tm,),
        in_specs=[pl.BlockSpec((tm, d), row),
                  pl.BlockSpec((1, d), fixed),
                  pl.BlockSpec((1, d), fixed),
                  pl.BlockSpec((n_exp, d), fixed),
                  pl.BlockSpec((n_exp, d), fixed),
                  pl.BlockSpec((n_exp, 1), fixed)],
        out_specs=[pl.BlockSpec((tm, d), row),
                   pl.BlockSpec((tm, d), row),
                   pl.BlockSpec((tm, d // 2), row),
                   pl.BlockSpec((TOP_K, tm), colb),
                   pl.BlockSpec((TOP_K, tm), colb),
                   pl.BlockSpec((TOP_K, tm), colb),
                   pl.BlockSpec((n_exp, 128), fixed)],
        out_shape=[jax.ShapeDtypeStruct((t, d), F32),
                   jax.ShapeDtypeStruct((t, d), BF16),
                   jax.ShapeDtypeStruct((t, d // 2), U32),
                   jax.ShapeDtypeStruct((TOP_K, t), I32),
                   jax.ShapeDtypeStruct((TOP_K, t), F32),
                   jax.ShapeDtypeStruct((TOP_K, t), I32),
                   jax.ShapeDtypeStruct((n_exp, 128), I32)],
        scratch_shapes=[pltpu.VMEM((n_exp, 128), F32)],
        compiler_params=_params(1),
        name="ln1_router",
    )(r, g, b, rw_hi, rw_lo, bias_col)


def _dest_kernel(gstart_ref, eidx_ref, pos_ref, dest_ref, *, n_exp):
    eidx = eidx_ref[...]
    acc = pos_ref[...]
    for e in range(n_exp):
        acc = acc + jnp.where(eidx == e, gstart_ref[e], 0)
    dest_ref[...] = acc


def _dest_rows(gstart, eidx, pos):
    n_exp = gstart.shape[0]
    return pl.pallas_call(
        functools.partial(_dest_kernel, n_exp=n_exp),
        grid_spec=pltpu.PrefetchScalarGridSpec(
            num_scalar_prefetch=1,
            grid=(1,),
            in_specs=[pl.BlockSpec(eidx.shape, lambda i, gs: (0, 0)),
                      pl.BlockSpec(pos.shape, lambda i, gs: (0, 0))],
            out_specs=pl.BlockSpec(eidx.shape, lambda i, gs: (0, 0))),
        out_shape=jax.ShapeDtypeStruct(eidx.shape, I32),
        compiler_params=_params(1),
        name="dest_rows",
    )(gstart, eidx, pos)


def _dispatch_kernel(dest_ref, hp_ref, xs_ref, sem, *, tm):
    def issue(t, carry):
        for k in range(TOP_K):
            d = dest_ref[0, 0, t * TOP_K + k]
            pltpu.make_async_copy(hp_ref.at[pl.ds(t, 1)], xs_ref.at[pl.ds(d, 1)], sem).start()
        return carry
    lax.fori_loop(0, tm, issue, 0)
    pltpu.make_async_copy(xs_ref.at[pl.ds(0, tm * TOP_K)], xs_ref.at[pl.ds(0, tm * TOP_K)], sem).wait()


def _dispatch(dest_tiles, h1p, tm):
    t, dw = h1p.shape
    return pl.pallas_call(
        functools.partial(_dispatch_kernel, tm=tm),
        grid=(t // tm,),
        in_specs=[pl.BlockSpec((1, 1, tm * TOP_K), lambda i: (i, 0, 0), memory_space=pltpu.SMEM),
                  pl.BlockSpec((tm, dw), lambda i: (i, 0))],
        out_specs=pl.BlockSpec(memory_space=pl.ANY),
        out_shape=jax.ShapeDtypeStruct((t * TOP_K, dw), U32),
        scratch_shapes=[pltpu.SemaphoreType.DMA(())],
        compiler_params=_params(1),
        name="dispatch_rows",
    )(dest_tiles, h1p)


def _pack_rows(y):
    n = y.shape[1] // 2
    yb = y.astype(BF16)
    hi = lax.bitcast_convert_type(yb[:, 0:n].astype(F32), U32)
    lo = lax.bitcast_convert_type(yb[:, n:].astype(F32), U32)
    return hi | (lo >> 16)


def _unpack_rows_f32(xu):
    hi = lax.bitcast_convert_type(xu & jnp.uint32(0xFFFF0000), F32)
    lo = lax.bitcast_convert_type(xu << 16, F32)
    return hi, lo


def _unpack_rows(xu):
    hi, lo = _unpack_rows_f32(xu)
    return hi.astype(BF16), lo.astype(BF16)


def _expert_weights(v, ve_ref, vfe_ref, eo_ref, vne_ref, hbm_refs, f32_refs, bf16_refs, sems):
    @pl.when(vfe_ref[v] == 1)
    def _():
        slot = eo_ref[v] % 2
        e = ve_ref[v]

        def copies(expert, s):
            return [pltpu.make_async_copy(h.at[expert], f.at[s], sem.at[s])
                    for h, f, sem in zip(hbm_refs, f32_refs, sems)]

        @pl.when(v == 0)
        def _():
            for cp in copies(e, slot):
                cp.start()

        for cp in copies(e, slot):
            cp.wait()
        ne = vne_ref[v]

        @pl.when(ne >= 0)
        def _():
            for cp in copies(ne, 1 - slot):
                cp.start()

        for f, bf in zip(f32_refs, bf16_refs):
            bf[...] = f[slot].astype(BF16)


def _visit_subblocks(vt_ref, ve_ref, vft_ref, gs_ref, ge_ref, v, tm, sb, o_ref, compute):
    e = ve_ref[v]
    gs, ge = gs_ref[e], ge_ref[e]
    first = vft_ref[v] == 1
    width = o_ref.shape[1]
    t0 = vt_ref[v] * tm
    whole = (t0 >= gs) & (t0 + tm <= ge)

    @pl.when(whole)
    def _():
        o_ref[...] = compute(0, tm).astype(o_ref.dtype)

    for s in range(tm // sb):
        r0 = t0 + s * sb
        hit = (r0 < ge) & (r0 + sb > gs)
        rows = pl.ds(s * sb, sb)

        @pl.when(hit & jnp.logical_not(whole))
        def _():
            y = compute(s * sb, sb).astype(o_ref.dtype)
            owned = (r0 >= gs) & (r0 + sb <= ge)

            @pl.when(owned)
            def _():
                o_ref[rows, :] = y

            @pl.when(jnp.logical_not(owned))
            def _():
                ridx = r0 + lax.broadcasted_iota(I32, (sb, width), 0)
                mask = (ridx >= gs) & (ridx < ge)

                @pl.when(first)
                def _():
                    o_ref[rows, :] = jnp.where(mask, y, jnp.zeros_like(y))

                @pl.when(jnp.logical_not(first))
                def _():
                    o_ref[rows, :] = jnp.where(mask, y, o_ref[rows, :])

        @pl.when(jnp.logical_not(hit) & first)
        def _():
            o_ref[rows, :] = jnp.zeros((sb, width), o_ref.dtype)


def _gmm_up_kernel(vt_ref, ve_ref, vft_ref, vfe_ref, eo_ref, vne_ref, gs_ref, ge_ref, nv_ref,
                   xs_ref, wg_hbm, wu_hbm, o_ref, wgf_ref, wuf_ref, wgb_ref, wub_ref, semg, semu, *, tm, sb):
    v = pl.program_id(0)

    @pl.when(v < nv_ref[0])
    def _():
        _expert_weights(v, ve_ref, vfe_ref, eo_ref, vne_ref, (wg_hbm, wu_hbm), (wgf_ref, wuf_ref),
                        (wgb_ref, wub_ref), (semg, semu))

        def compute(start, size):
            hi, lo = _unpack_rows(xs_ref[start:start + size, :])
            kh = hi.shape[1]
            hg = _dot(hi, wgb_ref[0:kh, :]) + _dot(lo, wgb_ref[kh:, :])
            hu = _dot(hi, wub_ref[0:kh, :]) + _dot(lo, wub_ref[kh:, :])
            return _silu(hg) * hu

        _visit_subblocks(vt_ref, ve_ref, vft_ref, gs_ref, ge_ref, v, tm, sb, o_ref, compute)


def _gmm_down_kernel(vt_ref, ve_ref, vft_ref, vfe_ref, eo_ref, vne_ref, gs_ref, ge_ref, nv_ref,
                     a_ref, wd_hbm, o_ref, wdf_ref, wdb_ref, semd, *, tm, sb):
    v = pl.program_id(0)

    @pl.when(v < nv_ref[0])
    def _():
        _expert_weights(v, ve_ref, vfe_ref, eo_ref, vne_ref, (wd_hbm,), (wdf_ref,), (wdb_ref,), (semd,))

        def compute(start, size):
            return _pack_rows(_dot(a_ref[start:start + size, :], wdb_ref[...]))

        _visit_subblocks(vt_ref, ve_ref, vft_ref, gs_ref, ge_ref, v, tm, sb, o_ref, compute)


def _visit_schedule(counts, n_rows, tm):
    n_exp = counts.shape[0]
    n_tiles = n_rows // tm
    n_visits = n_tiles + n_exp - 1
    ends = jnp.cumsum(counts)
    starts = ends - counts
    first_tile = starts // tm
    last_tile = jnp.maximum(ends - 1, 0) // tm
    tiles_e = jnp.where(counts > 0, last_tile - first_tile + 1, 0)
    vend = jnp.cumsum(tiles_e)
    vstart = vend - tiles_e
    total = vend[-1]
    v = jnp.arange(n_visits, dtype=I32)
    vc = jnp.minimum(v, total - 1)
    ve = jnp.sum(vend[None, :] <= vc[:, None], axis=1).astype(I32)
    vt = (vc - vstart[ve] + first_tile[ve]).astype(I32)
    prev_t = jnp.concatenate([jnp.full((1,), -1, I32), vt[:-1]])
    prev_e = jnp.concatenate([jnp.full((1,), -1, I32), ve[:-1]])
    vft = (vt != prev_t).astype(I32)
    vfe = (ve != prev_e).astype(I32)
    eo = (jnp.cumsum(vfe) - 1).astype(I32)
    eidx = jnp.arange(n_exp, dtype=I32)
    later = lax.cummin(jnp.where(counts > 0, eidx, n_exp), axis=0, reverse=True)
    nxt = jnp.concatenate([later[1:], jnp.full((1,), n_exp, I32)])
    vne = jnp.where(nxt < n_exp, nxt, -1)[ve].astype(I32)
    sched = (vt, ve, vft, vfe, eo, vne, starts.astype(I32), ends.astype(I32), total.reshape(1).astype(I32))
    return sched, n_visits


def _gmm_up(sched, n_visits, xs, wg, wu, tm, sb):
    n, dw = xs.shape
    n_exp, d, f = wg.shape
    return pl.pallas_call(
        functools.partial(_gmm_up_kernel, tm=tm, sb=sb),
        grid_spec=pltpu.PrefetchScalarGridSpec(
            num_scalar_prefetch=len(sched),
            grid=(n_visits,),
            in_specs=[pl.BlockSpec((tm, dw), lambda v, vt, *_: (vt[v], 0)),
                      pl.BlockSpec(memory_space=pl.ANY),
                      pl.BlockSpec(memory_space=pl.ANY)],
            out_specs=pl.BlockSpec((tm, f), lambda v, vt, *_: (vt[v], 0)),
            scratch_shapes=[pltpu.VMEM((2, d, f), F32), pltpu.VMEM((2, d, f), F32),
                            pltpu.VMEM((d, f), BF16), pltpu.VMEM((d, f), BF16),
                            pltpu.SemaphoreType.DMA((2,)), pltpu.SemaphoreType.DMA((2,))]),
        out_shape=jax.ShapeDtypeStruct((n, f), BF16),
        compiler_params=_params(1),
        name="experts_gate_up",
    )(*sched, xs, wg, wu)


def _gmm_down(sched, n_visits, act, wd, tm, sb):
    n, f = act.shape
    n_exp, _, d = wd.shape
    return pl.pallas_call(
        functools.partial(_gmm_down_kernel, tm=tm, sb=sb),
        grid_spec=pltpu.PrefetchScalarGridSpec(
            num_scalar_prefetch=len(sched),
            grid=(n_visits,),
            in_specs=[pl.BlockSpec((tm, f), lambda v, vt, *_: (vt[v], 0)),
                      pl.BlockSpec(memory_space=pl.ANY)],
            out_specs=pl.BlockSpec((tm, d // 2), lambda v, vt, *_: (vt[v], 0)),
            scratch_shapes=[pltpu.VMEM((2, f, d), F32), pltpu.VMEM((f, d), BF16),
                            pltpu.SemaphoreType.DMA((2,))]),
        out_shape=jax.ShapeDtypeStruct((n, d // 2), U32),
        compiler_params=_params(1),
        name="experts_down",
    )(*sched, act, wd)


def _combine_kernel(dcur_ref, dnxt_ref, w_ref, h1_ref, as_ref, wsd_ref, g_ref, b_ref, ys_ref, o_ref,
                    gbuf_ref, ybuf_ref, sem, *, tm, alpha):
    i = pl.program_id(0)
    n = pl.num_programs(0)
    slot = i % 2
    groups = tm // SUBLANES
    d = o_ref.shape[1]
    half = d // 2

    def issue_token(dref, dst_slot, t):
        for k in range(TOP_K):
            src = dref[0, 0, t * TOP_K + k]
            pltpu.make_async_copy(ys_ref.at[pl.ds(src, 1)], gbuf_ref.at[dst_slot, k, pl.ds(t, 1)],
                                  sem.at[dst_slot]).start()

    def wait_slot(s):
        for k in range(TOP_K):
            pltpu.make_async_copy(ys_ref.at[pl.ds(0, tm)], gbuf_ref.at[s, k], sem.at[s]).wait()

    @pl.when(i == 0)
    def _():
        def first(t, carry):
            issue_token(dcur_ref, 0, t)
            return carry
        lax.fori_loop(0, tm, first, 0)

    ybuf_ref[...] = alpha * h1_ref[...] + _dot(as_ref[...], wsd_ref[...])
    wait_slot(slot)

    def reduce_group(r, carry):
        r8 = pl.multiple_of(r * SUBLANES, SUBLANES)
        rows = pl.ds(r8, SUBLANES)
        y_hi = ybuf_ref[rows, 0:half]
        y_lo = ybuf_ref[rows, half:]
        w = w_ref[rows, :]
        for j in range(SUBLANES):
            issue_token(dnxt_ref, 1 - slot, r8 + j)
            for k in range(j * TOP_K // SUBLANES, (j + 1) * TOP_K // SUBLANES):
                e_hi, e_lo = _unpack_rows_f32(gbuf_ref[slot, k, rows, :])
                wk = w[:, k:k + 1]
                y_hi = y_hi + wk * e_hi
                y_lo = y_lo + wk * e_lo
        ybuf_ref[rows, 0:half] = y_hi
        ybuf_ref[rows, half:] = y_lo
        return carry
    lax.fori_loop(0, groups, reduce_group, 0)
    o_ref[...] = _ln_rows(ybuf_ref[...], g_ref[...], b_ref[...])

    @pl.when(i == n - 1)
    def _():
        wait_slot(1 - slot)


def _combine(dest_tiles, w_tok, h1, act_s, wsd, g, b, ys, tm, alpha):
    t, d = h1.shape
    f = act_s.shape[1]
    n = t // tm
    row = lambda i: (i, 0)
    fixed = lambda i: (0, 0)
    return pl.pallas_call(
        functools.partial(_combine_kernel, tm=tm, alpha=alpha),
        grid=(n,),
        in_specs=[pl.BlockSpec((1, 1, tm * TOP_K), lambda i: (i, 0, 0), memory_space=pltpu.SMEM),
                  pl.BlockSpec((1, 1, tm * TOP_K), lambda i: (jnp.minimum(i + 1, n - 1), 0, 0),
                               memory_space=pltpu.SMEM),
                  pl.BlockSpec((tm, TOP_K), row),
                  pl.BlockSpec((tm, d), row),
                  pl.BlockSpec((tm, f), row),
                  pl.BlockSpec((f, d), fixed),
                  pl.BlockSpec((1, d), fixed),
                  pl.BlockSpec((1, d), fixed),
                  pl.BlockSpec(memory_space=pl.ANY)],
        out_specs=pl.BlockSpec((tm, d), row),
        out_shape=jax.ShapeDtypeStruct((t, d), F32),
        scratch_shapes=[pltpu.VMEM((2, TOP_K, tm, d // 2), U32),
                        pltpu.VMEM((tm, d), F32),
                        pltpu.SemaphoreType.DMA((2,))],
        compiler_params=_params(1),
        name="combine_ln2",
    )(dest_tiles, dest_tiles, w_tok, h1, act_s, wsd, g, b, ys)


def _tile(n, pref):
    t = min(n, pref)
    assert n % t == 0, (n, pref)
    return t


def kernel(x, meta_tokens, emb_ln_g, emb_ln_b, w_in, conv_w, conv_b, conv_ln_g, conv_ln_b, short_conv_w, a_log, dt_bias, gdn_norm_g, w_out, ln1_g, ln1_b, router_w, router_bias, expert_w_gate, expert_w_up, expert_w_down, shared_w_gate, shared_w_up, shared_w_down, ln2_g, ln2_b):
    depth = w_in.shape[0]
    assert depth == 1 and x.shape[0] == 1, "single layer, single sequence"
    t, d = x.shape[1], x.shape[2]
    cw = conv_w.shape[2]
    n_heads = a_log.shape[1]
    gw = n_heads * HEAD_DIM
    alpha = (2.0 * depth) ** 0.25
    assert meta_tokens.shape[0] == N_META and t % CHUNK == 0 and w_in.shape[2] == 2 * cw + 4 * gw + 2 * n_heads
    row = lambda a: a.reshape(1, -1)

    h_b, h_stats = _embed_ln(x[0], row(emb_ln_g), row(emb_ln_b), _tile(t, 256), "embed_ln")
    hm_b, _ = _embed_ln(meta_tokens, row(emb_ln_g), row(emb_ln_b), N_META, "embed_ln_meta")

    w_t = jnp.transpose(w_in[0])
    tm = _tile(t, 1024)
    zeros_h = jnp.zeros((n_heads,), F32)
    alog_row = row(jnp.concatenate([zeros_h, a_log[0]]))
    dtb_row = row(jnp.concatenate([zeros_h, dt_bias[0]]))
    c_x, c_m = _inproj_glu(h_b, hm_b, w_t, 0, cw, cw, tm, _tile(cw, 256))
    qkv_x, qkv_m = _inproj_conv(h_b, hm_b, w_t, short_conv_w[0], 2 * cw, 3 * gw, tm, _tile(3 * gw, 512))
    z_x, z_m = _inproj_plain(h_b, hm_b, w_t, 2 * cw + 3 * gw, gw, tm, _tile(gw, 512))
    bg_x, bg_m = _inproj_decay(h_b, hm_b, w_t[2 * cw + 4 * gw:], alog_row, dtb_row, tm, n_heads)

    y_conv = _conv_module(c_x, c_m, conv_w[0], row(conv_b[0]), row(conv_ln_g[0]), row(conv_ln_b[0]),
                          _tile(t, 256))

    front = lambda a: jnp.pad(a, ((CHUNK - N_META, 0), (0, 0)))
    bg_all = jnp.concatenate([front(bg_m), bg_x], axis=0)
    gt3 = bg_all[:, n_heads:].reshape(-1, CHUNK, n_heads).transpose(0, 2, 1)
    y_gdn = _gdn(qkv_x, front(qkv_m), z_x, front(z_m), bg_all, gt3, row(gdn_norm_g[0]), n_heads)

    r1 = _proj_residual(y_conv, y_gdn, w_out[0], x[0], h_stats, row(emb_ln_g), row(emb_ln_b), alpha, tm,
                        _tile(d, 512))
    rwt = router_w[0].T
    rw_hi = rwt.astype(BF16)
    rw_lo = (rwt - rw_hi.astype(F32)).astype(BF16)
    h1, h1_b, h1_p, eidx, wsel, pos, cnt = _router(
        r1, row(ln1_g[0]), row(ln1_b[0]), rw_hi, rw_lo, router_bias[0].reshape(-1, 1), _tile(t, 256))
    counts = cnt[:, 0]
    gstart = jnp.cumsum(counts) - counts
    dest = _dest_rows(gstart.astype(I32), eidx, pos)

    tmd = _tile(t, 256)
    xs = _dispatch(dest.T.reshape(t // tmd, 1, tmd * TOP_K), h1_p, tmd)
    tmg = _tile(t * TOP_K, 512)
    sbg = _tile(tmg, 128)
    sched, n_visits = _visit_schedule(counts, t * TOP_K, tmg)
    act = _gmm_up(sched, n_visits, xs, expert_w_gate[0], expert_w_up[0], tmg, sbg)
    ys = _gmm_down(sched, n_visits, act, expert_w_down[0], tmg, _tile(tmg, 256))

    act_s = _swiglu(h1_b, shared_w_gate[0], shared_w_up[0], tm, _tile(shared_w_gate.shape[2], 256), BF16)
    tmc = _tile(t, 128)
    out = _combine(dest.T.reshape(t // tmc, 1, tmc * TOP_K), wsel.T, h1, act_s, shared_w_down[0].astype(BF16),
                   row(ln2_g[0]), row(ln2_b[0]), ys, tmc, alpha)
    return out[None]
```

```python
import functools

import jax
import jax.numpy as jnp
from jax import lax
from jax.experimental import pallas as pl
from jax.experimental.pallas import tpu as pltpu

F32 = jnp.float32
BF16 = jnp.bfloat16
I32 = jnp.int32
U32 = jnp.uint32

N_META = 16
CHUNK = 64
HEAD_DIM = 128
HEAD_GROUP = 16
TOP_K = 8
N_EXPERT_GROUPS = 8
TOPK_GROUPS = 4
ROUTED_SCALE = 2.5
LN_EPS = 1e-5
RMS_EPS = 1e-6
CONV_HALO = 32
CONV_ROWS = 32
CONV_LANES = 512
SUBLANES = 8
LANES = 128
VMEM_LIMIT = 56 * 1024 * 1024


def _params(n_axes, vmem=VMEM_LIMIT):
    return pltpu.CompilerParams(dimension_semantics=("arbitrary",) * n_axes, vmem_limit_bytes=vmem)


def _ln_rows(x, g, b):
    mu = jnp.mean(x, -1, keepdims=True)
    xc = x - mu
    var = jnp.mean(xc * xc, -1, keepdims=True)
    return xc * lax.rsqrt(var + LN_EPS) * g + b


def _sigmoid(x):
    return 1.0 / (1.0 + jnp.exp(-x))


def _silu(x):
    return x * _sigmoid(x)


def _dot(a, b):
    return jnp.dot(a, b, preferred_element_type=F32)


def _dot_nt(a, b, precision=None):
    return lax.dot_general(a, b, (((1,), (1,)), ((), ())), precision=precision, preferred_element_type=F32)


def _dot_tn(a, b):
    return lax.dot_general(a, b, (((0,), (0,)), ((), ())), preferred_element_type=F32)


def _embed_ln_kernel(x_ref, g_ref, b_ref, hb_ref, st_ref):
    x = x_ref[...]
    mu = jnp.mean(x, -1, keepdims=True)
    xc = x - mu
    rs = lax.rsqrt(jnp.mean(xc * xc, -1, keepdims=True) + LN_EPS)
    hb_ref[...] = (xc * rs * g_ref[...] + b_ref[...]).astype(BF16)
    lane = lax.broadcasted_iota(I32, st_ref.shape, 1)
    st_ref[...] = jnp.where(lane == 0, mu, jnp.where(lane == 1, rs, 0.0))


def _embed_ln(x, g, b, tm, name):
    m, d = x.shape
    return pl.pallas_call(
        _embed_ln_kernel,
        grid=(m // tm,),
        in_specs=[pl.BlockSpec((tm, d), lambda i: (i, 0)),
                  pl.BlockSpec((1, d), lambda i: (0, 0)),
                  pl.BlockSpec((1, d), lambda i: (0, 0))],
        out_specs=[pl.BlockSpec((tm, d), lambda i: (i, 0)),
                   pl.BlockSpec((tm, LANES), lambda i: (i, 0))],
        out_shape=[jax.ShapeDtypeStruct((m, d), BF16), jax.ShapeDtypeStruct((m, LANES), F32)],
        compiler_params=_params(1),
        name=name,
    )(x, g, b)


def _inproj_plain_kernel(a_ref, am_ref, w_ref, o_ref, om_ref, wb_ref):
    @pl.when(pl.program_id(1) == 0)
    def _():
        wb_ref[...] = w_ref[...].astype(BF16)
        om_ref[...] = _dot_nt(am_ref[...], wb_ref[...])
    o_ref[...] = _dot_nt(a_ref[...], wb_ref[...])


def _inproj_conv_kernel(a_ref, am_ref, w_ref, cw_ref, o_ref, om_ref, wb_ref, hbuf_ref):
    i = pl.program_id(1)
    tm = a_ref.shape[0]
    kw = cw_ref.shape[0]

    def conv_silu(rows):
        acc = cw_ref[kw - 1:kw, :] * hbuf_ref[SUBLANES:SUBLANES + rows, :]
        for j in range(kw - 1):
            off = SUBLANES - (kw - 1) + j
            acc = acc + cw_ref[j:j + 1, :] * hbuf_ref[off:off + rows, :]
        return _silu(acc)

    @pl.when(i == 0)
    def _():
        wb_ref[...] = w_ref[...].astype(BF16)
        hbuf_ref[0:SUBLANES, :] = jnp.zeros((SUBLANES, hbuf_ref.shape[1]), F32)
        hbuf_ref[SUBLANES:SUBLANES + N_META, :] = _dot_nt(am_ref[...], wb_ref[...])
        om_ref[...] = conv_silu(N_META)
        hbuf_ref[0:SUBLANES, :] = hbuf_ref[N_META:N_META + SUBLANES, :]

    @pl.when(i > 0)
    def _():
        hbuf_ref[0:SUBLANES, :] = hbuf_ref[tm:tm + SUBLANES, :]

    hbuf_ref[SUBLANES:SUBLANES + tm, :] = _dot_nt(a_ref[...], wb_ref[...])
    o_ref[...] = conv_silu(tm)


def _inproj_glu_kernel(a_ref, am_ref, w1_ref, w2_ref, o_ref, om_ref, w1b_ref, w2b_ref):
    @pl.when(pl.program_id(1) == 0)
    def _():
        w1b_ref[...] = w1_ref[...].astype(BF16)
        w2b_ref[...] = w2_ref[...].astype(BF16)
        am = am_ref[...]
        om_ref[...] = _dot_nt(am, w1b_ref[...]) * _sigmoid(_dot_nt(am, w2b_ref[...]))
    a = a_ref[...]
    o_ref[...] = _dot_nt(a, w1b_ref[...]) * _sigmoid(_dot_nt(a, w2b_ref[...]))


def _decay_epilogue(p, alog, dtb, n_heads):
    x = p + dtb
    softplus = jnp.maximum(x, 0.0) + jnp.log1p(jnp.exp(-jnp.abs(x)))
    g = -jnp.exp(alog) * softplus
    lane = lax.broadcasted_iota(I32, p.shape, 1)
    return jnp.where(lane < n_heads, _sigmoid(p), g)


def _inproj_decay_kernel(a_ref, am_ref, w_ref, alog_ref, dtb_ref, o_ref, om_ref, wb_ref, *, n_heads):
    @pl.when(pl.program_id(1) == 0)
    def _():
        wb_ref[...] = w_ref[...].astype(BF16)
        om_ref[...] = _decay_epilogue(_dot_nt(am_ref[...], wb_ref[...]), alog_ref[...], dtb_ref[...], n_heads)
    o_ref[...] = _decay_epilogue(_dot_nt(a_ref[...], wb_ref[...]), alog_ref[...], dtb_ref[...], n_heads)


def _swiglu_kernel(a_ref, w1_ref, w2_ref, o_ref, w1b_ref, w2b_ref):
    @pl.when(pl.program_id(1) == 0)
    def _():
        w1b_ref[...] = w1_ref[...].astype(BF16)
        w2b_ref[...] = w2_ref[...].astype(BF16)
    a = a_ref[...]
    o_ref[...] = (_silu(_dot(a, w1b_ref[...])) * _dot(a, w2b_ref[...])).astype(o_ref.dtype)


def _proj_residual_kernel(a1_ref, a2_ref, w_ref, x_ref, st_ref, g_ref, b_ref, o_ref, wb_ref, *, alpha, k1):
    @pl.when(pl.program_id(1) == 0)
    def _():
        wb_ref[...] = w_ref[...].astype(BF16)
    acc = _dot(a1_ref[...], wb_ref[0:k1, :]) + _dot(a2_ref[...], wb_ref[k1:, :])
    st = st_ref[...]
    resid = (x_ref[...] - st[:, 0:1]) * st[:, 1:2] * g_ref[...] + b_ref[...]
    o_ref[...] = alpha * resid + acc


def _inproj_plain(a, am, w_t, row0, n, tm, tn):
    m, k = a.shape
    mm = am.shape[0]
    rb = row0 // tn
    return pl.pallas_call(
        _inproj_plain_kernel,
        grid=(n // tn, m // tm),
        in_specs=[pl.BlockSpec((tm, k), lambda j, i: (i, 0)),
                  pl.BlockSpec((mm, k), lambda j, i: (0, 0)),
                  pl.BlockSpec((tn, k), lambda j, i: (j + rb, 0))],
        out_specs=[pl.BlockSpec((tm, tn), lambda j, i: (i, j)),
                   pl.BlockSpec((mm, tn), lambda j, i: (0, j))],
        out_shape=[jax.ShapeDtypeStruct((m, n), F32), jax.ShapeDtypeStruct((mm, n), F32)],
        scratch_shapes=[pltpu.VMEM((tn, k), BF16)],
        compiler_params=_params(2),
        name="inproj_z",
    )(a, am, w_t)


def _inproj_conv(a, am, w_t, conv_w, row0, n, tm, tn):
    m, k = a.shape
    mm = am.shape[0]
    kw = conv_w.shape[0]
    assert kw <= SUBLANES and mm == N_META and N_META % SUBLANES == 0
    rb = row0 // tn
    return pl.pallas_call(
        _inproj_conv_kernel,
        grid=(n // tn, m // tm),
        in_specs=[pl.BlockSpec((tm, k), lambda j, i: (i, 0)),
                  pl.BlockSpec((mm, k), lambda j, i: (0, 0)),
                  pl.BlockSpec((tn, k), lambda j, i: (j + rb, 0)),
                  pl.BlockSpec((kw, tn), lambda j, i: (0, j))],
        out_specs=[pl.BlockSpec((tm, tn), lambda j, i: (i, j)),
                   pl.BlockSpec((mm, tn), lambda j, i: (0, j))],
        out_shape=[jax.ShapeDtypeStruct((m, n), F32), jax.ShapeDtypeStruct((mm, n), F32)],
        scratch_shapes=[pltpu.VMEM((tn, k), BF16), pltpu.VMEM((SUBLANES + tm, tn), F32)],
        compiler_params=_params(2),
        name="inproj_qkv_conv",
    )(a, am, w_t, conv_w)


def _inproj_glu(a, am, w_t, row1, row2, n, tm, tn):
    m, k = a.shape
    mm = am.shape[0]
    r1, r2 = row1 // tn, row2 // tn
    return pl.pallas_call(
        _inproj_glu_kernel,
        grid=(n // tn, m // tm),
        in_specs=[pl.BlockSpec((tm, k), lambda j, i: (i, 0)),
                  pl.BlockSpec((mm, k), lambda j, i: (0, 0)),
                  pl.BlockSpec((tn, k), lambda j, i: (j + r1, 0)),
                  pl.BlockSpec((tn, k), lambda j, i: (j + r2, 0))],
        out_specs=[pl.BlockSpec((tm, tn), lambda j, i: (i, j)),
                   pl.BlockSpec((mm, tn), lambda j, i: (0, j))],
        out_shape=[jax.ShapeDtypeStruct((m, n), F32), jax.ShapeDtypeStruct((mm, n), F32)],
        scratch_shapes=[pltpu.VMEM((tn, k), BF16), pltpu.VMEM((tn, k), BF16)],
        compiler_params=_params(2),
        name="inproj_glu",
    )(a, am, w_t, w_t)


def _inproj_decay(a, am, w_ba_t, alog_row, dtb_row, tm, n_heads):
    m, k = a.shape
    mm = am.shape[0]
    n = w_ba_t.shape[0]
    return pl.pallas_call(
        functools.partial(_inproj_decay_kernel, n_heads=n_heads),
        grid=(1, m // tm),
        in_specs=[pl.BlockSpec((tm, k), lambda j, i: (i, 0)),
                  pl.BlockSpec((mm, k), lambda j, i: (0, 0)),
                  pl.BlockSpec((n, k), lambda j, i: (0, 0)),
                  pl.BlockSpec((1, n), lambda j, i: (0, 0)),
                  pl.BlockSpec((1, n), lambda j, i: (0, 0))],
        out_specs=[pl.BlockSpec((tm, n), lambda j, i: (i, 0)),
                   pl.BlockSpec((mm, n), lambda j, i: (0, 0))],
        out_shape=[jax.ShapeDtypeStruct((m, n), F32), jax.ShapeDtypeStruct((mm, n), F32)],
        scratch_shapes=[pltpu.VMEM((n, k), BF16)],
        compiler_params=_params(2),
        name="inproj_decay",
    )(a, am, w_ba_t, alog_row, dtb_row)


def _swiglu(a, w1, w2, tm, tn, out_dtype):
    m, k = a.shape
    n = w1.shape[1]
    return pl.pallas_call(
        _swiglu_kernel,
        grid=(n // tn, m // tm),
        in_specs=[pl.BlockSpec((tm, k), lambda j, i: (i, 0)),
                  pl.BlockSpec((k, tn), lambda j, i: (0, j)),
                  pl.BlockSpec((k, tn), lambda j, i: (0, j))],
        out_specs=pl.BlockSpec((tm, tn), lambda j, i: (i, j)),
        out_shape=jax.ShapeDtypeStruct((m, n), out_dtype),
        scratch_shapes=[pltpu.VMEM((k, tn), BF16), pltpu.VMEM((k, tn), BF16)],
        compiler_params=_params(2),
        name="shared_swiglu",
    )(a, w1, w2)


def _proj_residual(a1, a2, w, x, stats, g, b, alpha, tm, tn):
    m, k1 = a1.shape
    k2 = a2.shape[1]
    n = w.shape[1]
    return pl.pallas_call(
        functools.partial(_proj_residual_kernel, alpha=alpha, k1=k1),
        grid=(n // tn, m // tm),
        in_specs=[pl.BlockSpec((tm, k1), lambda j, i: (i, 0)),
                  pl.BlockSpec((tm, k2), lambda j, i: (i, 0)),
                  pl.BlockSpec((k1 + k2, tn), lambda j, i: (0, j)),
                  pl.BlockSpec((tm, tn), lambda j, i: (i, j)),
                  pl.BlockSpec((tm, LANES), lambda j, i: (i, 0)),
                  pl.BlockSpec((1, tn), lambda j, i: (0, j)),
                  pl.BlockSpec((1, tn), lambda j, i: (0, j))],
        out_specs=pl.BlockSpec((tm, tn), lambda j, i: (i, j)),
        out_shape=jax.ShapeDtypeStruct((m, n), F32),
        scratch_shapes=[pltpu.VMEM((k1 + k2, tn), BF16)],
        compiler_params=_params(2),
        name="outproj_residual",
    )(a1, a2, w, x, stats, g, b)


def _conv_module_kernel(c_ref, cm_ref, w_ref, b_ref, g_ref, be_ref, o_ref, buf_ref, sh_ref, acc_ref, *, ts, kw):
    i = pl.program_id(0)
    ch = c_ref.shape[1]

    @pl.when(i == 0)
    def _():
        buf_ref[0:CONV_HALO - N_META, :] = jnp.zeros((CONV_HALO - N_META, ch), F32)
        buf_ref[CONV_HALO - N_META:CONV_HALO, :] = cm_ref[...]

    @pl.when(i > 0)
    def _():
        buf_ref[0:CONV_HALO, :] = buf_ref[ts:ts + CONV_HALO, :]

    buf_ref[CONV_HALO:CONV_HALO + ts, :] = c_ref[...]
    base = CONV_HALO - (kw - 1)
    span = ts + CONV_HALO - SUBLANES
    for r in range(1, SUBLANES):
        sh_ref[r - 1] = buf_ref[r:r + span, :]

    def row_block(rb, carry):
        r0 = pl.multiple_of(rb * CONV_ROWS, CONV_ROWS)
        for cb in range(ch // CONV_LANES):
            lanes = slice(cb * CONV_LANES, (cb + 1) * CONV_LANES)
            acc = jnp.zeros((CONV_ROWS, CONV_LANES), F32) + b_ref[:, lanes]
            for j in range(kw):
                off = base + j
                r, q = off % SUBLANES, off - off % SUBLANES
                if r == 0:
                    src = buf_ref[pl.ds(r0 + q, CONV_ROWS), lanes]
                else:
                    src = sh_ref[r - 1, pl.ds(r0 + q, CONV_ROWS), lanes]
                acc = acc + w_ref[j:j + 1, lanes] * src
            acc_ref[pl.ds(r0, CONV_ROWS), lanes] = acc
        return carry
    lax.fori_loop(0, ts // CONV_ROWS, row_block, 0)
    y = _ln_rows(acc_ref[...], g_ref[...], be_ref[...])
    o_ref[...] = _silu(y).astype(o_ref.dtype)


def _conv_module(c, c_meta, w, b, g, be, ts):
    t, ch = c.shape
    kw = w.shape[0]
    assert kw - 1 <= CONV_HALO and N_META <= CONV_HALO and ts >= CONV_HALO
    assert ts % CONV_ROWS == 0 and ch % CONV_LANES == 0
    return pl.pallas_call(
        functools.partial(_conv_module_kernel, ts=ts, kw=kw),
        grid=(t // ts,),
        in_specs=[pl.BlockSpec((ts, ch), lambda i: (i, 0)),
                  pl.BlockSpec((N_META, ch), lambda i: (0, 0)),
                  pl.BlockSpec((kw, ch), lambda i: (0, 0)),
                  pl.BlockSpec((1, ch), lambda i: (0, 0)),
                  pl.BlockSpec((1, ch), lambda i: (0, 0)),
                  pl.BlockSpec((1, ch), lambda i: (0, 0))],
        out_specs=pl.BlockSpec((ts, ch), lambda i: (i, 0)),
        out_shape=jax.ShapeDtypeStruct((t, ch), BF16),
        scratch_shapes=[pltpu.VMEM((CONV_HALO + ts, ch), F32),
                        pltpu.VMEM((SUBLANES - 1, ts + CONV_HALO - SUBLANES, ch), F32),
                        pltpu.VMEM((ts, ch), F32)],
        compiler_params=_params(1),
        name="conv_module",
    )(c, c_meta, w, b, g, be)


def _lane_sums(blocks):
    rows = blocks[0].shape[0]
    x = jnp.concatenate(blocks, axis=0)
    hi = x.astype(BF16)
    lo = (x - hi.astype(F32)).astype(BF16)
    ones = jnp.ones((HEAD_DIM, HEAD_DIM), BF16)
    s = _dot(hi, ones) + _dot(lo, ones)
    return [s[n * rows:(n + 1) * rows] for n in range(len(blocks))]


def _gdn_kernel(ax_ref, am_ref, zx_ref, zm_ref, bg_ref, gt_ref, gn_ref, o_ref, s_ref, *, n_heads):
    c = pl.program_id(0)
    gw = n_heads * HEAD_DIM
    C = CHUNK

    @pl.when(c == 0)
    def _():
        s_ref[...] = jnp.zeros(s_ref.shape, F32)

    def act(col0):
        cols = slice(col0, col0 + HEAD_DIM)
        return jnp.where(c == 0, am_ref[:, cols], ax_ref[:, cols])

    bg = bg_ref[...]
    beta_all = bg[:, 0:n_heads]
    g_all = bg[:, n_heads:2 * n_heads]
    row = lax.broadcasted_iota(I32, (C, C), 0)
    col = lax.broadcasted_iota(I32, (C, C), 1)
    incl = row >= col
    strict = row > col
    tri_incl = jnp.where(incl, 1.0, 0.0).astype(F32)
    tri_upper = jnp.where(row <= col, 1.0, 0.0).astype(F32)
    hp = lax.Precision.HIGHEST
    gc_cols = jnp.dot(tri_incl, g_all, precision=hp, preferred_element_type=F32)
    gc_rows = jnp.dot(gt_ref[0], tri_upper, precision=hp, preferred_element_type=F32)
    eye = jnp.where(row == col, 1.0, 0.0).astype(F32)
    right_half = lax.broadcasted_iota(I32, (C, 2 * C), 1) >= C
    gn = gn_ref[...]

    for h0 in range(0, n_heads, HEAD_GROUP):
        heads = range(h0, min(h0 + HEAD_GROUP, n_heads))
        st = {h: {} for h in heads}

        qs = [act(h * HEAD_DIM) for h in heads]
        ks = [act(gw + h * HEAD_DIM) for h in heads]
        ssq = _lane_sums([x * x for x in qs + ks])

        for n, h in enumerate(heads):
            d = st[h]
            q = qs[n] * lax.rsqrt(ssq[n] + RMS_EPS) * (HEAD_DIM ** -0.5)
            k = ks[n] * lax.rsqrt(ssq[len(qs) + n] + RMS_EPS)
            beta = beta_all[:, h:h + 1]
            gcol = gc_cols[:, h:h + 1]
            grow = gc_rows[h:h + 1, :]
            g_last = gc_cols[C - 1:C, h:h + 1]
            d["decay"] = jnp.where(incl, jnp.exp(jnp.where(incl, gcol - grow, 0.0)), 0.0)
            egc = jnp.exp(gcol)
            kb = k * beta
            d["a1"] = _dot_nt(jnp.concatenate([kb, q], axis=0).astype(BF16), k.astype(BF16))
            v = act(2 * gw + h * HEAD_DIM)
            d["rhs"] = jnp.concatenate([v * beta, kb * egc], axis=1).astype(BF16)
            d["q_dec"] = q * egc
            d["k_dec"] = (k * jnp.exp(g_last - gcol)).astype(BF16)
            d["s_scale"] = jnp.exp(g_last)

        for h in heads:
            d = st[h]
            lmat = jnp.where(strict, d["a1"][0:C] * d["decay"], 0.0)
            d["attn"] = (d["a1"][C:2 * C] * d["decay"]).astype(BF16)
            d["wmat"] = jnp.concatenate([-lmat, eye], axis=1)

        for _ in range(6):
            for h in heads:
                d = st[h]
                wmat = d["wmat"]
                r = _dot(wmat[:, 0:C].astype(BF16), wmat.astype(BF16))
                d["wmat"] = r + jnp.where(right_half, wmat, 0.0)

        for h in heads:
            d = st[h]
            tinv = jnp.where(right_half, d["wmat"], 0.0).astype(BF16)
            d["sol"] = _dot(tinv, jnp.concatenate([d["rhs"], d["rhs"]], axis=0))

        for h in heads:
            d = st[h]
            w = d["sol"][:, HEAD_DIM:2 * HEAD_DIM]
            d["ws_qs"] = _dot(jnp.concatenate([w, d["q_dec"]], axis=0).astype(BF16), s_ref[h].astype(BF16))

        for h in heads:
            d = st[h]
            vnb = (d["sol"][:, 0:HEAD_DIM] - d["ws_qs"][0:C]).astype(BF16)
            d["o"] = d["ws_qs"][C:2 * C] + _dot(d["attn"], vnb)
            s_ref[h] = s_ref[h] * d["s_scale"] + _dot_tn(d["k_dec"], vnb)

        osq = _lane_sums([st[h]["o"] * st[h]["o"] for h in heads])
        for n, h in enumerate(heads):
            o = st[h]["o"]
            sl = slice(h * HEAD_DIM, (h + 1) * HEAD_DIM)
            o = o * lax.rsqrt(osq[n] * (1.0 / HEAD_DIM) + RMS_EPS) * gn
            z = jnp.where(c == 0, zm_ref[:, sl], zx_ref[:, sl])
            o_ref[:, sl] = (o * _silu(z)).astype(o_ref.dtype)


def _gdn(qkv, qkv_meta, z, z_meta, bg_all, gt3, gn_row, n_heads):
    t = qkv.shape[0]
    gw = n_heads * HEAD_DIM
    n_chunks = t // CHUNK + 1
    xmap = lambda c: (jnp.maximum(c - 1, 0), 0)
    fixed = lambda c: (0, 0)
    return pl.pallas_call(
        functools.partial(_gdn_kernel, n_heads=n_heads),
        grid=(n_chunks,),
        in_specs=[pl.BlockSpec((CHUNK, 3 * gw), xmap),
                  pl.BlockSpec((CHUNK, 3 * gw), fixed),
                  pl.BlockSpec((CHUNK, gw), xmap),
                  pl.BlockSpec((CHUNK, gw), fixed),
                  pl.BlockSpec((CHUNK, 2 * n_heads), lambda c: (c, 0)),
                  pl.BlockSpec((1, n_heads, CHUNK), lambda c: (c, 0, 0)),
                  pl.BlockSpec((1, HEAD_DIM), fixed)],
        out_specs=pl.BlockSpec((CHUNK, gw), xmap),
        out_shape=jax.ShapeDtypeStruct((t, gw), BF16),
        scratch_shapes=[pltpu.VMEM((n_heads, HEAD_DIM, HEAD_DIM), F32)],
        compiler_params=_params(1),
        name="gated_delta_rule",
    )(qkv, qkv_meta, z, z_meta, bg_all, gt3, gn_row)


def _router_kernel(r_ref, g_ref, b_ref, rwh_ref, rwl_ref, bias_ref, h1_ref, h1b_ref, h1p_ref,
                   eidx_ref, wsel_ref, pos_ref, cnt_ref, carry_ref, *, tm, n_exp):
    i = pl.program_id(0)
    per = n_exp // N_EXPERT_GROUPS

    @pl.when(i == 0)
    def _():
        carry_ref[...] = jnp.zeros(carry_ref.shape, F32)

    h1 = _ln_rows(r_ref[...], g_ref[...], b_ref[...])
    h1_ref[...] = h1
    hb = h1.astype(BF16)
    h1b_ref[...] = hb
    h1p_ref[...] = _pack_rows(h1)

    hl = (h1 - hb.astype(F32)).astype(BF16)
    logits = _dot_nt(rwh_ref[...], hb) + (_dot_nt(rwl_ref[...], hb) + _dot_nt(rwh_ref[...], hl))
    scores = _sigmoid(logits)
    biased = scores + bias_ref[...]
    neg_inf = F32(-jnp.inf)

    gs_rows = []
    sub = lax.broadcasted_iota(I32, (per, tm), 0)
    for g in range(N_EXPERT_GROUPS):
        xg = biased[g * per:(g + 1) * per, :]
        m1 = jnp.max(xg, axis=0, keepdims=True)
        first = jnp.min(jnp.where(xg == m1, sub, per), axis=0, keepdims=True)
        m2 = jnp.max(jnp.where(sub == first, neg_inf, xg), axis=0, keepdims=True)
        gs_rows.append(m1 + m2)
    gs = jnp.concatenate(gs_rows, axis=0)
    gid = lax.broadcasted_iota(I32, (N_EXPERT_GROUPS, tm), 0)
    grank = jnp.zeros((N_EXPERT_GROUPS, tm), F32)
    for g in range(N_EXPERT_GROUPS):
        rowv = gs[g:g + 1, :]
        grank = grank + jnp.where(rowv > gs, 1.0, jnp.where((rowv == gs) & (gid > g), 1.0, 0.0))
    gsel = jnp.where(grank < TOPK_GROUPS, 1.0, 0.0)
    emask = jnp.concatenate(
        [jnp.broadcast_to(gsel[g:g + 1, :], (per, tm)) for g in range(N_EXPERT_GROUPS)], axis=0)
    masked = jnp.where(emask > 0.0, biased, neg_inf)
    eid = lax.broadcasted_iota(I32, (n_exp, tm), 0)
    erank = jnp.zeros((n_exp, tm), F32)
    for e in range(n_exp):
        rowv = masked[e:e + 1, :]
        erank = erank + jnp.where(rowv > masked, 1.0, jnp.where((rowv == masked) & (eid > e), 1.0, 0.0))
    sel = jnp.where(erank < TOP_K, emask, 0.0)
    wdense = scores * sel
    gate = wdense / jnp.sum(wdense, axis=0, keepdims=True) * ROUTED_SCALE

    er = lax.broadcasted_iota(I32, (n_exp, n_exp), 0)
    ec = lax.broadcasted_iota(I32, (n_exp, n_exp), 1)
    lower = jnp.where(er > ec, 1.0, 0.0).astype(BF16)
    selb = sel.astype(BF16)
    slot = _dot(lower, selb)
    tr = lax.broadcasted_iota(I32, (tm, tm), 0)
    tc = lax.broadcasted_iota(I32, (tm, tm), 1)
    upper = jnp.where(tr < tc, 1.0, 0.0).astype(BF16)
    carry = carry_ref[:, 0:1]
    pos = _dot(selb, upper) + carry
    eid_f = eid.astype(F32)
    e_rows, w_rows, p_rows = [], [], []
    for k in range(TOP_K):
        mk = jnp.where(slot == float(k), sel, 0.0)
        e_rows.append(jnp.sum(mk * eid_f, axis=0, keepdims=True))
        w_rows.append(jnp.sum(mk * gate, axis=0, keepdims=True))
        p_rows.append(jnp.sum(mk * pos, axis=0, keepdims=True))
    eidx_ref[...] = jnp.concatenate(e_rows, axis=0).astype(I32)
    wsel_ref[...] = jnp.concatenate(w_rows, axis=0)
    pos_ref[...] = jnp.concatenate(p_rows, axis=0).astype(I32)
    new_carry = carry + jnp.sum(sel, axis=1, keepdims=True)
    carry_ref[...] = jnp.broadcast_to(new_carry, carry_ref.shape)
    cnt_ref[...] = jnp.broadcast_to(new_carry, cnt_ref.shape).astype(I32)


def _router(r, g, b, rw_hi, rw_lo, bias_col, tm):
    t, d = r.shape
    n_exp = rw_hi.shape[0]
    row = lambda i: (i, 0)
    colb = lambda i: (0, i)
    fixed = lambda i: (0, 0)
    return pl.pallas_call(
        functools.partial(_router_kernel, tm=tm, n_exp=n_exp),
        grid=(t // tm,),
        in_specs=[pl.BlockSpec((tm, d), row),
                  pl.BlockSpec((1, d), fixed),
                  pl.BlockSpec((1, d), fixed),
                  pl.BlockSpec((n_exp, d), fixed),
                  pl.BlockSpec((n_exp, d), fixed),
                  pl.BlockSpec((n_exp, 1), fixed)],
        out_specs=[pl.BlockSpec((tm, d), row),
                   pl.BlockSpec((tm, d), row),
                   pl.BlockSpec((tm, d // 2), row),
                   pl.BlockSpec((TOP_K, tm), colb),
                   pl.BlockSpec((TOP_K, tm), colb),
                   pl.BlockSpec((TOP_K, tm), colb),
                   pl.BlockSpec((n_exp, 128), fixed)],
        out_shape=[jax.ShapeDtypeStruct((t, d), F32),
                   jax.ShapeDtypeStruct((t, d), BF16),
                   jax.ShapeDtypeStruct((t, d // 2), U32),
                   jax.ShapeDtypeStruct((TOP_K, t), I32),
                   jax.ShapeDtypeStruct((TOP_K, t), F32),
                   jax.ShapeDtypeStruct((TOP_K, t), I32),
                   jax.ShapeDtypeStruct((n_exp, 128), I32)],
        scratch_shapes=[pltpu.VMEM((n_exp, 128), F32)],
        compiler_params=_params(1),
        name="ln1_router",
    )(r, g, b, rw_hi, rw_lo, bias_col)


def _dest_kernel(gstart_ref, eidx_ref, pos_ref, dest_ref, *, n_exp):
    eidx = eidx_ref[...]
    acc = pos_ref[...]
    for e in range(n_exp):
        acc = acc + jnp.where(eidx == e, gstart_ref[e], 0)
    dest_ref[...] = acc


def _dest_rows(gstart, eidx, pos):
    n_exp = gstart.shape[0]
    return pl.pallas_call(
        functools.partial(_dest_kernel, n_exp=n_exp),
        grid_spec=pltpu.PrefetchScalarGridSpec(
            num_scalar_prefetch=1,
            grid=(1,),
            in_specs=[pl.BlockSpec(eidx.shape, lambda i, gs: (0, 0)),
                      pl.BlockSpec(pos.shape, lambda i, gs: (0, 0))],
            out_specs=pl.BlockSpec(eidx.shape, lambda i, gs: (0, 0))),
        out_shape=jax.ShapeDtypeStruct(eidx.shape, I32),
        compiler_params=_params(1),
        name="dest_rows",
    )(gstart, eidx, pos)


def _dispatch_kernel(dest_ref, hp_ref, xs_ref, sem, *, tm):
    def issue(t, carry):
        for k in range(TOP_K):
            d = dest_ref[0, 0, t * TOP_K + k]
            pltpu.make_async_copy(hp_ref.at[pl.ds(t, 1)], xs_ref.at[pl.ds(d, 1)], sem).start(priority=k % 2)
        return carry
    lax.fori_loop(0, tm, issue, 0)
    pltpu.make_async_copy(xs_ref.at[pl.ds(0, tm * TOP_K)], xs_ref.at[pl.ds(0, tm * TOP_K)], sem).wait()


def _dispatch(dest_tiles, h1p, tm):
    t, dw = h1p.shape
    return pl.pallas_call(
        functools.partial(_dispatch_kernel, tm=tm),
        grid=(t // tm,),
        in_specs=[pl.BlockSpec((1, 1, tm * TOP_K), lambda i: (i, 0, 0), memory_space=pltpu.SMEM),
                  pl.BlockSpec((tm, dw), lambda i: (i, 0))],
        out_specs=pl.BlockSpec(memory_space=pl.ANY),
        out_shape=jax.ShapeDtypeStruct((t * TOP_K, dw), U32),
        scratch_shapes=[pltpu.SemaphoreType.DMA(())],
        compiler_params=_params(1),
        name="dispatch_rows",
    )(dest_tiles, h1p)


def _pack_rows(y):
    n = y.shape[1] // 2
    yb = y.astype(BF16)
    hi = lax.bitcast_convert_type(yb[:, 0:n].astype(F32), U32)
    lo = lax.bitcast_convert_type(yb[:, n:].astype(F32), U32)
    return hi | (lo >> 16)


def _unpack_rows_f32(xu):
    hi = lax.bitcast_convert_type(xu & jnp.uint32(0xFFFF0000), F32)
    lo = lax.bitcast_convert_type(xu << 16, F32)
    return hi, lo


def _unpack_rows(xu):
    hi, lo = _unpack_rows_f32(xu)
    return hi.astype(BF16), lo.astype(BF16)


def _expert_weights(v, ve_ref, vfe_ref, eo_ref, vne_ref, hbm_refs, f32_refs, bf16_refs, sems):
    @pl.when(vfe_ref[v] == 1)
    def _():
        slot = eo_ref[v] % 2
        e = ve_ref[v]

        def copies(expert, s):
            return [pltpu.make_async_copy(h.at[expert], f.at[s], sem.at[s])
                    for h, f, sem in zip(hbm_refs, f32_refs, sems)]

        @pl.when(v == 0)
        def _():
            for cp in copies(e, slot):
                cp.start()

        for cp in copies(e, slot):
            cp.wait()
        ne = vne_ref[v]

        @pl.when(ne >= 0)
        def _():
            for cp in copies(ne, 1 - slot):
                cp.start()

        for f, bf in zip(f32_refs, bf16_refs):
            bf[...] = f[slot].astype(BF16)


def _visit_subblocks(vt_ref, ve_ref, vft_ref, gs_ref, ge_ref, v, tm, sb, o_ref, compute):
    e = ve_ref[v]
    gs, ge = gs_ref[e], ge_ref[e]
    first = vft_ref[v] == 1
    width = o_ref.shape[1]
    t0 = vt_ref[v] * tm
    whole = (t0 >= gs) & (t0 + tm <= ge)

    @pl.when(whole)
    def _():
        o_ref[...] = compute(0, tm).astype(o_ref.dtype)

    for s in range(tm // sb):
        r0 = t0 + s * sb
        hit = (r0 < ge) & (r0 + sb > gs)
        rows = pl.ds(s * sb, sb)

        @pl.when(hit & jnp.logical_not(whole))
        def _():
            y = compute(s * sb, sb).astype(o_ref.dtype)
            owned = (r0 >= gs) & (r0 + sb <= ge)

            @pl.when(owned)
            def _():
                o_ref[rows, :] = y

            @pl.when(jnp.logical_not(owned))
            def _():
                ridx = r0 + lax.broadcasted_iota(I32, (sb, width), 0)
                mask = (ridx >= gs) & (ridx < ge)

                @pl.when(first)
                def _():
                    o_ref[rows, :] = jnp.where(mask, y, jnp.zeros_like(y))

                @pl.when(jnp.logical_not(first))
                def _():
                    o_ref[rows, :] = jnp.where(mask, y, o_ref[rows, :])

        @pl.when(jnp.logical_not(hit) & first)
        def _():
            o_ref[rows, :] = jnp.zeros((sb, width), o_ref.dtype)


def _gmm_up_kernel(vt_ref, ve_ref, vft_ref, vfe_ref, eo_ref, vne_ref, gs_ref, ge_ref, nv_ref,
                   xs_ref, wg_hbm, wu_hbm, o_ref, wgf_ref, wuf_ref, wgb_ref, wub_ref, semg, semu, *, tm, sb):
    v = pl.program_id(0)

    @pl.when(v < nv_ref[0])
    def _():
        _expert_weights(v, ve_ref, vfe_ref, eo_ref, vne_ref, (wg_hbm, wu_hbm), (wgf_ref, wuf_ref),
                        (wgb_ref, wub_ref), (semg, semu))

        def compute(start, size):
            hi, lo = _unpack_rows(xs_ref[start:start + size, :])
            kh = hi.shape[1]
            hg = _dot(hi, wgb_ref[0:kh, :]) + _dot(lo, wgb_ref[kh:, :])
            hu = _dot(hi, wub_ref[0:kh, :]) + _dot(lo, wub_ref[kh:, :])
            return _silu(hg) * hu

        _visit_subblocks(vt_ref, ve_ref, vft_ref, gs_ref, ge_ref, v, tm, sb, o_ref, compute)


def _gmm_down_kernel(vt_ref, ve_ref, vft_ref, vfe_ref, eo_ref, vne_ref, gs_ref, ge_ref, nv_ref,
                     a_ref, wd_hbm, o_ref, wdf_ref, wdb_ref, semd, *, tm, sb):
    v = pl.program_id(0)

    @pl.when(v < nv_ref[0])
    def _():
        _expert_weights(v, ve_ref, vfe_ref, eo_ref, vne_ref, (wd_hbm,), (wdf_ref,), (wdb_ref,), (semd,))

        def compute(start, size):
            return _pack_rows(_dot(a_ref[start:start + size, :], wdb_ref[...]))

        _visit_subblocks(vt_ref, ve_ref, vft_ref, gs_ref, ge_ref, v, tm, sb, o_ref, compute)


def _visit_schedule(counts, n_rows, tm):
    n_exp = counts.shape[0]
    n_tiles = n_rows // tm
    n_visits = n_tiles + n_exp - 1
    ends = jnp.cumsum(counts)
    starts = ends - counts
    first_tile = starts // tm
    last_tile = jnp.maximum(ends - 1, 0) // tm
    tiles_e = jnp.where(counts > 0, last_tile - first_tile + 1, 0)
    vend = jnp.cumsum(tiles_e)
    vstart = vend - tiles_e
    total = vend[-1]
    v = jnp.arange(n_visits, dtype=I32)
    vc = jnp.minimum(v, total - 1)
    ve = jnp.sum(vend[None, :] <= vc[:, None], axis=1).astype(I32)
    vt = (vc - vstart[ve] + first_tile[ve]).astype(I32)
    prev_t = jnp.concatenate([jnp.full((1,), -1, I32), vt[:-1]])
    prev_e = jnp.concatenate([jnp.full((1,), -1, I32), ve[:-1]])
    vft = (vt != prev_t).astype(I32)
    vfe = (ve != prev_e).astype(I32)
    eo = (jnp.cumsum(vfe) - 1).astype(I32)
    eidx = jnp.arange(n_exp, dtype=I32)
    later = lax.cummin(jnp.where(counts > 0, eidx, n_exp), axis=0, reverse=True)
    nxt = jnp.concatenate([later[1:], jnp.full((1,), n_exp, I32)])
    vne = jnp.where(nxt < n_exp, nxt, -1)[ve].astype(I32)
    sched = (vt, ve, vft, vfe, eo, vne, starts.astype(I32), ends.astype(I32), total.reshape(1).astype(I32))
    return sched, n_visits


def _gmm_up(sched, n_visits, xs, wg, wu, tm, sb):
    n, dw = xs.shape
    n_exp, d, f = wg.shape
    return pl.pallas_call(
        functools.partial(_gmm_up_kernel, tm=tm, sb=sb),
        grid_spec=pltpu.PrefetchScalarGridSpec(
            num_scalar_prefetch=len(sched),
            grid=(n_visits,),
            in_specs=[pl.BlockSpec((tm, dw), lambda v, vt, *_: (vt[v], 0)),
                      pl.BlockSpec(memory_space=pl.ANY),
                      pl.BlockSpec(memory_space=pl.ANY)],
            out_specs=pl.BlockSpec((tm, f), lambda v, vt, *_: (vt[v], 0)),
            scratch_shapes=[pltpu.VMEM((2, d, f), F32), pltpu.VMEM((2, d, f), F32),
                            pltpu.VMEM((d, f), BF16), pltpu.VMEM((d, f), BF16),
                            pltpu.SemaphoreType.DMA((2,)), pltpu.SemaphoreType.DMA((2,))]),
        out_shape=jax.ShapeDtypeStruct((n, f), BF16),
        compiler_params=_params(1),
        name="experts_gate_up",
    )(*sched, xs, wg, wu)


def _gmm_down(sched, n_visits, act, wd, tm, sb):
    n, f = act.shape
    n_exp, _, d = wd.shape
    return pl.pallas_call(
        functools.partial(_gmm_down_kernel, tm=tm, sb=sb),
        grid_spec=pltpu.PrefetchScalarGridSpec(
            num_scalar_prefetch=len(sched),
            grid=(n_visits,),
            in_specs=[pl.BlockSpec((tm, f), lambda v, vt, *_: (vt[v], 0)),
                      pl.BlockSpec(memory_space=pl.ANY)],
            out_specs=pl.BlockSpec((tm, d // 2), lambda v, vt, *_: (vt[v], 0)),
            scratch_shapes=[pltpu.VMEM((2, f, d), F32), pltpu.VMEM((f, d), BF16),
                            pltpu.SemaphoreType.DMA((2,))]),
        out_shape=jax.ShapeDtypeStruct((n, d // 2), U32),
        compiler_params=_params(1),
        name="experts_down",
    )(*sched, act, wd)


def _combine_kernel(dcur_ref, dnxt_ref, w_ref, h1_ref, as_ref, wsd_ref, g_ref, b_ref, ys_ref, o_ref,
                    gbuf_ref, ybuf_ref, sem, *, tm, alpha):
    i = pl.program_id(0)
    n = pl.num_programs(0)
    slot = i % 2
    groups = tm // SUBLANES
    d = o_ref.shape[1]
    half = d // 2

    def issue_token(dref, dst_slot, t):
        for k in range(TOP_K):
            src = dref[0, 0, t * TOP_K + k]
            pltpu.make_async_copy(ys_ref.at[pl.ds(src, 1)], gbuf_ref.at[dst_slot, k, pl.ds(t, 1)],
                                  sem.at[dst_slot]).start(priority=k % 2)

    def wait_slot(s):
        for k in range(TOP_K):
            pltpu.make_async_copy(ys_ref.at[pl.ds(0, tm)], gbuf_ref.at[s, k], sem.at[s]).wait()

    @pl.when(i == 0)
    def _():
        def first(t, carry):
            issue_token(dcur_ref, 0, t)
            return carry
        lax.fori_loop(0, tm, first, 0)

    ybuf_ref[...] = alpha * h1_ref[...] + _dot(as_ref[...], wsd_ref[...])
    wait_slot(slot)

    def reduce_group(r, carry):
        r8 = pl.multiple_of(r * SUBLANES, SUBLANES)
        rows = pl.ds(r8, SUBLANES)
        y_hi = ybuf_ref[rows, 0:half]
        y_lo = ybuf_ref[rows, half:]
        w = w_ref[rows, :]
        for j in range(SUBLANES):
            issue_token(dnxt_ref, 1 - slot, r8 + j)
            for k in range(j * TOP_K // SUBLANES, (j + 1) * TOP_K // SUBLANES):
                e_hi, e_lo = _unpack_rows_f32(gbuf_ref[slot, k, rows, :])
                wk = w[:, k:k + 1]
                y_hi = y_hi + wk * e_hi
                y_lo = y_lo + wk * e_lo
        ybuf_ref[rows, 0:half] = y_hi
        ybuf_ref[rows, half:] = y_lo
        return carry
    lax.fori_loop(0, groups, reduce_group, 0)
    o_ref[...] = _ln_rows(ybuf_ref[...], g_ref[...], b_ref[...])

    @pl.when(i == n - 1)
    def _():
        wait_slot(1 - slot)


def _combine(dest_tiles, w_tok, h1, act_s, wsd, g, b, ys, tm, alpha):
    t, d = h1.shape
    f = act_s.shape[1]
    n = t // tm
    row = lambda i: (i, 0)
    fixed = lambda i: (0, 0)
    return pl.pallas_call(
        functools.partial(_combine_kernel, tm=tm, alpha=alpha),
        grid=(n,),
        in_specs=[pl.BlockSpec((1, 1, tm * TOP_K), lambda i: (i, 0, 0), memory_space=pltpu.SMEM),
                  pl.BlockSpec((1, 1, tm * TOP_K), lambda i: (jnp.minimum(i + 1, n - 1), 0, 0),
                               memory_space=pltpu.SMEM),
                  pl.BlockSpec((tm, TOP_K), row),
                  pl.BlockSpec((tm, d), row),
                  pl.BlockSpec((tm, f), row),
                  pl.BlockSpec((f, d), fixed),
                  pl.BlockSpec((1, d), fixed),
                  pl.BlockSpec((1, d), fixed),
                  pl.BlockSpec(memory_space=pl.ANY)],
        out_specs=pl.BlockSpec((tm, d), row),
        out_shape=jax.ShapeDtypeStruct((t, d), F32),
        scratch_shapes=[pltpu.VMEM((2, TOP_K, tm, d // 2), U32),
                        pltpu.VMEM((tm, d), F32),
                        pltpu.SemaphoreType.DMA((2,))],
        compiler_params=_params(1),
        name="combine_ln2",
    )(dest_tiles, dest_tiles, w_tok, h1, act_s, wsd, g, b, ys)


def _tile(n, pref):
    t = min(n, pref)
    assert n % t == 0, (n, pref)
    return t


def kernel(x, meta_tokens, emb_ln_g, emb_ln_b, w_in, conv_w, conv_b, conv_ln_g, conv_ln_b, short_conv_w, a_log, dt_bias, gdn_norm_g, w_out, ln1_g, ln1_b, router_w, router_bias, expert_w_gate, expert_w_up, expert_w_down, shared_w_gate, shared_w_up, shared_w_down, ln2_g, ln2_b):
    depth = w_in.shape[0]
    assert depth == 1 and x.shape[0] == 1, "single layer, single sequence"
    t, d = x.shape[1], x.shape[2]
    cw = conv_w.shape[2]
    n_heads = a_log.shape[1]
    gw = n_heads * HEAD_DIM
    alpha = (2.0 * depth) ** 0.25
    assert meta_tokens.shape[0] == N_META and t % CHUNK == 0 and w_in.shape[2] == 2 * cw + 4 * gw + 2 * n_heads
    row = lambda a: a.reshape(1, -1)

    h_b, h_stats = _embed_ln(x[0], row(emb_ln_g), row(emb_ln_b), _tile(t, 256), "embed_ln")
    hm_b, _ = _embed_ln(meta_tokens, row(emb_ln_g), row(emb_ln_b), N_META, "embed_ln_meta")

    w_t = jnp.transpose(w_in[0])
    tm = _tile(t, 1024)
    zeros_h = jnp.zeros((n_heads,), F32)
    alog_row = row(jnp.concatenate([zeros_h, a_log[0]]))
    dtb_row = row(jnp.concatenate([zeros_h, dt_bias[0]]))
    c_x, c_m = _inproj_glu(h_b, hm_b, w_t, 0, cw, cw, tm, _tile(cw, 256))
    qkv_x, qkv_m = _inproj_conv(h_b, hm_b, w_t, short_conv_w[0], 2 * cw, 3 * gw, tm, _tile(3 * gw, 512))
    z_x, z_m = _inproj_plain(h_b, hm_b, w_t, 2 * cw + 3 * gw, gw, tm, _tile(gw, 512))
    bg_x, bg_m = _inproj_decay(h_b, hm_b, w_t[2 * cw + 4 * gw:], alog_row, dtb_row, tm, n_heads)

    y_conv = _conv_module(c_x, c_m, conv_w[0], row(conv_b[0]), row(conv_ln_g[0]), row(conv_ln_b[0]),
                          _tile(t, 256))

    front = lambda a: jnp.pad(a, ((CHUNK - N_META, 0), (0, 0)))
    bg_all = jnp.concatenate([front(bg_m), bg_x], axis=0)
    gt3 = bg_all[:, n_heads:].reshape(-1, CHUNK, n_heads).transpose(0, 2, 1)
    y_gdn = _gdn(qkv_x, front(qkv_m), z_x, front(z_m), bg_all, gt3, row(gdn_norm_g[0]), n_heads)

    r1 = _proj_residual(y_conv, y_gdn, w_out[0], x[0], h_stats, row(emb_ln_g), row(emb_ln_b), alpha, tm,
                        _tile(d, 512))
    rwt = router_w[0].T
    rw_hi = rwt.astype(BF16)
    rw_lo = (rwt - rw_hi.astype(F32)).astype(BF16)
    h1, h1_b, h1_p, eidx, wsel, pos, cnt = _router(
        r1, row(ln1_g[0]), row(ln1_b[0]), rw_hi, rw_lo, router_bias[0].reshape(-1, 1), _tile(t, 256))
    counts = cnt[:, 0]
    gstart = jnp.cumsum(counts) - counts
    dest = _dest_rows(gstart.astype(I32), eidx, pos)

    tmd = _tile(t, 256)
    xs = _dispatch(dest.T.reshape(t // tmd, 1, tmd * TOP_K), h1_p, tmd)
    tmg = _tile(t * TOP_K, 512)
    sbg = _tile(tmg, 128)
    sched, n_visits = _visit_schedule(counts, t * TOP_K, tmg)
    act = _gmm_up(sched, n_visits, xs, expert_w_gate[0], expert_w_up[0], tmg, sbg)
    ys = _gmm_down(sched, n_visits, act, expert_w_down[0], tmg, _tile(tmg, 256))

    act_s = _swiglu(h1_b, shared_w_gate[0], shared_w_up[0], tm, _tile(shared_w_gate.shape[2], 256), BF16)
    tmc = _tile(t, 128)
    out = _combine(dest.T.reshape(t // tmc, 1, tmc * TOP_K), wsel.T, h1, act_s, shared_w_down[0].astype(BF16),
                   row(ln2_g[0]), row(ln2_b[0]), ys, tmc, alpha)
    return out[None]
```

```python
import functools

import jax
import jax.numpy as jnp
from jax import lax
from jax.experimental import pallas as pl
from jax.experimental.pallas import tpu as pltpu

F32 = jnp.float32
BF16 = jnp.bfloat16
I32 = jnp.int32
U32 = jnp.uint32

N_META = 16
CHUNK = 64
HEAD_DIM = 128
HEAD_GROUP = 16
TOP_K = 8
N_EXPERT_GROUPS = 8
TOPK_GROUPS = 4
ROUTED_SCALE = 2.5
LN_EPS = 1e-5
RMS_EPS = 1e-6
CONV_HALO = 32
CONV_ROWS = 32
CONV_LANES = 512
SUBLANES = 8
LANES = 128
VMEM_LIMIT = 56 * 1024 * 1024


def _params(n_axes, vmem=VMEM_LIMIT):
    return pltpu.CompilerParams(dimension_semantics=("arbitrary",) * n_axes, vmem_limit_bytes=vmem)


def _ln_rows(x, g, b):
    mu = jnp.mean(x, -1, keepdims=True)
    xc = x - mu
    var = jnp.mean(xc * xc, -1, keepdims=True)
    return xc * lax.rsqrt(var + LN_EPS) * g + b


def _sigmoid(x):
    return 1.0 / (1.0 + jnp.exp(-x))


def _silu(x):
    return x * _sigmoid(x)


def _dot(a, b):
    return jnp.dot(a, b, preferred_element_type=F32)


def _dot_nt(a, b, precision=None):
    return lax.dot_general(a, b, (((1,), (1,)), ((), ())), precision=precision, preferred_element_type=F32)


def _dot_tn(a, b):
    return lax.dot_general(a, b, (((0,), (0,)), ((), ())), preferred_element_type=F32)


def _embed_ln_kernel(x_ref, g_ref, b_ref, hb_ref, st_ref):
    x = x_ref[...]
    mu = jnp.mean(x, -1, keepdims=True)
    xc = x - mu
    rs = lax.rsqrt(jnp.mean(xc * xc, -1, keepdims=True) + LN_EPS)
    hb_ref[...] = (xc * rs * g_ref[...] + b_ref[...]).astype(BF16)
    lane = lax.broadcasted_iota(I32, st_ref.shape, 1)
    st_ref[...] = jnp.where(lane == 0, mu, jnp.where(lane == 1, rs, 0.0))


def _embed_ln(x, g, b, tm, name):
    m, d = x.shape
    return pl.pallas_call(
        _embed_ln_kernel,
        grid=(m // tm,),
        in_specs=[pl.BlockSpec((tm, d), lambda i: (i, 0)),
                  pl.BlockSpec((1, d), lambda i: (0, 0)),
                  pl.BlockSpec((1, d), lambda i: (0, 0))],
        out_specs=[pl.BlockSpec((tm, d), lambda i: (i, 0)),
                   pl.BlockSpec((tm, LANES), lambda i: (i, 0))],
        out_shape=[jax.ShapeDtypeStruct((m, d), BF16), jax.ShapeDtypeStruct((m, LANES), F32)],
        compiler_params=_params(1),
        name=name,
    )(x, g, b)


def _inproj_plain_kernel(a_ref, am_ref, w_ref, o_ref, om_ref, wb_ref):
    @pl.when(pl.program_id(1) == 0)
    def _():
        wb_ref[...] = w_ref[...].astype(BF16)
        om_ref[...] = _dot_nt(am_ref[...], wb_ref[...])
    o_ref[...] = _dot_nt(a_ref[...], wb_ref[...])


def _inproj_conv_kernel(a_ref, am_ref, w_ref, cw_ref, o_ref, om_ref, wb_ref, hbuf_ref):
    i = pl.program_id(1)
    tm = a_ref.shape[0]
    kw = cw_ref.shape[0]

    def conv_silu(rows):
        acc = cw_ref[kw - 1:kw, :] * hbuf_ref[SUBLANES:SUBLANES + rows, :]
        for j in range(kw - 1):
            off = SUBLANES - (kw - 1) + j
            acc = acc + cw_ref[j:j + 1, :] * hbuf_ref[off:off + rows, :]
        return _silu(acc)

    @pl.when(i == 0)
    def _():
        wb_ref[...] = w_ref[...].astype(BF16)
        hbuf_ref[0:SUBLANES, :] = jnp.zeros((SUBLANES, hbuf_ref.shape[1]), F32)
        hbuf_ref[SUBLANES:SUBLANES + N_META, :] = _dot_nt(am_ref[...], wb_ref[...])
        om_ref[...] = conv_silu(N_META)
        hbuf_ref[0:SUBLANES, :] = hbuf_ref[N_META:N_META + SUBLANES, :]

    @pl.when(i > 0)
    def _():
        hbuf_ref[0:SUBLANES, :] = hbuf_ref[tm:tm + SUBLANES, :]

    hbuf_ref[SUBLANES:SUBLANES + tm, :] = _dot_nt(a_ref[...], wb_ref[...])
    o_ref[...] = conv_silu(tm)


def _inproj_glu_kernel(a_ref, am_ref, w1_ref, w2_ref, o_ref, om_ref, w1b_ref, w2b_ref):
    @pl.when(pl.program_id(1) == 0)
    def _():
        w1b_ref[...] = w1_ref[...].astype(BF16)
        w2b_ref[...] = w2_ref[...].astype(BF16)
        am = am_ref[...]
        om_ref[...] = _dot_nt(am, w1b_ref[...]) * _sigmoid(_dot_nt(am, w2b_ref[...]))
    a = a_ref[...]
    o_ref[...] = _dot_nt(a, w1b_ref[...]) * _sigmoid(_dot_nt(a, w2b_ref[...]))


def _decay_epilogue(p, alog, dtb, n_heads):
    x = p + dtb
    softplus = jnp.maximum(x, 0.0) + jnp.log1p(jnp.exp(-jnp.abs(x)))
    g = -jnp.exp(alog) * softplus
    lane = lax.broadcasted_iota(I32, p.shape, 1)
    return jnp.where(lane < n_heads, _sigmoid(p), g)


def _inproj_decay_kernel(a_ref, am_ref, w_ref, alog_ref, dtb_ref, o_ref, om_ref, wb_ref, *, n_heads):
    @pl.when(pl.program_id(1) == 0)
    def _():
        wb_ref[...] = w_ref[...].astype(BF16)
        om_ref[...] = _decay_epilogue(_dot_nt(am_ref[...], wb_ref[...]), alog_ref[...], dtb_ref[...], n_heads)
    o_ref[...] = _decay_epilogue(_dot_nt(a_ref[...], wb_ref[...]), alog_ref[...], dtb_ref[...], n_heads)


def _swiglu_kernel(a_ref, w1_ref, w2_ref, o_ref, w1b_ref, w2b_ref):
    @pl.when(pl.program_id(1) == 0)
    def _():
        w1b_ref[...] = w1_ref[...].astype(BF16)
        w2b_ref[...] = w2_ref[...].astype(BF16)
    a = a_ref[...]
    o_ref[...] = (_silu(_dot(a, w1b_ref[...])) * _dot(a, w2b_ref[...])).astype(o_ref.dtype)


def _proj_residual_kernel(a1_ref, a2_ref, w_ref, x_ref, st_ref, g_ref, b_ref, o_ref, wb_ref, *, alpha, k1):
    @pl.when(pl.program_id(1) == 0)
    def _():
        wb_ref[...] = w_ref[...].astype(BF16)
    acc = _dot(a1_ref[...], wb_ref[0:k1, :]) + _dot(a2_ref[...], wb_ref[k1:, :])
    st = st_ref[...]
    resid = (x_ref[...] - st[:, 0:1]) * st[:, 1:2] * g_ref[...] + b_ref[...]
    o_ref[...] = alpha * resid + acc


def _inproj_plain(a, am, w_t, row0, n, tm, tn):
    m, k = a.shape
    mm = am.shape[0]
    rb = row0 // tn
    return pl.pallas_call(
        _inproj_plain_kernel,
        grid=(n // tn, m // tm),
        in_specs=[pl.BlockSpec((tm, k), lambda j, i: (i, 0)),
                  pl.BlockSpec((mm, k), lambda j, i: (0, 0)),
                  pl.BlockSpec((tn, k), lambda j, i: (j + rb, 0))],
        out_specs=[pl.BlockSpec((tm, tn), lambda j, i: (i, j)),
                   pl.BlockSpec((mm, tn), lambda j, i: (0, j))],
        out_shape=[jax.ShapeDtypeStruct((m, n), F32), jax.ShapeDtypeStruct((mm, n), F32)],
        scratch_shapes=[pltpu.VMEM((tn, k), BF16)],
        compiler_params=_params(2),
        name="inproj_z",
    )(a, am, w_t)


def _inproj_conv(a, am, w_t, conv_w, row0, n, tm, tn):
    m, k = a.shape
    mm = am.shape[0]
    kw = conv_w.shape[0]
    assert kw <= SUBLANES and mm == N_META and N_META % SUBLANES == 0
    rb = row0 // tn
    return pl.pallas_call(
        _inproj_conv_kernel,
        grid=(n // tn, m // tm),
        in_specs=[pl.BlockSpec((tm, k), lambda j, i: (i, 0)),
                  pl.BlockSpec((mm, k), lambda j, i: (0, 0)),
                  pl.BlockSpec((tn, k), lambda j, i: (j + rb, 0)),
                  pl.BlockSpec((kw, tn), lambda j, i: (0, j))],
        out_specs=[pl.BlockSpec((tm, tn), lambda j, i: (i, j)),
                   pl.BlockSpec((mm, tn), lambda j, i: (0, j))],
        out_shape=[jax.ShapeDtypeStruct((m, n), F32), jax.ShapeDtypeStruct((mm, n), F32)],
        scratch_shapes=[pltpu.VMEM((tn, k), BF16), pltpu.VMEM((SUBLANES + tm, tn), F32)],
        compiler_params=_params(2),
        name="inproj_qkv_conv",
    )(a, am, w_t, conv_w)


def _inproj_glu(a, am, w_t, row1, row2, n, tm, tn):
    m, k = a.shape
    mm = am.shape[0]
    r1, r2 = row1 // tn, row2 // tn
    return pl.pallas_call(
        _inproj_glu_kernel,
        grid=(n // tn, m // tm),
        in_specs=[pl.BlockSpec((tm, k), lambda j, i: (i, 0)),
                  pl.BlockSpec((mm, k), lambda j, i: (0, 0)),
                  pl.BlockSpec((tn, k), lambda j, i: (j + r1, 0)),
                  pl.BlockSpec((tn, k), lambda j, i: (j + r2, 0))],
        out_specs=[pl.BlockSpec((tm, tn), lambda j, i: (i, j)),
                   pl.BlockSpec((mm, tn), lambda j, i: (0, j))],
        out_shape=[jax.ShapeDtypeStruct((m, n), F32), jax.ShapeDtypeStruct((mm, n), F32)],
        scratch_shapes=[pltpu.VMEM((tn, k), BF16), pltpu.VMEM((tn, k), BF16)],
        compiler_params=_params(2),
        name="inproj_glu",
    )(a, am, w_t, w_t)


def _inproj_decay(a, am, w_ba_t, alog_row, dtb_row, tm, n_heads):
    m, k = a.shape
    mm = am.shape[0]
    n = w_ba_t.shape[0]
    return pl.pallas_call(
        functools.partial(_inproj_decay_kernel, n_heads=n_heads),
        grid=(1, m // tm),
        in_specs=[pl.BlockSpec((tm, k), lambda j, i: (i, 0)),
                  pl.BlockSpec((mm, k), lambda j, i: (0, 0)),
                  pl.BlockSpec((n, k), lambda j, i: (0, 0)),
                  pl.BlockSpec((1, n), lambda j, i: (0, 0)),
                  pl.BlockSpec((1, n), lambda j, i: (0, 0))],
        out_specs=[pl.BlockSpec((tm, n), lambda j, i: (i, 0)),
                   pl.BlockSpec((mm, n), lambda j, i: (0, 0))],
        out_shape=[jax.ShapeDtypeStruct((m, n), F32), jax.ShapeDtypeStruct((mm, n), F32)],
        scratch_shapes=[pltpu.VMEM((n, k), BF16)],
        compiler_params=_params(2),
        name="inproj_decay",
    )(a, am, w_ba_t, alog_row, dtb_row)


def _swiglu(a, w1, w2, tm, tn, out_dtype):
    m, k = a.shape
    n = w1.shape[1]
    return pl.pallas_call(
        _swiglu_kernel,
        grid=(n // tn, m // tm),
        in_specs=[pl.BlockSpec((tm, k), lambda j, i: (i, 0)),
                  pl.BlockSpec((k, tn), lambda j, i: (0, j)),
                  pl.BlockSpec((k, tn), lambda j, i: (0, j))],
        out_specs=pl.BlockSpec((tm, tn), lambda j, i: (i, j)),
        out_shape=jax.ShapeDtypeStruct((m, n), out_dtype),
        scratch_shapes=[pltpu.VMEM((k, tn), BF16), pltpu.VMEM((k, tn), BF16)],
        compiler_params=_params(2),
        name="shared_swiglu",
    )(a, w1, w2)


def _proj_residual(a1, a2, w, x, stats, g, b, alpha, tm, tn):
    m, k1 = a1.shape
    k2 = a2.shape[1]
    n = w.shape[1]
    return pl.pallas_call(
        functools.partial(_proj_residual_kernel, alpha=alpha, k1=k1),
        grid=(n // tn, m // tm),
        in_specs=[pl.BlockSpec((tm, k1), lambda j, i: (i, 0)),
                  pl.BlockSpec((tm, k2), lambda j, i: (i, 0)),
                  pl.BlockSpec((k1 + k2, tn), lambda j, i: (0, j)),
                  pl.BlockSpec((tm, tn), lambda j, i: (i, j)),
                  pl.BlockSpec((tm, LANES), lambda j, i: (i, 0)),
                  pl.BlockSpec((1, tn), lambda j, i: (0, j)),
                  pl.BlockSpec((1, tn), lambda j, i: (0, j))],
        out_specs=pl.BlockSpec((tm, tn), lambda j, i: (i, j)),
        out_shape=jax.ShapeDtypeStruct((m, n), F32),
        scratch_shapes=[pltpu.VMEM((k1 + k2, tn), BF16)],
        compiler_params=_params(2),
        name="outproj_residual",
    )(a1, a2, w, x, stats, g, b)


def _conv_module_kernel(c_ref, cm_ref, w_ref, b_ref, g_ref, be_ref, o_ref, buf_ref, sh_ref, acc_ref, *, ts, kw):
    i = pl.program_id(0)
    ch = c_ref.shape[1]

    @pl.when(i == 0)
    def _():
        buf_ref[0:CONV_HALO - N_META, :] = jnp.zeros((CONV_HALO - N_META, ch), F32)
        buf_ref[CONV_HALO - N_META:CONV_HALO, :] = cm_ref[...]

    @pl.when(i > 0)
    def _():
        buf_ref[0:CONV_HALO, :] = buf_ref[ts:ts + CONV_HALO, :]

    buf_ref[CONV_HALO:CONV_HALO + ts, :] = c_ref[...]
    base = CONV_HALO - (kw - 1)
    span = ts + CONV_HALO - SUBLANES
    for r in range(1, SUBLANES):
        sh_ref[r - 1] = buf_ref[r:r + span, :]

    def row_block(rb, carry):
        r0 = pl.multiple_of(rb * CONV_ROWS, CONV_ROWS)
        for cb in range(ch // CONV_LANES):
            lanes = slice(cb * CONV_LANES, (cb + 1) * CONV_LANES)
            acc = jnp.zeros((CONV_ROWS, CONV_LANES), F32) + b_ref[:, lanes]
            for j in range(kw):
                off = base + j
                r, q = off % SUBLANES, off - off % SUBLANES
                if r == 0:
                    src = buf_ref[pl.ds(r0 + q, CONV_ROWS), lanes]
                else:
                    src = sh_ref[r - 1, pl.ds(r0 + q, CONV_ROWS), lanes]
                acc = acc + w_ref[j:j + 1, lanes] * src
            acc_ref[pl.ds(r0, CONV_ROWS), lanes] = acc
        return carry
    lax.fori_loop(0, ts // CONV_ROWS, row_block, 0)
    y = _ln_rows(acc_ref[...], g_ref[...], be_ref[...])
    o_ref[...] = _silu(y).astype(o_ref.dtype)


def _conv_module(c, c_meta, w, b, g, be, ts):
    t, ch = c.shape
    kw = w.shape[0]
    assert kw - 1 <= CONV_HALO and N_META <= CONV_HALO and ts >= CONV_HALO
    assert ts % CONV_ROWS == 0 and ch % CONV_LANES == 0
    return pl.pallas_call(
        functools.partial(_conv_module_kernel, ts=ts, kw=kw),
        grid=(t // ts,),
        in_specs=[pl.BlockSpec((ts, ch), lambda i: (i, 0)),
                  pl.BlockSpec((N_META, ch), lambda i: (0, 0)),
                  pl.BlockSpec((kw, ch), lambda i: (0, 0)),
                  pl.BlockSpec((1, ch), lambda i: (0, 0)),
                  pl.BlockSpec((1, ch), lambda i: (0, 0)),
                  pl.BlockSpec((1, ch), lambda i: (0, 0))],
        out_specs=pl.BlockSpec((ts, ch), lambda i: (i, 0)),
        out_shape=jax.ShapeDtypeStruct((t, ch), BF16),
        scratch_shapes=[pltpu.VMEM((CONV_HALO + ts, ch), F32),
                        pltpu.VMEM((SUBLANES - 1, ts + CONV_HALO - SUBLANES, ch), F32),
                        pltpu.VMEM((ts, ch), F32)],
        compiler_params=_params(1),
        name="conv_module",
    )(c, c_meta, w, b, g, be)


def _lane_sums(blocks):
    rows = blocks[0].shape[0]
    x = jnp.concatenate(blocks, axis=0)
    hi = x.astype(BF16)
    lo = (x - hi.astype(F32)).astype(BF16)
    ones = jnp.ones((HEAD_DIM, HEAD_DIM), BF16)
    s = _dot(hi, ones) + _dot(lo, ones)
    return [s[n * rows:(n + 1) * rows] for n in range(len(blocks))]


def _gdn_kernel(ax_ref, am_ref, zx_ref, zm_ref, bg_ref, gt_ref, gn_ref, o_ref, s_ref, *, n_heads):
    c = pl.program_id(0)
    gw = n_heads * HEAD_DIM
    C = CHUNK

    @pl.when(c == 0)
    def _():
        s_ref[...] = jnp.zeros(s_ref.shape, F32)

    def act(col0):
        cols = slice(col0, col0 + HEAD_DIM)
        return jnp.where(c == 0, am_ref[:, cols], ax_ref[:, cols])

    bg = bg_ref[...]
    beta_all = bg[:, 0:n_heads]
    g_all = bg[:, n_heads:2 * n_heads]
    row = lax.broadcasted_iota(I32, (C, C), 0)
    col = lax.broadcasted_iota(I32, (C, C), 1)
    incl = row >= col
    strict = row > col
    tri_incl = jnp.where(incl, 1.0, 0.0).astype(F32)
    tri_upper = jnp.where(row <= col, 1.0, 0.0).astype(F32)
    hp = lax.Precision.HIGHEST
    gc_cols = jnp.dot(tri_incl, g_all, precision=hp, preferred_element_type=F32)
    gc_rows = jnp.dot(gt_ref[0], tri_upper, precision=hp, preferred_element_type=F32)
    eye = jnp.where(row == col, 1.0, 0.0).astype(F32)
    right_half = lax.broadcasted_iota(I32, (C, 2 * C), 1) >= C
    gn = gn_ref[...]

    for h0 in range(0, n_heads, HEAD_GROUP):
        heads = range(h0, min(h0 + HEAD_GROUP, n_heads))
        st = {h: {} for h in heads}

        qs = [act(h * HEAD_DIM) for h in heads]
        ks = [act(gw + h * HEAD_DIM) for h in heads]
        ssq = _lane_sums([x * x for x in qs + ks])

        for n, h in enumerate(heads):
            d = st[h]
            q = qs[n] * lax.rsqrt(ssq[n] + RMS_EPS) * (HEAD_DIM ** -0.5)
            k = ks[n] * lax.rsqrt(ssq[len(qs) + n] + RMS_EPS)
            beta = beta_all[:, h:h + 1]
            gcol = gc_cols[:, h:h + 1]
            grow = gc_rows[h:h + 1, :]
            g_last = gc_cols[C - 1:C, h:h + 1]
            d["decay"] = jnp.where(incl, jnp.exp(jnp.where(incl, gcol - grow, 0.0)), 0.0)
            egc = jnp.exp(gcol)
            kb = k * beta
            d["a1"] = _dot_nt(jnp.concatenate([kb, q], axis=0).astype(BF16), k.astype(BF16))
            v = act(2 * gw + h * HEAD_DIM)
            d["rhs"] = jnp.concatenate([v * beta, kb * egc], axis=1).astype(BF16)
            d["q_dec"] = q * egc
            d["k_dec"] = (k * jnp.exp(g_last - gcol)).astype(BF16)
            d["s_scale"] = jnp.exp(g_last)

        for h in heads:
            d = st[h]
            lmat = jnp.where(strict, d["a1"][0:C] * d["decay"], 0.0)
            d["attn"] = (d["a1"][C:2 * C] * d["decay"]).astype(BF16)
            d["wmat"] = jnp.concatenate([-lmat, eye], axis=1)

        for _ in range(6):
            for h in heads:
                d = st[h]
                wmat = d["wmat"]
                r = _dot(wmat[:, 0:C].astype(BF16), wmat.astype(BF16))
                d["wmat"] = r + jnp.where(right_half, wmat, 0.0)

        for h in heads:
            d = st[h]
            tinv = jnp.where(right_half, d["wmat"], 0.0).astype(BF16)
            d["sol"] = _dot(tinv, jnp.concatenate([d["rhs"], d["rhs"]], axis=0))

        for h in heads:
            d = st[h]
            w = d["sol"][:, HEAD_DIM:2 * HEAD_DIM]
            d["ws_qs"] = _dot(jnp.concatenate([w, d["q_dec"]], axis=0).astype(BF16), s_ref[h].astype(BF16))

        for h in heads:
            d = st[h]
            vnb = (d["sol"][:, 0:HEAD_DIM] - d["ws_qs"][0:C]).astype(BF16)
            d["o"] = d["ws_qs"][C:2 * C] + _dot(d["attn"], vnb)
            s_ref[h] = s_ref[h] * d["s_scale"] + _dot_tn(d["k_dec"], vnb)

        osq = _lane_sums([st[h]["o"] * st[h]["o"] for h in heads])
        for n, h in enumerate(heads):
            o = st[h]["o"]
            sl = slice(h * HEAD_DIM, (h + 1) * HEAD_DIM)
            o = o * lax.rsqrt(osq[n] * (1.0 / HEAD_DIM) + RMS_EPS) * gn
            z = jnp.where(c == 0, zm_ref[:, sl], zx_ref[:, sl])
            o_ref[:, sl] = (o * _silu(z)).astype(o_ref.dtype)


def _gdn(qkv, qkv_meta, z, z_meta, bg_all, gt3, gn_row, n_heads):
    t = qkv.shape[0]
    gw = n_heads * HEAD_DIM
    n_chunks = t // CHUNK + 1
    xmap = lambda c: (jnp.maximum(c - 1, 0), 0)
    fixed = lambda c: (0, 0)
    return pl.pallas_call(
        functools.partial(_gdn_kernel, n_heads=n_heads),
        grid=(n_chunks,),
        in_specs=[pl.BlockSpec((CHUNK, 3 * gw), xmap),
                  pl.BlockSpec((CHUNK, 3 * gw), fixed),
                  pl.BlockSpec((CHUNK, gw), xmap),
                  pl.BlockSpec((CHUNK, gw), fixed),
                  pl.BlockSpec((CHUNK, 2 * n_heads), lambda c: (c, 0)),
                  pl.BlockSpec((1, n_heads, CHUNK), lambda c: (c, 0, 0)),
                  pl.BlockSpec((1, HEAD_DIM), fixed)],
        out_specs=pl.BlockSpec((CHUNK, gw), xmap),
        out_shape=jax.ShapeDtypeStruct((t, gw), BF16),
        scratch_shapes=[pltpu.VMEM((n_heads, HEAD_DIM, HEAD_DIM), F32)],
        compiler_params=_params(1),
        name="gated_delta_rule",
    )(qkv, qkv_meta, z, z_meta, bg_all, gt3, gn_row)


def _router_kernel(r_ref, g_ref, b_ref, rwh_ref, rwl_ref, bias_ref, h1_ref, h1b_ref, h1p_ref,
                   eidx_ref, wsel_ref, pos_ref, cnt_ref, carry_ref, *, tm, n_exp):
    i = pl.program_id(0)
    per = n_exp // N_EXPERT_GROUPS

    @pl.when(i == 0)
    def _():
        carry_ref[...] = jnp.zeros(carry_ref.shape, F32)

    h1 = _ln_rows(r_ref[...], g_ref[...], b_ref[...])
    h1_ref[...] = h1
    hb = h1.astype(BF16)
    h1b_ref[...] = hb
    h1p_ref[...] = _pack_rows(h1)

    hl = (h1 - hb.astype(F32)).astype(BF16)
    logits = _dot_nt(rwh_ref[...], hb) + (_dot_nt(rwl_ref[...], hb) + _dot_nt(rwh_ref[...], hl))
    scores = _sigmoid(logits)
    biased = scores + bias_ref[...]
    neg_inf = F32(-jnp.inf)

    gs_rows = []
    sub = lax.broadcasted_iota(I32, (per, tm), 0)
    for g in range(N_EXPERT_GROUPS):
        xg = biased[g * per:(g + 1) * per, :]
        m1 = jnp.max(xg, axis=0, keepdims=True)
        first = jnp.min(jnp.where(xg == m1, sub, per), axis=0, keepdims=True)
        m2 = jnp.max(jnp.where(sub == first, neg_inf, xg), axis=0, keepdims=True)
        gs_rows.append(m1 + m2)
    gs = jnp.concatenate(gs_rows, axis=0)
    gid = lax.broadcasted_iota(I32, (N_EXPERT_GROUPS, tm), 0)
    grank = jnp.zeros((N_EXPERT_GROUPS, tm), F32)
    for g in range(N_EXPERT_GROUPS):
        rowv = gs[g:g + 1, :]
        grank = grank + jnp.where(rowv > gs, 1.0, jnp.where((rowv == gs) & (gid > g), 1.0, 0.0))
    gsel = jnp.where(grank < TOPK_GROUPS, 1.0, 0.0)
    emask = jnp.concatenate(
        [jnp.broadcast_to(gsel[g:g + 1, :], (per, tm)) for g in range(N_EXPERT_GROUPS)], axis=0)
    masked = jnp.where(emask > 0.0, biased, neg_inf)
    eid = lax.broadcasted_iota(I32, (n_exp, tm), 0)
    erank = jnp.zeros((n_exp, tm), F32)
    for e in range(n_exp):
        rowv = masked[e:e + 1, :]
        erank = erank + jnp.where(rowv > masked, 1.0, jnp.where((rowv == masked) & (eid > e), 1.0, 0.0))
    sel = jnp.where(erank < TOP_K, emask, 0.0)
    wdense = scores * sel
    gate = wdense / jnp.sum(wdense, axis=0, keepdims=True) * ROUTED_SCALE

    er = lax.broadcasted_iota(I32, (n_exp, n_exp), 0)
    ec = lax.broadcasted_iota(I32, (n_exp, n_exp), 1)
    lower = jnp.where(er > ec, 1.0, 0.0).astype(BF16)
    selb = sel.astype(BF16)
    slot = _dot(lower, selb)
    tr = lax.broadcasted_iota(I32, (tm, tm), 0)
    tc = lax.broadcasted_iota(I32, (tm, tm), 1)
    upper = jnp.where(tr < tc, 1.0, 0.0).astype(BF16)
    carry = carry_ref[:, 0:1]
    pos = _dot(selb, upper) + carry
    eid_f = eid.astype(F32)
    e_rows, w_rows, p_rows = [], [], []
    for k in range(TOP_K):
        mk = jnp.where(slot == float(k), sel, 0.0)
        e_rows.append(jnp.sum(mk * eid_f, axis=0, keepdims=True))
        w_rows.append(jnp.sum(mk * gate, axis=0, keepdims=True))
        p_rows.append(jnp.sum(mk * pos, axis=0, keepdims=True))
    eidx_ref[...] = jnp.concatenate(e_rows, axis=0).astype(I32)
    wsel_ref[...] = jnp.concatenate(w_rows, axis=0)
    pos_ref[...] = jnp.concatenate(p_rows, axis=0).astype(I32)
    new_carry = carry + jnp.sum(sel, axis=1, keepdims=True)
    carry_ref[...] = jnp.broadcast_to(new_carry, carry_ref.shape)
    cnt_ref[...] = jnp.broadcast_to(new_carry, cnt_ref.shape).astype(I32)


def _router(r, g, b, rw_hi, rw_lo, bias_col, tm):
    t, d = r.shape
    n_exp = rw_hi.shape[0]
    row = lambda i: (i, 0)
    colb = lambda i: (0, i)
    fixed = lambda i: (0, 0)
    return pl.pallas_call(
        functools.partial(_router_kernel, tm=tm, n_exp=n_exp),
        grid=(t // tm,),
        in_specs=[pl.BlockSpec((tm, d), row),
                  pl.BlockSpec((1, d), fixed),
                  pl.BlockSpec((1, d), fixed),
                  pl.BlockSpec((n_exp, d), fixed),
                  pl.BlockSpec((n_exp, d), fixed),
                  pl.BlockSpec((n_exp, 1), fixed)],
        out_specs=[pl.BlockSpec((tm, d), row),
                   pl.BlockSpec((tm, d), row),
                   pl.BlockSpec((tm, d // 2), row),
                   pl.BlockSpec((TOP_K, tm), colb),
                   pl.BlockSpec((TOP_K, tm), colb),
                   pl.BlockSpec((TOP_K, tm), colb),
                   pl.BlockSpec((n_exp, 128), fixed)],
        out_shape=[jax.ShapeDtypeStruct((t, d), F32),
                   jax.ShapeDtypeStruct((t, d), BF16),
                   jax.ShapeDtypeStruct((t, d // 2), U32),
                   jax.ShapeDtypeStruct((TOP_K, t), I32),
                   jax.ShapeDtypeStruct((TOP_K, t), F32),
                   jax.ShapeDtypeStruct((TOP_K, t), I32),
                   jax.ShapeDtypeStruct((n_exp, 128), I32)],
        scratch_shapes=[pltpu.VMEM((n_exp, 128), F32)],
        compiler_params=_params(1),
        name="ln1_router",
    )(r, g, b, rw_hi, rw_lo, bias_col)


def _dest_kernel(gstart_ref, eidx_ref, pos_ref, dest_ref, *, n_exp):
    eidx = eidx_ref[...]
    acc = pos_ref[...]
    for e in range(n_exp):
        acc = acc + jnp.where(eidx == e, gstart_ref[e], 0)
    dest_ref[...] = acc


def _dest_rows(gstart, eidx, pos):
    n_exp = gstart.shape[0]
    return pl.pallas_call(
        functools.partial(_dest_kernel, n_exp=n_exp),
        grid_spec=pltpu.PrefetchScalarGridSpec(
            num_scalar_prefetch=1,
            grid=(1,),
            in_specs=[pl.BlockSpec(eidx.shape, lambda i, gs: (0, 0)),
                      pl.BlockSpec(pos.shape, lambda i, gs: (0, 0))],
            out_specs=pl.BlockSpec(eidx.shape, lambda i, gs: (0, 0))),
        out_shape=jax.ShapeDtypeStruct(eidx.shape, I32),
        compiler_params=_params(1),
        name="dest_rows",
    )(gstart, eidx, pos)


def _dispatch_kernel(dest_ref, hp_ref, xs_ref, sem, *, tm):
    def issue(t, carry):
        for k in range(TOP_K):
            d = dest_ref[0, 0, t * TOP_K + k]
            pltpu.make_async_copy(hp_ref.at[pl.ds(t, 1)], xs_ref.at[pl.ds(d, 1)], sem).start(priority=k % 2)
        return carry
    lax.fori_loop(0, tm, issue, 0)
    pltpu.make_async_copy(xs_ref.at[pl.ds(0, tm * TOP_K)], xs_ref.at[pl.ds(0, tm * TOP_K)], sem).wait()


def _dispatch(dest_tiles, h1p, tm):
    t, dw = h1p.shape
    return pl.pallas_call(
        functools.partial(_dispatch_kernel, tm=tm),
        grid=(t // tm,),
        in_specs=[pl.BlockSpec((1, 1, tm * TOP_K), lambda i: (i, 0, 0), memory_space=pltpu.SMEM),
                  pl.BlockSpec((tm, dw), lambda i: (i, 0))],
        out_specs=pl.BlockSpec(memory_space=pl.ANY),
        out_shape=jax.ShapeDtypeStruct((t * TOP_K, dw), U32),
        scratch_shapes=[pltpu.SemaphoreType.DMA(())],
        compiler_params=_params(1),
        name="dispatch_rows",
    )(dest_tiles, h1p)


def _pack_rows(y):
    n = y.shape[1] // 2
    yb = y.astype(BF16)
    hi = lax.bitcast_convert_type(yb[:, 0:n].astype(F32), U32)
    lo = lax.bitcast_convert_type(yb[:, n:].astype(F32), U32)
    return hi | (lo >> 16)


def _unpack_rows_f32(xu):
    hi = lax.bitcast_convert_type(xu & jnp.uint32(0xFFFF0000), F32)
    lo = lax.bitcast_convert_type(xu << 16, F32)
    return hi, lo


def _unpack_rows(xu):
    hi, lo = _unpack_rows_f32(xu)
    return hi.astype(BF16), lo.astype(BF16)


def _expert_weights(v, ve_ref, vfe_ref, eo_ref, vne_ref, hbm_refs, f32_refs, bf16_refs, sems):
    @pl.when(vfe_ref[v] == 1)
    def _():
        slot = eo_ref[v] % 2
        e = ve_ref[v]

        def copies(expert, s):
            return [pltpu.make_async_copy(h.at[expert], f.at[s], sem.at[s])
                    for h, f, sem in zip(hbm_refs, f32_refs, sems)]

        @pl.when(v == 0)
        def _():
            for cp in copies(e, slot):
                cp.start()

        for cp in copies(e, slot):
            cp.wait()
        ne = vne_ref[v]

        @pl.when(ne >= 0)
        def _():
            for cp in copies(ne, 1 - slot):
                cp.start()

        for f, bf in zip(f32_refs, bf16_refs):
            bf[...] = f[slot].astype(BF16)


def _visit_subblocks(vt_ref, ve_ref, vft_ref, gs_ref, ge_ref, v, tm, sb, o_ref, compute):
    e = ve_ref[v]
    gs, ge = gs_ref[e], ge_ref[e]
    first = vft_ref[v] == 1
    width = o_ref.shape[1]
    t0 = vt_ref[v] * tm
    whole = (t0 >= gs) & (t0 + tm <= ge)

    @pl.when(whole)
    def _():
        o_ref[...] = compute(0, tm).astype(o_ref.dtype)

    for s in range(tm // sb):
        r0 = t0 + s * sb
        hit = (r0 < ge) & (r0 + sb > gs)
        rows = pl.ds(s * sb, sb)

        @pl.when(hit & jnp.logical_not(whole))
        def _():
            y = compute(s * sb, sb).astype(o_ref.dtype)
            owned = (r0 >= gs) & (r0 + sb <= ge)

            @pl.when(owned)
            def _():
                o_ref[rows, :] = y

            @pl.when(jnp.logical_not(owned))
            def _():
                ridx = r0 + lax.broadcasted_iota(I32, (sb, width), 0)
                mask = (ridx >= gs) & (ridx < ge)

                @pl.when(first)
                def _():
                    o_ref[rows, :] = jnp.where(mask, y, jnp.zeros_like(y))

                @pl.when(jnp.logical_not(first))
                def _():
                    o_ref[rows, :] = jnp.where(mask, y, o_ref[rows, :])

        @pl.when(jnp.logical_not(hit) & first)
        def _():
            o_ref[rows, :] = jnp.zeros((sb, width), o_ref.dtype)


def _gmm_up_kernel(vt_ref, ve_ref, vft_ref, vfe_ref, eo_ref, vne_ref, gs_ref, ge_ref, nv_ref,
                   xs_ref, wg_hbm, wu_hbm, o_ref, wgf_ref, wuf_ref, wgb_ref, wub_ref, semg, semu, *, tm, sb):
    v = pl.program_id(0)

    @pl.when(v < nv_ref[0])
    def _():
        _expert_weights(v, ve_ref, vfe_ref, eo_ref, vne_ref, (wg_hbm, wu_hbm), (wgf_ref, wuf_ref),
                        (wgb_ref, wub_ref), (semg, semu))

        def compute(start, size):
            hi, lo = _unpack_rows(xs_ref[start:start + size, :])
            kh = hi.shape[1]
            hg = _dot(hi, wgb_ref[0:kh, :]) + _dot(lo, wgb_ref[kh:, :])
            hu = _dot(hi, wub_ref[0:kh, :]) + _dot(lo, wub_ref[kh:, :])
            return _silu(hg) * hu

        _visit_subblocks(vt_ref, ve_ref, vft_ref, gs_ref, ge_ref, v, tm, sb, o_ref, compute)


def _gmm_down_kernel(vt_ref, ve_ref, vft_ref, vfe_ref, eo_ref, vne_ref, gs_ref, ge_ref, nv_ref,
                     a_ref, wd_hbm, o_ref, wdf_ref, wdb_ref, semd, *, tm, sb):
    v = pl.program_id(0)

    @pl.when(v < nv_ref[0])
    def _():
        _expert_weights(v, ve_ref, vfe_ref, eo_ref, vne_ref, (wd_hbm,), (wdf_ref,), (wdb_ref,), (semd,))

        def compute(start, size):
            return _pack_rows(_dot(a_ref[start:start + size, :], wdb_ref[...]))

        _visit_subblocks(vt_ref, ve_ref, vft_ref, gs_ref, ge_ref, v, tm, sb, o_ref, compute)


def _visit_schedule(counts, n_rows, tm):
    n_exp = counts.shape[0]
    n_tiles = n_rows // tm
    n_visits = n_tiles + n_exp - 1
    ends = jnp.cumsum(counts)
    starts = ends - counts
    first_tile = starts // tm
    last_tile = jnp.maximum(ends - 1, 0) // tm
    tiles_e = jnp.where(counts > 0, last_tile - first_tile + 1, 0)
    vend = jnp.cumsum(tiles_e)
    vstart = vend - tiles_e
    total = vend[-1]
    v = jnp.arange(n_visits, dtype=I32)
    vc = jnp.minimum(v, total - 1)
    ve = jnp.sum(vend[None, :] <= vc[:, None], axis=1).astype(I32)
    vt = (vc - vstart[ve] + first_tile[ve]).astype(I32)
    prev_t = jnp.concatenate([jnp.full((1,), -1, I32), vt[:-1]])
    prev_e = jnp.concatenate([jnp.full((1,), -1, I32), ve[:-1]])
    vft = (vt != prev_t).astype(I32)
    vfe = (ve != prev_e).astype(I32)
    eo = (jnp.cumsum(vfe) - 1).astype(I32)
    eidx = jnp.arange(n_exp, dtype=I32)
    later = lax.cummin(jnp.where(counts > 0, eidx, n_exp), axis=0, reverse=True)
    nxt = jnp.concatenate([later[1:], jnp.full((1,), n_exp, I32)])
    vne = jnp.where(nxt < n_exp, nxt, -1)[ve].astype(I32)
    sched = (vt, ve, vft, vfe, eo, vne, starts.astype(I32), ends.astype(I32), total.reshape(1).astype(I32))
    return sched, n_visits


def _gmm_up(sched, n_visits, xs, wg, wu, tm, sb):
    n, dw = xs.shape
    n_exp, d, f = wg.shape
    return pl.pallas_call(
        functools.partial(_gmm_up_kernel, tm=tm, sb=sb),
        grid_spec=pltpu.PrefetchScalarGridSpec(
            num_scalar_prefetch=len(sched),
            grid=(n_visits,),
            in_specs=[pl.BlockSpec((tm, dw), lambda v, vt, *_: (vt[v], 0)),
                      pl.BlockSpec(memory_space=pl.ANY),
                      pl.BlockSpec(memory_space=pl.ANY)],
            out_specs=pl.BlockSpec((tm, f), lambda v, vt, *_: (vt[v], 0)),
            scratch_shapes=[pltpu.VMEM((2, d, f), F32), pltpu.VMEM((2, d, f), F32),
                            pltpu.VMEM((d, f), BF16), pltpu.VMEM((d, f), BF16),
                            pltpu.SemaphoreType.DMA((2,)), pltpu.SemaphoreType.DMA((2,))]),
        out_shape=jax.ShapeDtypeStruct((n, f), BF16),
        compiler_params=_params(1),
        name="experts_gate_up",
    )(*sched, xs, wg, wu)


def _gmm_down(sched, n_visits, act, wd, tm, sb):
    n, f = act.shape
    n_exp, _, d = wd.shape
    return pl.pallas_call(
        functools.partial(_gmm_down_kernel, tm=tm, sb=sb),
        grid_spec=pltpu.PrefetchScalarGridSpec(
            num_scalar_prefetch=len(sched),
            grid=(n_visits,),
            in_specs=[pl.BlockSpec((tm, f), lambda v, vt, *_: (vt[v], 0)),
                      pl.BlockSpec(memory_space=pl.ANY)],
            out_specs=pl.BlockSpec((tm, d // 2), lambda v, vt, *_: (vt[v], 0)),
            scratch_shapes=[pltpu.VMEM((2, f, d), F32), pltpu.VMEM((f, d), BF16),
                            pltpu.SemaphoreType.DMA((2,))]),
        out_shape=jax.ShapeDtypeStruct((n, d // 2), U32),
        compiler_params=_params(1),
        name="experts_down",
    )(*sched, act, wd)


def _combine_kernel(dcur_ref, dnxt_ref, w_ref, h1_ref, as_ref, wsd_ref, g_ref, b_ref, ys_ref, o_ref,
                    gbuf0_ref, gbuf1_ref, ybuf_ref, sem, *, tm, alpha):
    i = pl.program_id(0)
    n = pl.num_programs(0)
    groups = tm // SUBLANES
    half = o_ref.shape[1] // 2

    def issue_token(dref, buf_ref, s, t):
        for k in range(TOP_K):
            src = dref[0, 0, t * TOP_K + k]
            pltpu.make_async_copy(ys_ref.at[pl.ds(src, 1)], buf_ref.at[k, pl.ds(t, 1)],
                                  sem.at[s]).start(priority=k % 2)

    def wait_buf(buf_ref, s):
        for k in range(TOP_K):
            pltpu.make_async_copy(ys_ref.at[pl.ds(0, tm)], buf_ref.at[k], sem.at[s]).wait()

    @pl.when(i == 0)
    def _():
        def first(t, carry):
            issue_token(dcur_ref, gbuf0_ref, 0, t)
            return carry
        lax.fori_loop(0, tm, first, 0)

    ybuf_ref[...] = alpha * h1_ref[...] + _dot(as_ref[...], wsd_ref[...])

    def reduce_tile(cur_ref, cur_s, nxt_ref, nxt_s):
        wait_buf(cur_ref, cur_s)

        def reduce_group(r, carry):
            r8 = pl.multiple_of(r * SUBLANES, SUBLANES)
            rows = pl.ds(r8, SUBLANES)
            y_hi = ybuf_ref[rows, 0:half]
            y_lo = ybuf_ref[rows, half:]
            w = w_ref[rows, :]
            for j in range(SUBLANES):
                issue_token(dnxt_ref, nxt_ref, nxt_s, r8 + j)
                for k in range(j * TOP_K // SUBLANES, (j + 1) * TOP_K // SUBLANES):
                    e_hi, e_lo = _unpack_rows_f32(cur_ref[k, rows, :])
                    wk = w[:, k:k + 1]
                    y_hi = y_hi + wk * e_hi
                    y_lo = y_lo + wk * e_lo
            ybuf_ref[rows, 0:half] = y_hi
            ybuf_ref[rows, half:] = y_lo
            return carry
        lax.fori_loop(0, groups, reduce_group, 0)

        @pl.when(i == n - 1)
        def _():
            wait_buf(nxt_ref, nxt_s)

    @pl.when(i % 2 == 0)
    def _():
        reduce_tile(gbuf0_ref, 0, gbuf1_ref, 1)

    @pl.when(i % 2 == 1)
    def _():
        reduce_tile(gbuf1_ref, 1, gbuf0_ref, 0)

    o_ref[...] = _ln_rows(ybuf_ref[...], g_ref[...], b_ref[...])


def _combine(dest_tiles, w_tok, h1, act_s, wsd, g, b, ys, tm, alpha):
    t, d = h1.shape
    f = act_s.shape[1]
    n = t // tm
    row = lambda i: (i, 0)
    fixed = lambda i: (0, 0)
    return pl.pallas_call(
        functools.partial(_combine_kernel, tm=tm, alpha=alpha),
        grid=(n,),
        in_specs=[pl.BlockSpec((1, 1, tm * TOP_K), lambda i: (i, 0, 0), memory_space=pltpu.SMEM),
                  pl.BlockSpec((1, 1, tm * TOP_K), lambda i: (jnp.minimum(i + 1, n - 1), 0, 0),
                               memory_space=pltpu.SMEM),
                  pl.BlockSpec((tm, TOP_K), row),
                  pl.BlockSpec((tm, d), row),
                  pl.BlockSpec((tm, f), row),
                  pl.BlockSpec((f, d), fixed),
                  pl.BlockSpec((1, d), fixed),
                  pl.BlockSpec((1, d), fixed),
                  pl.BlockSpec(memory_space=pl.ANY)],
        out_specs=pl.BlockSpec((tm, d), row),
        out_shape=jax.ShapeDtypeStruct((t, d), F32),
        scratch_shapes=[pltpu.VMEM((TOP_K, tm, d // 2), U32),
                        pltpu.VMEM((TOP_K, tm, d // 2), U32),
                        pltpu.VMEM((tm, d), F32),
                        pltpu.SemaphoreType.DMA((2,))],
        compiler_params=_params(1),
        name="combine_ln2",
    )(dest_tiles, dest_tiles, w_tok, h1, act_s, wsd, g, b, ys)


def _tile(n, pref):
    t = min(n, pref)
    assert n % t == 0, (n, pref)
    return t


def kernel(x, meta_tokens, emb_ln_g, emb_ln_b, w_in, conv_w, conv_b, conv_ln_g, conv_ln_b, short_conv_w, a_log, dt_bias, gdn_norm_g, w_out, ln1_g, ln1_b, router_w, router_bias, expert_w_gate, expert_w_up, expert_w_down, shared_w_gate, shared_w_up, shared_w_down, ln2_g, ln2_b):
    depth = w_in.shape[0]
    assert depth == 1 and x.shape[0] == 1, "single layer, single sequence"
    t, d = x.shape[1], x.shape[2]
    cw = conv_w.shape[2]
    n_heads = a_log.shape[1]
    gw = n_heads * HEAD_DIM
    alpha = (2.0 * depth) ** 0.25
    assert meta_tokens.shape[0] == N_META and t % CHUNK == 0 and w_in.shape[2] == 2 * cw + 4 * gw + 2 * n_heads
    row = lambda a: a.reshape(1, -1)

    h_b, h_stats = _embed_ln(x[0], row(emb_ln_g), row(emb_ln_b), _tile(t, 256), "embed_ln")
    hm_b, _ = _embed_ln(meta_tokens, row(emb_ln_g), row(emb_ln_b), N_META, "embed_ln_meta")

    w_t = jnp.transpose(w_in[0])
    tm = _tile(t, 1024)
    zeros_h = jnp.zeros((n_heads,), F32)
    alog_row = row(jnp.concatenate([zeros_h, a_log[0]]))
    dtb_row = row(jnp.concatenate([zeros_h, dt_bias[0]]))
    c_x, c_m = _inproj_glu(h_b, hm_b, w_t, 0, cw, cw, tm, _tile(cw, 256))
    qkv_x, qkv_m = _inproj_conv(h_b, hm_b, w_t, short_conv_w[0], 2 * cw, 3 * gw, tm, _tile(3 * gw, 512))
    z_x, z_m = _inproj_plain(h_b, hm_b, w_t, 2 * cw + 3 * gw, gw, tm, _tile(gw, 512))
    bg_x, bg_m = _inproj_decay(h_b, hm_b, w_t[2 * cw + 4 * gw:], alog_row, dtb_row, tm, n_heads)

    y_conv = _conv_module(c_x, c_m, conv_w[0], row(conv_b[0]), row(conv_ln_g[0]), row(conv_ln_b[0]),
                          _tile(t, 256))

    front = lambda a: jnp.pad(a, ((CHUNK - N_META, 0), (0, 0)))
    bg_all = jnp.concatenate([front(bg_m), bg_x], axis=0)
    gt3 = bg_all[:, n_heads:].reshape(-1, CHUNK, n_heads).transpose(0, 2, 1)
    y_gdn = _gdn(qkv_x, front(qkv_m), z_x, front(z_m), bg_all, gt3, row(gdn_norm_g[0]), n_heads)

    r1 = _proj_residual(y_conv, y_gdn, w_out[0], x[0], h_stats, row(emb_ln_g), row(emb_ln_b), alpha, tm,
                        _tile(d, 512))
    rwt = router_w[0].T
    rw_hi = rwt.astype(BF16)
    rw_lo = (rwt - rw_hi.astype(F32)).astype(BF16)
    h1, h1_b, h1_p, eidx, wsel, pos, cnt = _router(
        r1, row(ln1_g[0]), row(ln1_b[0]), rw_hi, rw_lo, router_bias[0].reshape(-1, 1), _tile(t, 256))
    counts = cnt[:, 0]
    gstart = jnp.cumsum(counts) - counts
    dest = _dest_rows(gstart.astype(I32), eidx, pos)

    tmd = _tile(t, 256)
    xs = _dispatch(dest.T.reshape(t // tmd, 1, tmd * TOP_K), h1_p, tmd)
    tmg = _tile(t * TOP_K, 512)
    sbg = _tile(tmg, 128)
    sched, n_visits = _visit_schedule(counts, t * TOP_K, tmg)
    act = _gmm_up(sched, n_visits, xs, expert_w_gate[0], expert_w_up[0], tmg, sbg)
    ys = _gmm_down(sched, n_visits, act, expert_w_down[0], tmg, _tile(tmg, 256))

    act_s = _swiglu(h1_b, shared_w_gate[0], shared_w_up[0], tm, _tile(shared_w_gate.shape[2], 256), BF16)
    tmc = _tile(t, 128)
    out = _combine(dest.T.reshape(t // tmc, 1, tmc * TOP_K), wsel.T, h1, act_s, shared_w_down[0].astype(BF16),
                   row(ln2_g[0]), row(ln2_b[0]), ys, tmc, alpha)
    return out[None]
```

```python
import functools

import jax
import jax.numpy as jnp
from jax import lax
from jax.experimental import pallas as pl
from jax.experimental.pallas import tpu as pltpu

F32 = jnp.float32
BF16 = jnp.bfloat16
I32 = jnp.int32
U32 = jnp.uint32

N_META = 16
CHUNK = 64
HEAD_DIM = 128
HEAD_GROUP = 16
TOP_K = 8
N_EXPERT_GROUPS = 8
TOPK_GROUPS = 4
ROUTED_SCALE = 2.5
LN_EPS = 1e-5
RMS_EPS = 1e-6
CONV_HALO = 32
CONV_ROWS = 32
CONV_LANES = 512
SUBLANES = 8
LANES = 128
VMEM_LIMIT = 56 * 1024 * 1024


def _params(n_axes, vmem=VMEM_LIMIT):
    return pltpu.CompilerParams(dimension_semantics=("arbitrary",) * n_axes, vmem_limit_bytes=vmem)


def _ln_rows(x, g, b):
    mu = jnp.mean(x, -1, keepdims=True)
    xc = x - mu
    var = jnp.mean(xc * xc, -1, keepdims=True)
    return xc * lax.rsqrt(var + LN_EPS) * g + b


def _sigmoid(x):
    return 1.0 / (1.0 + jnp.exp(-x))


def _silu(x):
    return x * _sigmoid(x)


def _dot(a, b):
    return jnp.dot(a, b, preferred_element_type=F32)


def _dot_nt(a, b, precision=None):
    return lax.dot_general(a, b, (((1,), (1,)), ((), ())), precision=precision, preferred_element_type=F32)


def _dot_tn(a, b):
    return lax.dot_general(a, b, (((0,), (0,)), ((), ())), preferred_element_type=F32)


def _embed_ln_kernel(x_ref, g_ref, b_ref, hb_ref, st_ref):
    x = x_ref[...]
    mu = jnp.mean(x, -1, keepdims=True)
    xc = x - mu
    rs = lax.rsqrt(jnp.mean(xc * xc, -1, keepdims=True) + LN_EPS)
    hb_ref[...] = (xc * rs * g_ref[...] + b_ref[...]).astype(BF16)
    lane = lax.broadcasted_iota(I32, st_ref.shape, 1)
    st_ref[...] = jnp.where(lane == 0, mu, jnp.where(lane == 1, rs, 0.0))


def _embed_ln(x, g, b, tm, name):
    m, d = x.shape
    return pl.pallas_call(
        _embed_ln_kernel,
        grid=(m // tm,),
        in_specs=[pl.BlockSpec((tm, d), lambda i: (i, 0)),
                  pl.BlockSpec((1, d), lambda i: (0, 0)),
                  pl.BlockSpec((1, d), lambda i: (0, 0))],
        out_specs=[pl.BlockSpec((tm, d), lambda i: (i, 0)),
                   pl.BlockSpec((tm, LANES), lambda i: (i, 0))],
        out_shape=[jax.ShapeDtypeStruct((m, d), BF16), jax.ShapeDtypeStruct((m, LANES), F32)],
        compiler_params=_params(1),
        name=name,
    )(x, g, b)


def _inproj_plain_kernel(a_ref, am_ref, w_ref, o_ref, om_ref, wb_ref):
    @pl.when(pl.program_id(1) == 0)
    def _():
        wb_ref[...] = w_ref[...].astype(BF16)
        om_ref[...] = _dot_nt(am_ref[...], wb_ref[...])
    o_ref[...] = _dot_nt(a_ref[...], wb_ref[...])


def _inproj_conv_kernel(a_ref, am_ref, w_ref, cw_ref, o_ref, om_ref, wb_ref, hbuf_ref):
    i = pl.program_id(1)
    tm = a_ref.shape[0]
    kw = cw_ref.shape[0]

    def conv_silu(rows):
        acc = cw_ref[kw - 1:kw, :] * hbuf_ref[SUBLANES:SUBLANES + rows, :]
        for j in range(kw - 1):
            off = SUBLANES - (kw - 1) + j
            acc = acc + cw_ref[j:j + 1, :] * hbuf_ref[off:off + rows, :]
        return _silu(acc)

    @pl.when(i == 0)
    def _():
        wb_ref[...] = w_ref[...].astype(BF16)
        hbuf_ref[0:SUBLANES, :] = jnp.zeros((SUBLANES, hbuf_ref.shape[1]), F32)
        hbuf_ref[SUBLANES:SUBLANES + N_META, :] = _dot_nt(am_ref[...], wb_ref[...])
        om_ref[...] = conv_silu(N_META)
        hbuf_ref[0:SUBLANES, :] = hbuf_ref[N_META:N_META + SUBLANES, :]

    @pl.when(i > 0)
    def _():
        hbuf_ref[0:SUBLANES, :] = hbuf_ref[tm:tm + SUBLANES, :]

    hbuf_ref[SUBLANES:SUBLANES + tm, :] = _dot_nt(a_ref[...], wb_ref[...])
    o_ref[...] = conv_silu(tm)


def _inproj_glu_kernel(a_ref, am_ref, w1_ref, w2_ref, o_ref, om_ref, w1b_ref, w2b_ref):
    @pl.when(pl.program_id(1) == 0)
    def _():
        w1b_ref[...] = w1_ref[...].astype(BF16)
        w2b_ref[...] = w2_ref[...].astype(BF16)
        am = am_ref[...]
        om_ref[...] = _dot_nt(am, w1b_ref[...]) * _sigmoid(_dot_nt(am, w2b_ref[...]))
    a = a_ref[...]
    o_ref[...] = _dot_nt(a, w1b_ref[...]) * _sigmoid(_dot_nt(a, w2b_ref[...]))


def _decay_epilogue(p, alog, dtb, n_heads):
    x = p + dtb
    softplus = jnp.maximum(x, 0.0) + jnp.log1p(jnp.exp(-jnp.abs(x)))
    g = -jnp.exp(alog) * softplus
    lane = lax.broadcasted_iota(I32, p.shape, 1)
    return jnp.where(lane < n_heads, _sigmoid(p), g)


def _inproj_decay_kernel(a_ref, am_ref, w_ref, alog_ref, dtb_ref, o_ref, om_ref, wb_ref, *, n_heads):
    @pl.when(pl.program_id(1) == 0)
    def _():
        wb_ref[...] = w_ref[...].astype(BF16)
        om_ref[...] = _decay_epilogue(_dot_nt(am_ref[...], wb_ref[...]), alog_ref[...], dtb_ref[...], n_heads)
    o_ref[...] = _decay_epilogue(_dot_nt(a_ref[...], wb_ref[...]), alog_ref[...], dtb_ref[...], n_heads)


def _swiglu_kernel(a_ref, w1_ref, w2_ref, o_ref, w1b_ref, w2b_ref):
    @pl.when(pl.program_id(1) == 0)
    def _():
        w1b_ref[...] = w1_ref[...].astype(BF16)
        w2b_ref[...] = w2_ref[...].astype(BF16)
    a = a_ref[...]
    o_ref[...] = (_silu(_dot(a, w1b_ref[...])) * _dot(a, w2b_ref[...])).astype(o_ref.dtype)


def _proj_residual_kernel(a1_ref, a2_ref, w_ref, x_ref, st_ref, g_ref, b_ref, o_ref, wb_ref, *, alpha, k1):
    @pl.when(pl.program_id(1) == 0)
    def _():
        wb_ref[...] = w_ref[...].astype(BF16)
    acc = _dot(a1_ref[...], wb_ref[0:k1, :]) + _dot(a2_ref[...], wb_ref[k1:, :])
    st = st_ref[...]
    resid = (x_ref[...] - st[:, 0:1]) * st[:, 1:2] * g_ref[...] + b_ref[...]
    o_ref[...] = alpha * resid + acc


def _inproj_plain(a, am, w_t, row0, n, tm, tn):
    m, k = a.shape
    mm = am.shape[0]
    rb = row0 // tn
    return pl.pallas_call(
        _inproj_plain_kernel,
        grid=(n // tn, m // tm),
        in_specs=[pl.BlockSpec((tm, k), lambda j, i: (i, 0)),
                  pl.BlockSpec((mm, k), lambda j, i: (0, 0)),
                  pl.BlockSpec((tn, k), lambda j, i: (j + rb, 0))],
        out_specs=[pl.BlockSpec((tm, tn), lambda j, i: (i, j)),
                   pl.BlockSpec((mm, tn), lambda j, i: (0, j))],
        out_shape=[jax.ShapeDtypeStruct((m, n), F32), jax.ShapeDtypeStruct((mm, n), F32)],
        scratch_shapes=[pltpu.VMEM((tn, k), BF16)],
        compiler_params=_params(2),
        name="inproj_z",
    )(a, am, w_t)


def _inproj_conv(a, am, w_t, conv_w, row0, n, tm, tn):
    m, k = a.shape
    mm = am.shape[0]
    kw = conv_w.shape[0]
    assert kw <= SUBLANES and mm == N_META and N_META % SUBLANES == 0
    rb = row0 // tn
    return pl.pallas_call(
        _inproj_conv_kernel,
        grid=(n // tn, m // tm),
        in_specs=[pl.BlockSpec((tm, k), lambda j, i: (i, 0)),
                  pl.BlockSpec((mm, k), lambda j, i: (0, 0)),
                  pl.BlockSpec((tn, k), lambda j, i: (j + rb, 0)),
                  pl.BlockSpec((kw, tn), lambda j, i: (0, j))],
        out_specs=[pl.BlockSpec((tm, tn), lambda j, i: (i, j)),
                   pl.BlockSpec((mm, tn), lambda j, i: (0, j))],
        out_shape=[jax.ShapeDtypeStruct((m, n), F32), jax.ShapeDtypeStruct((mm, n), F32)],
        scratch_shapes=[pltpu.VMEM((tn, k), BF16), pltpu.VMEM((SUBLANES + tm, tn), F32)],
        compiler_params=_params(2),
        name="inproj_qkv_conv",
    )(a, am, w_t, conv_w)


def _inproj_glu(a, am, w_t, row1, row2, n, tm, tn):
    m, k = a.shape
    mm = am.shape[0]
    r1, r2 = row1 // tn, row2 // tn
    return pl.pallas_call(
        _inproj_glu_kernel,
        grid=(n // tn, m // tm),
        in_specs=[pl.BlockSpec((tm, k), lambda j, i: (i, 0)),
                  pl.BlockSpec((mm, k), lambda j, i: (0, 0)),
                  pl.BlockSpec((tn, k), lambda j, i: (j + r1, 0)),
                  pl.BlockSpec((tn, k), lambda j, i: (j + r2, 0))],
        out_specs=[pl.BlockSpec((tm, tn), lambda j, i: (i, j)),
                   pl.BlockSpec((mm, tn), lambda j, i: (0, j))],
        out_shape=[jax.ShapeDtypeStruct((m, n), F32), jax.ShapeDtypeStruct((mm, n), F32)],
        scratch_shapes=[pltpu.VMEM((tn, k), BF16), pltpu.VMEM((tn, k), BF16)],
        compiler_params=_params(2),
        name="inproj_glu",
    )(a, am, w_t, w_t)


def _inproj_decay(a, am, w_ba_t, alog_row, dtb_row, tm, n_heads):
    m, k = a.shape
    mm = am.shape[0]
    n = w_ba_t.shape[0]
    return pl.pallas_call(
        functools.partial(_inproj_decay_kernel, n_heads=n_heads),
        grid=(1, m // tm),
        in_specs=[pl.BlockSpec((tm, k), lambda j, i: (i, 0)),
                  pl.BlockSpec((mm, k), lambda j, i: (0, 0)),
                  pl.BlockSpec((n, k), lambda j, i: (0, 0)),
                  pl.BlockSpec((1, n), lambda j, i: (0, 0)),
                  pl.BlockSpec((1, n), lambda j, i: (0, 0))],
        out_specs=[pl.BlockSpec((tm, n), lambda j, i: (i, 0)),
                   pl.BlockSpec((mm, n), lambda j, i: (0, 0))],
        out_shape=[jax.ShapeDtypeStruct((m, n), F32), jax.ShapeDtypeStruct((mm, n), F32)],
        scratch_shapes=[pltpu.VMEM((n, k), BF16)],
        compiler_params=_params(2),
        name="inproj_decay",
    )(a, am, w_ba_t, alog_row, dtb_row)


def _swiglu(a, w1, w2, tm, tn, out_dtype):
    m, k = a.shape
    n = w1.shape[1]
    return pl.pallas_call(
        _swiglu_kernel,
        grid=(n // tn, m // tm),
        in_specs=[pl.BlockSpec((tm, k), lambda j, i: (i, 0)),
                  pl.BlockSpec((k, tn), lambda j, i: (0, j)),
                  pl.BlockSpec((k, tn), lambda j, i: (0, j))],
        out_specs=pl.BlockSpec((tm, tn), lambda j, i: (i, j)),
        out_shape=jax.ShapeDtypeStruct((m, n), out_dtype),
        scratch_shapes=[pltpu.VMEM((k, tn), BF16), pltpu.VMEM((k, tn), BF16)],
        compiler_params=_params(2),
        name="shared_swiglu",
    )(a, w1, w2)


def _proj_residual(a1, a2, w, x, stats, g, b, alpha, tm, tn):
    m, k1 = a1.shape
    k2 = a2.shape[1]
    n = w.shape[1]
    return pl.pallas_call(
        functools.partial(_proj_residual_kernel, alpha=alpha, k1=k1),
        grid=(n // tn, m // tm),
        in_specs=[pl.BlockSpec((tm, k1), lambda j, i: (i, 0)),
                  pl.BlockSpec((tm, k2), lambda j, i: (i, 0)),
                  pl.BlockSpec((k1 + k2, tn), lambda j, i: (0, j)),
                  pl.BlockSpec((tm, tn), lambda j, i: (i, j)),
                  pl.BlockSpec((tm, LANES), lambda j, i: (i, 0)),
                  pl.BlockSpec((1, tn), lambda j, i: (0, j)),
                  pl.BlockSpec((1, tn), lambda j, i: (0, j))],
        out_specs=pl.BlockSpec((tm, tn), lambda j, i: (i, j)),
        out_shape=jax.ShapeDtypeStruct((m, n), F32),
        scratch_shapes=[pltpu.VMEM((k1 + k2, tn), BF16)],
        compiler_params=_params(2),
        name="outproj_residual",
    )(a1, a2, w, x, stats, g, b)


def _conv_module_kernel(c_ref, cm_ref, w_ref, b_ref, g_ref, be_ref, o_ref, buf_ref, sh_ref, acc_ref, *, ts, kw):
    i = pl.program_id(0)
    ch = c_ref.shape[1]

    @pl.when(i == 0)
    def _():
        buf_ref[0:CONV_HALO - N_META, :] = jnp.zeros((CONV_HALO - N_META, ch), F32)
        buf_ref[CONV_HALO - N_META:CONV_HALO, :] = cm_ref[...]

    @pl.when(i > 0)
    def _():
        buf_ref[0:CONV_HALO, :] = buf_ref[ts:ts + CONV_HALO, :]

    buf_ref[CONV_HALO:CONV_HALO + ts, :] = c_ref[...]
    base = CONV_HALO - (kw - 1)
    span = ts + CONV_HALO - SUBLANES
    for r in range(1, SUBLANES):
        sh_ref[r - 1] = buf_ref[r:r + span, :]

    def row_block(rb, carry):
        r0 = pl.multiple_of(rb * CONV_ROWS, CONV_ROWS)
        for cb in range(ch // CONV_LANES):
            lanes = slice(cb * CONV_LANES, (cb + 1) * CONV_LANES)
            acc = jnp.zeros((CONV_ROWS, CONV_LANES), F32) + b_ref[:, lanes]
            for j in range(kw):
                off = base + j
                r, q = off % SUBLANES, off - off % SUBLANES
                if r == 0:
                    src = buf_ref[pl.ds(r0 + q, CONV_ROWS), lanes]
                else:
                    src = sh_ref[r - 1, pl.ds(r0 + q, CONV_ROWS), lanes]
                acc = acc + w_ref[j:j + 1, lanes] * src
            acc_ref[pl.ds(r0, CONV_ROWS), lanes] = acc
        return carry
    lax.fori_loop(0, ts // CONV_ROWS, row_block, 0)
    y = _ln_rows(acc_ref[...], g_ref[...], be_ref[...])
    o_ref[...] = _silu(y).astype(o_ref.dtype)


def _conv_module(c, c_meta, w, b, g, be, ts):
    t, ch = c.shape
    kw = w.shape[0]
    assert kw - 1 <= CONV_HALO and N_META <= CONV_HALO and ts >= CONV_HALO
    assert ts % CONV_ROWS == 0 and ch % CONV_LANES == 0
    return pl.pallas_call(
        functools.partial(_conv_module_kernel, ts=ts, kw=kw),
        grid=(t // ts,),
        in_specs=[pl.BlockSpec((ts, ch), lambda i: (i, 0)),
                  pl.BlockSpec((N_META, ch), lambda i: (0, 0)),
                  pl.BlockSpec((kw, ch), lambda i: (0, 0)),
                  pl.BlockSpec((1, ch), lambda i: (0, 0)),
                  pl.BlockSpec((1, ch), lambda i: (0, 0)),
                  pl.BlockSpec((1, ch), lambda i: (0, 0))],
        out_specs=pl.BlockSpec((ts, ch), lambda i: (i, 0)),
        out_shape=jax.ShapeDtypeStruct((t, ch), BF16),
        scratch_shapes=[pltpu.VMEM((CONV_HALO + ts, ch), F32),
                        pltpu.VMEM((SUBLANES - 1, ts + CONV_HALO - SUBLANES, ch), F32),
                        pltpu.VMEM((ts, ch), F32)],
        compiler_params=_params(1),
        name="conv_module",
    )(c, c_meta, w, b, g, be)


def _lane_sums(blocks):
    rows = blocks[0].shape[0]
    x = jnp.concatenate(blocks, axis=0)
    hi = x.astype(BF16)
    lo = (x - hi.astype(F32)).astype(BF16)
    ones = jnp.ones((HEAD_DIM, HEAD_DIM), BF16)
    s = _dot(hi, ones) + _dot(lo, ones)
    return [s[n * rows:(n + 1) * rows] for n in range(len(blocks))]


def _gdn_kernel(ax_ref, am_ref, zx_ref, zm_ref, bg_ref, gt_ref, gn_ref, o_ref, s_ref, *, n_heads):
    c = pl.program_id(0)
    gw = n_heads * HEAD_DIM
    C = CHUNK

    @pl.when(c == 0)
    def _():
        s_ref[...] = jnp.zeros(s_ref.shape, F32)

    def act(col0):
        cols = slice(col0, col0 + HEAD_DIM)
        return jnp.where(c == 0, am_ref[:, cols], ax_ref[:, cols])

    bg = bg_ref[...]
    beta_all = bg[:, 0:n_heads]
    g_all = bg[:, n_heads:2 * n_heads]
    row = lax.broadcasted_iota(I32, (C, C), 0)
    col = lax.broadcasted_iota(I32, (C, C), 1)
    incl = row >= col
    strict = row > col
    tri_incl = jnp.where(incl, 1.0, 0.0).astype(F32)
    tri_upper = jnp.where(row <= col, 1.0, 0.0).astype(F32)
    hp = lax.Precision.HIGHEST
    gc_cols = jnp.dot(tri_incl, g_all, precision=hp, preferred_element_type=F32)
    gc_rows = jnp.dot(gt_ref[0], tri_upper, precision=hp, preferred_element_type=F32)
    eye = jnp.where(row == col, 1.0, 0.0).astype(F32)
    right_half = lax.broadcasted_iota(I32, (C, 2 * C), 1) >= C
    gn = gn_ref[...]

    for h0 in range(0, n_heads, HEAD_GROUP):
        heads = range(h0, min(h0 + HEAD_GROUP, n_heads))
        st = {h: {} for h in heads}

        qs = [act(h * HEAD_DIM) for h in heads]
        ks = [act(gw + h * HEAD_DIM) for h in heads]
        ssq = _lane_sums([x * x for x in qs + ks])

        for n, h in enumerate(heads):
            d = st[h]
            q = qs[n] * lax.rsqrt(ssq[n] + RMS_EPS) * (HEAD_DIM ** -0.5)
            k = ks[n] * lax.rsqrt(ssq[len(qs) + n] + RMS_EPS)
            beta = beta_all[:, h:h + 1]
            gcol = gc_cols[:, h:h + 1]
            grow = gc_rows[h:h + 1, :]
            g_last = gc_cols[C - 1:C, h:h + 1]
            d["decay"] = jnp.where(incl, jnp.exp(jnp.where(incl, gcol - grow, 0.0)), 0.0)
            egc = jnp.exp(gcol)
            kb = k * beta
            d["a1"] = _dot_nt(jnp.concatenate([kb, q], axis=0).astype(BF16), k.astype(BF16))
            v = act(2 * gw + h * HEAD_DIM)
            d["rhs"] = jnp.concatenate([v * beta, kb * egc], axis=1).astype(BF16)
            d["q_dec"] = q * egc
            d["k_dec"] = (k * jnp.exp(g_last - gcol)).astype(BF16)
            d["s_scale"] = jnp.exp(g_last)

        for h in heads:
            d = st[h]
            lmat = jnp.where(strict, d["a1"][0:C] * d["decay"], 0.0)
            d["attn"] = (d["a1"][C:2 * C] * d["decay"]).astype(BF16)
            d["wmat"] = jnp.concatenate([-lmat, eye], axis=1)

        for _ in range(6):
            for h in heads:
                d = st[h]
                wmat = d["wmat"]
                r = _dot(wmat[:, 0:C].astype(BF16), wmat.astype(BF16))
                d["wmat"] = r + jnp.where(right_half, wmat, 0.0)

        for h in heads:
            d = st[h]
            tinv = jnp.where(right_half, d["wmat"], 0.0).astype(BF16)
            d["sol"] = _dot(tinv, jnp.concatenate([d["rhs"], d["rhs"]], axis=0))

        for h in heads:
            d = st[h]
            w = d["sol"][:, HEAD_DIM:2 * HEAD_DIM]
            d["ws_qs"] = _dot(jnp.concatenate([w, d["q_dec"]], axis=0).astype(BF16), s_ref[h].astype(BF16))

        for h in heads:
            d = st[h]
            vnb = (d["sol"][:, 0:HEAD_DIM] - d["ws_qs"][0:C]).astype(BF16)
            d["o"] = d["ws_qs"][C:2 * C] + _dot(d["attn"], vnb)
            s_ref[h] = s_ref[h] * d["s_scale"] + _dot_tn(d["k_dec"], vnb)

        osq = _lane_sums([st[h]["o"] * st[h]["o"] for h in heads])
        for n, h in enumerate(heads):
            o = st[h]["o"]
            sl = slice(h * HEAD_DIM, (h + 1) * HEAD_DIM)
            o = o * lax.rsqrt(osq[n] * (1.0 / HEAD_DIM) + RMS_EPS) * gn
            z = jnp.where(c == 0, zm_ref[:, sl], zx_ref[:, sl])
            o_ref[:, sl] = (o * _silu(z)).astype(o_ref.dtype)


def _gdn(qkv, qkv_meta, z, z_meta, bg_all, gt3, gn_row, n_heads):
    t = qkv.shape[0]
    gw = n_heads * HEAD_DIM
    n_chunks = t // CHUNK + 1
    xmap = lambda c: (jnp.maximum(c - 1, 0), 0)
    fixed = lambda c: (0, 0)
    return pl.pallas_call(
        functools.partial(_gdn_kernel, n_heads=n_heads),
        grid=(n_chunks,),
        in_specs=[pl.BlockSpec((CHUNK, 3 * gw), xmap),
                  pl.BlockSpec((CHUNK, 3 * gw), fixed),
                  pl.BlockSpec((CHUNK, gw), xmap),
                  pl.BlockSpec((CHUNK, gw), fixed),
                  pl.BlockSpec((CHUNK, 2 * n_heads), lambda c: (c, 0)),
                  pl.BlockSpec((1, n_heads, CHUNK), lambda c: (c, 0, 0)),
                  pl.BlockSpec((1, HEAD_DIM), fixed)],
        out_specs=pl.BlockSpec((CHUNK, gw), xmap),
        out_shape=jax.ShapeDtypeStruct((t, gw), BF16),
        scratch_shapes=[pltpu.VMEM((n_heads, HEAD_DIM, HEAD_DIM), F32)],
        compiler_params=_params(1),
        name="gated_delta_rule",
    )(qkv, qkv_meta, z, z_meta, bg_all, gt3, gn_row)


def _router_kernel(r_ref, g_ref, b_ref, rwh_ref, rwl_ref, bias_ref, h1_ref, h1b_ref, h1p_ref,
                   eidx_ref, wsel_ref, pos_ref, cnt_ref, carry_ref, *, tm, n_exp):
    i = pl.program_id(0)
    per = n_exp // N_EXPERT_GROUPS

    @pl.when(i == 0)
    def _():
        carry_ref[...] = jnp.zeros(carry_ref.shape, F32)

    h1 = _ln_rows(r_ref[...], g_ref[...], b_ref[...])
    h1_ref[...] = h1
    hb = h1.astype(BF16)
    h1b_ref[...] = hb
    h1p_ref[...] = _pack_rows(h1)

    hl = (h1 - hb.astype(F32)).astype(BF16)
    logits = _dot_nt(rwh_ref[...], hb) + (_dot_nt(rwl_ref[...], hb) + _dot_nt(rwh_ref[...], hl))
    scores = _sigmoid(logits)
    biased = scores + bias_ref[...]
    neg_inf = F32(-jnp.inf)

    gs_rows = []
    sub = lax.broadcasted_iota(I32, (per, tm), 0)
    for g in range(N_EXPERT_GROUPS):
        xg = biased[g * per:(g + 1) * per, :]
        m1 = jnp.max(xg, axis=0, keepdims=True)
        first = jnp.min(jnp.where(xg == m1, sub, per), axis=0, keepdims=True)
        m2 = jnp.max(jnp.where(sub == first, neg_inf, xg), axis=0, keepdims=True)
        gs_rows.append(m1 + m2)
    gs = jnp.concatenate(gs_rows, axis=0)
    gid = lax.broadcasted_iota(I32, (N_EXPERT_GROUPS, tm), 0)
    grank = jnp.zeros((N_EXPERT_GROUPS, tm), F32)
    for g in range(N_EXPERT_GROUPS):
        rowv = gs[g:g + 1, :]
        grank = grank + jnp.where(rowv > gs, 1.0, jnp.where((rowv == gs) & (gid > g), 1.0, 0.0))
    gsel = jnp.where(grank < TOPK_GROUPS, 1.0, 0.0)
    emask = jnp.concatenate(
        [jnp.broadcast_to(gsel[g:g + 1, :], (per, tm)) for g in range(N_EXPERT_GROUPS)], axis=0)
    masked = jnp.where(emask > 0.0, biased, neg_inf)
    eid = lax.broadcasted_iota(I32, (n_exp, tm), 0)
    erank = jnp.zeros((n_exp, tm), F32)
    for e in range(n_exp):
        rowv = masked[e:e + 1, :]
        erank = erank + jnp.where(rowv > masked, 1.0, jnp.where((rowv == masked) & (eid > e), 1.0, 0.0))
    sel = jnp.where(erank < TOP_K, emask, 0.0)
    wdense = scores * sel
    gate = wdense / jnp.sum(wdense, axis=0, keepdims=True) * ROUTED_SCALE

    er = lax.broadcasted_iota(I32, (n_exp, n_exp), 0)
    ec = lax.broadcasted_iota(I32, (n_exp, n_exp), 1)
    lower = jnp.where(er > ec, 1.0, 0.0).astype(BF16)
    selb = sel.astype(BF16)
    slot = _dot(lower, selb)
    tr = lax.broadcasted_iota(I32, (tm, tm), 0)
    tc = lax.broadcasted_iota(I32, (tm, tm), 1)
    upper = jnp.where(tr < tc, 1.0, 0.0).astype(BF16)
    carry = carry_ref[:, 0:1]
    pos = _dot(selb, upper) + carry
    eid_f = eid.astype(F32)
    e_rows, w_rows, p_rows = [], [], []
    for k in range(TOP_K):
        mk = jnp.where(slot == float(k), sel, 0.0)
        e_rows.append(jnp.sum(mk * eid_f, axis=0, keepdims=True))
        w_rows.append(jnp.sum(mk * gate, axis=0, keepdims=True))
        p_rows.append(jnp.sum(mk * pos, axis=0, keepdims=True))
    eidx_ref[...] = jnp.concatenate(e_rows, axis=0).astype(I32)
    wsel_ref[...] = jnp.concatenate(w_rows, axis=0)
    pos_ref[...] = jnp.concatenate(p_rows, axis=0).astype(I32)
    new_carry = carry + jnp.sum(sel, axis=1, keepdims=True)
    carry_ref[...] = jnp.broadcast_to(new_carry, carry_ref.shape)
    cnt_ref[...] = jnp.broadcast_to(new_carry, cnt_ref.shape).astype(I32)


def _router(r, g, b, rw_hi, rw_lo, bias_col, tm):
    t, d = r.shape
    n_exp = rw_hi.shape[0]
    row = lambda i: (i, 0)
    colb = lambda i: (0, i)
    fixed = lambda i: (0, 0)
    return pl.pallas_call(
        functools.partial(_router_kernel, tm=tm, n_exp=n_exp),
        grid=(t // tm,),
        in_specs=[pl.BlockSpec((tm, d), row),
                  pl.BlockSpec((1, d), fixed),
                  pl.BlockSpec((1, d), fixed),
                  pl.BlockSpec((n_exp, d), fixed),
                  pl.BlockSpec((n_exp, d), fixed),
                  pl.BlockSpec((n_exp, 1), fixed)],
        out_specs=[pl.BlockSpec((tm, d), row),
                   pl.BlockSpec((tm, d), row),
                   pl.BlockSpec((tm, d // 2), row),
                   pl.BlockSpec((TOP_K, tm), colb),
                   pl.BlockSpec((TOP_K, tm), colb),
                   pl.BlockSpec((TOP_K, tm), colb),
                   pl.BlockSpec((n_exp, 128), fixed)],
        out_shape=[jax.ShapeDtypeStruct((t, d), F32),
                   jax.ShapeDtypeStruct((t, d), BF16),
                   jax.ShapeDtypeStruct((t, d // 2), U32),
                   jax.ShapeDtypeStruct((TOP_K, t), I32),
                   jax.ShapeDtypeStruct((TOP_K, t), F32),
                   jax.ShapeDtypeStruct((TOP_K, t), I32),
                   jax.ShapeDtypeStruct((n_exp, 128), I32)],
        scratch_shapes=[pltpu.VMEM((n_exp, 128), F32)],
        compiler_params=_params(1),
        name="ln1_router",
    )(r, g, b, rw_hi, rw_lo, bias_col)


def _dest_kernel(gstart_ref, eidx_ref, pos_ref, dest_ref, *, n_exp):
    eidx = eidx_ref[...]
    acc = pos_ref[...]
    for e in range(n_exp):
        acc = acc + jnp.where(eidx == e, gstart_ref[e], 0)
    dest_ref[...] = acc


def _dest_rows(gstart, eidx, pos):
    n_exp = gstart.shape[0]
    return pl.pallas_call(
        functools.partial(_dest_kernel, n_exp=n_exp),
        grid_spec=pltpu.PrefetchScalarGridSpec(
            num_scalar_prefetch=1,
            grid=(1,),
            in_specs=[pl.BlockSpec(eidx.shape, lambda i, gs: (0, 0)),
                      pl.BlockSpec(pos.shape, lambda i, gs: (0, 0))],
            out_specs=pl.BlockSpec(eidx.shape, lambda i, gs: (0, 0))),
        out_shape=jax.ShapeDtypeStruct(eidx.shape, I32),
        compiler_params=_params(1),
        name="dest_rows",
    )(gstart, eidx, pos)


def _dispatch_kernel(dest_ref, hp_ref, xs_ref, sem, *, tm):
    def issue(t, carry):
        for k in range(TOP_K):
            d = dest_ref[0, 0, t * TOP_K + k]
            pltpu.make_async_copy(hp_ref.at[pl.ds(t, 1)], xs_ref.at[pl.ds(d, 1)], sem).start(priority=k % 2)
        return carry
    lax.fori_loop(0, tm, issue, 0)
    pltpu.make_async_copy(xs_ref.at[pl.ds(0, tm * TOP_K)], xs_ref.at[pl.ds(0, tm * TOP_K)], sem).wait()


def _dispatch(dest_tiles, h1p, tm):
    t, dw = h1p.shape
    return pl.pallas_call(
        functools.partial(_dispatch_kernel, tm=tm),
        grid=(t // tm,),
        in_specs=[pl.BlockSpec((1, 1, tm * TOP_K), lambda i: (i, 0, 0), memory_space=pltpu.SMEM),
                  pl.BlockSpec((tm, dw), lambda i: (i, 0))],
        out_specs=pl.BlockSpec(memory_space=pl.ANY),
        out_shape=jax.ShapeDtypeStruct((t * TOP_K, dw), U32),
        scratch_shapes=[pltpu.SemaphoreType.DMA(())],
        compiler_params=_params(1),
        name="dispatch_rows",
    )(dest_tiles, h1p)


def _pack_rows(y):
    n = y.shape[1] // 2
    yb = y.astype(BF16)
    hi = lax.bitcast_convert_type(yb[:, 0:n].astype(F32), U32)
    lo = lax.bitcast_convert_type(yb[:, n:].astype(F32), U32)
    return hi | (lo >> 16)


def _unpack_rows_f32(xu):
    hi = lax.bitcast_convert_type(xu & jnp.uint32(0xFFFF0000), F32)
    lo = lax.bitcast_convert_type(xu << 16, F32)
    return hi, lo


def _unpack_rows(xu):
    hi, lo = _unpack_rows_f32(xu)
    return hi.astype(BF16), lo.astype(BF16)


def _expert_weights(v, ve_ref, vfe_ref, eo_ref, vne_ref, hbm_refs, f32_refs, bf16_refs, sems):
    @pl.when(vfe_ref[v] == 1)
    def _():
        slot = eo_ref[v] % 2
        e = ve_ref[v]

        def copies(expert, s):
            return [pltpu.make_async_copy(h.at[expert], f.at[s], sem.at[s])
                    for h, f, sem in zip(hbm_refs, f32_refs, sems)]

        @pl.when(v == 0)
        def _():
            for cp in copies(e, slot):
                cp.start()

        for cp in copies(e, slot):
            cp.wait()
        ne = vne_ref[v]

        @pl.when(ne >= 0)
        def _():
            for cp in copies(ne, 1 - slot):
                cp.start()

        for f, bf in zip(f32_refs, bf16_refs):
            bf[...] = f[slot].astype(BF16)


def _visit_subblocks(vt_ref, ve_ref, vft_ref, gs_ref, ge_ref, v, tm, sb, o_ref, compute):
    e = ve_ref[v]
    gs, ge = gs_ref[e], ge_ref[e]
    first = vft_ref[v] == 1
    width = o_ref.shape[1]
    t0 = vt_ref[v] * tm
    whole = (t0 >= gs) & (t0 + tm <= ge)

    @pl.when(whole)
    def _():
        o_ref[...] = compute(0, tm).astype(o_ref.dtype)

    for s in range(tm // sb):
        r0 = t0 + s * sb
        hit = (r0 < ge) & (r0 + sb > gs)
        rows = pl.ds(s * sb, sb)

        @pl.when(hit & jnp.logical_not(whole))
        def _():
            y = compute(s * sb, sb).astype(o_ref.dtype)
            owned = (r0 >= gs) & (r0 + sb <= ge)

            @pl.when(owned)
            def _():
                o_ref[rows, :] = y

            @pl.when(jnp.logical_not(owned))
            def _():
                ridx = r0 + lax.broadcasted_iota(I32, (sb, width), 0)
                mask = (ridx >= gs) & (ridx < ge)

                @pl.when(first)
                def _():
                    o_ref[rows, :] = jnp.where(mask, y, jnp.zeros_like(y))

                @pl.when(jnp.logical_not(first))
                def _():
                    o_ref[rows, :] = jnp.where(mask, y, o_ref[rows, :])

        @pl.when(jnp.logical_not(hit) & first)
        def _():
            o_ref[rows, :] = jnp.zeros((sb, width), o_ref.dtype)


def _gmm_up_kernel(vt_ref, ve_ref, vft_ref, vfe_ref, eo_ref, vne_ref, gs_ref, ge_ref, nv_ref,
                   xs_ref, wg_hbm, wu_hbm, o_ref, wgf_ref, wuf_ref, wgb_ref, wub_ref, semg, semu, *, tm, sb):
    v = pl.program_id(0)

    @pl.when(v < nv_ref[0])
    def _():
        _expert_weights(v, ve_ref, vfe_ref, eo_ref, vne_ref, (wg_hbm, wu_hbm), (wgf_ref, wuf_ref),
                        (wgb_ref, wub_ref), (semg, semu))

        def compute(start, size):
            hi, lo = _unpack_rows(xs_ref[start:start + size, :])
            kh = hi.shape[1]
            hg = _dot(hi, wgb_ref[0:kh, :]) + _dot(lo, wgb_ref[kh:, :])
            hu = _dot(hi, wub_ref[0:kh, :]) + _dot(lo, wub_ref[kh:, :])
            return _silu(hg) * hu

        _visit_subblocks(vt_ref, ve_ref, vft_ref, gs_ref, ge_ref, v, tm, sb, o_ref, compute)


def _gmm_down_kernel(vt_ref, ve_ref, vft_ref, vfe_ref, eo_ref, vne_ref, gs_ref, ge_ref, nv_ref,
                     a_ref, wd_hbm, o_ref, wdf_ref, wdb_ref, semd, *, tm, sb):
    v = pl.program_id(0)

    @pl.when(v < nv_ref[0])
    def _():
        _expert_weights(v, ve_ref, vfe_ref, eo_ref, vne_ref, (wd_hbm,), (wdf_ref,), (wdb_ref,), (semd,))

        def compute(start, size):
            return _pack_rows(_dot(a_ref[start:start + size, :], wdb_ref[...]))

        _visit_subblocks(vt_ref, ve_ref, vft_ref, gs_ref, ge_ref, v, tm, sb, o_ref, compute)


def _visit_schedule(counts, n_rows, tm):
    n_exp = counts.shape[0]
    n_tiles = n_rows // tm
    n_visits = n_tiles + n_exp - 1
    ends = jnp.cumsum(counts)
    starts = ends - counts
    first_tile = starts // tm
    last_tile = jnp.maximum(ends - 1, 0) // tm
    tiles_e = jnp.where(counts > 0, last_tile - first_tile + 1, 0)
    vend = jnp.cumsum(tiles_e)
    vstart = vend - tiles_e
    total = vend[-1]
    v = jnp.arange(n_visits, dtype=I32)
    vc = jnp.minimum(v, total - 1)
    ve = jnp.sum(vend[None, :] <= vc[:, None], axis=1).astype(I32)
    vt = (vc - vstart[ve] + first_tile[ve]).astype(I32)
    prev_t = jnp.concatenate([jnp.full((1,), -1, I32), vt[:-1]])
    prev_e = jnp.concatenate([jnp.full((1,), -1, I32), ve[:-1]])
    vft = (vt != prev_t).astype(I32)
    vfe = (ve != prev_e).astype(I32)
    eo = (jnp.cumsum(vfe) - 1).astype(I32)
    eidx = jnp.arange(n_exp, dtype=I32)
    later = lax.cummin(jnp.where(counts > 0, eidx, n_exp), axis=0, reverse=True)
    nxt = jnp.concatenate([later[1:], jnp.full((1,), n_exp, I32)])
    vne = jnp.where(nxt < n_exp, nxt, -1)[ve].astype(I32)
    sched = (vt, ve, vft, vfe, eo, vne, starts.astype(I32), ends.astype(I32), total.reshape(1).astype(I32))
    return sched, n_visits


def _gmm_up(sched, n_visits, xs, wg, wu, tm, sb):
    n, dw = xs.shape
    n_exp, d, f = wg.shape
    return pl.pallas_call(
        functools.partial(_gmm_up_kernel, tm=tm, sb=sb),
        grid_spec=pltpu.PrefetchScalarGridSpec(
            num_scalar_prefetch=len(sched),
            grid=(n_visits,),
            in_specs=[pl.BlockSpec((tm, dw), lambda v, vt, *_: (vt[v], 0)),
                      pl.BlockSpec(memory_space=pl.ANY),
                      pl.BlockSpec(memory_space=pl.ANY)],
            out_specs=pl.BlockSpec((tm, f), lambda v, vt, *_: (vt[v], 0)),
            scratch_shapes=[pltpu.VMEM((2, d, f), F32), pltpu.VMEM((2, d, f), F32),
                            pltpu.VMEM((d, f), BF16), pltpu.VMEM((d, f), BF16),
                            pltpu.SemaphoreType.DMA((2,)), pltpu.SemaphoreType.DMA((2,))]),
        out_shape=jax.ShapeDtypeStruct((n, f), BF16),
        compiler_params=_params(1),
        name="experts_gate_up",
    )(*sched, xs, wg, wu)


def _gmm_down(sched, n_visits, act, wd, tm, sb):
    n, f = act.shape
    n_exp, _, d = wd.shape
    return pl.pallas_call(
        functools.partial(_gmm_down_kernel, tm=tm, sb=sb),
        grid_spec=pltpu.PrefetchScalarGridSpec(
            num_scalar_prefetch=len(sched),
            grid=(n_visits,),
            in_specs=[pl.BlockSpec((tm, f), lambda v, vt, *_: (vt[v], 0)),
                      pl.BlockSpec(memory_space=pl.ANY)],
            out_specs=pl.BlockSpec((tm, d // 2), lambda v, vt, *_: (vt[v], 0)),
            scratch_shapes=[pltpu.VMEM((2, f, d), F32), pltpu.VMEM((f, d), BF16),
                            pltpu.SemaphoreType.DMA((2,))]),
        out_shape=jax.ShapeDtypeStruct((n, d // 2), U32),
        compiler_params=_params(1),
        name="experts_down",
    )(*sched, act, wd)


def _combine_kernel(dcur_ref, dnxt_ref, w_ref, h1_ref, as_ref, wsd_ref, g_ref, b_ref, ys_ref, o_ref,
                    gbuf0_ref, gbuf1_ref, ybuf_ref, sem, *, tm, alpha):
    i = pl.program_id(0)
    n = pl.num_programs(0)
    groups = tm // SUBLANES
    half = o_ref.shape[1] // 2

    def issue_token(dref, buf_ref, s, t):
        for k in range(TOP_K):
            src = dref[0, 0, t * TOP_K + k]
            pltpu.make_async_copy(ys_ref.at[pl.ds(src, 1)], buf_ref.at[k, pl.ds(t, 1)],
                                  sem.at[s]).start(priority=k % 2)

    def wait_buf(buf_ref, s):
        for k in range(TOP_K):
            pltpu.make_async_copy(ys_ref.at[pl.ds(0, tm)], buf_ref.at[k], sem.at[s]).wait()

    @pl.when(i == 0)
    def _():
        def first(t, carry):
            issue_token(dcur_ref, gbuf0_ref, 0, t)
            return carry
        lax.fori_loop(0, tm, first, 0)

    ybuf_ref[...] = alpha * h1_ref[...] + _dot(as_ref[...], wsd_ref[...])

    def reduce_tile(cur_ref, cur_s, nxt_ref, nxt_s):
        wait_buf(cur_ref, cur_s)

        def reduce_group(r, carry):
            r8 = pl.multiple_of(r * SUBLANES, SUBLANES)
            rows = pl.ds(r8, SUBLANES)
            y_hi = ybuf_ref[rows, 0:half]
            y_lo = ybuf_ref[rows, half:]
            w = w_ref[rows, :]
            for j in range(SUBLANES):
                issue_token(dnxt_ref, nxt_ref, nxt_s, r8 + j)
                for k in range(j * TOP_K // SUBLANES, (j + 1) * TOP_K // SUBLANES):
                    e_hi, e_lo = _unpack_rows_f32(cur_ref[k, rows, :])
                    wk = w[:, k:k + 1]
                    y_hi = y_hi + wk * e_hi
                    y_lo = y_lo + wk * e_lo
            ybuf_ref[rows, 0:half] = y_hi
            ybuf_ref[rows, half:] = y_lo
            return carry
        lax.fori_loop(0, groups, reduce_group, 0)

        @pl.when(i == n - 1)
        def _():
            wait_buf(nxt_ref, nxt_s)

    @pl.when(i % 2 == 0)
    def _():
        reduce_tile(gbuf0_ref, 0, gbuf1_ref, 1)

    @pl.when(i % 2 == 1)
    def _():
        reduce_tile(gbuf1_ref, 1, gbuf0_ref, 0)

    o_ref[...] = _ln_rows(ybuf_ref[...], g_ref[...], b_ref[...])


def _combine(dest_tiles, w_tok, h1, act_s, wsd, g, b, ys, tm, alpha):
    t, d = h1.shape
    f = act_s.shape[1]
    n = t // tm
    row = lambda i: (i, 0)
    fixed = lambda i: (0, 0)
    return pl.pallas_call(
        functools.partial(_combine_kernel, tm=tm, alpha=alpha),
        grid=(n,),
        in_specs=[pl.BlockSpec((1, 1, tm * TOP_K), lambda i: (i, 0, 0), memory_space=pltpu.SMEM),
                  pl.BlockSpec((1, 1, tm * TOP_K), lambda i: (jnp.minimum(i + 1, n - 1), 0, 0),
                               memory_space=pltpu.SMEM),
                  pl.BlockSpec((tm, TOP_K), row),
                  pl.BlockSpec((tm, d), row),
                  pl.BlockSpec((tm, f), row),
                  pl.BlockSpec((f, d), fixed),
                  pl.BlockSpec((1, d), fixed),
                  pl.BlockSpec((1, d), fixed),
                  pl.BlockSpec(memory_space=pl.ANY)],
        out_specs=pl.BlockSpec((tm, d), row),
        out_shape=jax.ShapeDtypeStruct((t, d), F32),
        scratch_shapes=[pltpu.VMEM((TOP_K, tm, d // 2), U32),
                        pltpu.VMEM((TOP_K, tm, d // 2), U32),
                        pltpu.VMEM((tm, d), F32),
                        pltpu.SemaphoreType.DMA((2,))],
        compiler_params=_params(1),
        name="combine_ln2",
    )(dest_tiles, dest_tiles, w_tok, h1, act_s, wsd, g, b, ys)


def _tile(n, pref):
    t = min(n, pref)
    assert n % t == 0, (n, pref)
    return t


def kernel(x, meta_tokens, emb_ln_g, emb_ln_b, w_in, conv_w, conv_b, conv_ln_g, conv_ln_b, short_conv_w, a_log, dt_bias, gdn_norm_g, w_out, ln1_g, ln1_b, router_w, router_bias, expert_w_gate, expert_w_up, expert_w_down, shared_w_gate, shared_w_up, shared_w_down, ln2_g, ln2_b):
    depth = w_in.shape[0]
    assert depth == 1 and x.shape[0] == 1, "single layer, single sequence"
    t, d = x.shape[1], x.shape[2]
    cw = conv_w.shape[2]
    n_heads = a_log.shape[1]
    gw = n_heads * HEAD_DIM
    alpha = (2.0 * depth) ** 0.25
    assert meta_tokens.shape[0] == N_META and t % CHUNK == 0 and w_in.shape[2] == 2 * cw + 4 * gw + 2 * n_heads
    row = lambda a: a.reshape(1, -1)

    h_b, h_stats = _embed_ln(x[0], row(emb_ln_g), row(emb_ln_b), _tile(t, 256), "embed_ln")
    hm_b, _ = _embed_ln(meta_tokens, row(emb_ln_g), row(emb_ln_b), N_META, "embed_ln_meta")

    w_t = jnp.transpose(w_in[0])
    tm = _tile(t, 1024)
    zeros_h = jnp.zeros((n_heads,), F32)
    alog_row = row(jnp.concatenate([zeros_h, a_log[0]]))
    dtb_row = row(jnp.concatenate([zeros_h, dt_bias[0]]))
    c_x, c_m = _inproj_glu(h_b, hm_b, w_t, 0, cw, cw, tm, _tile(cw, 256))
    qkv_x, qkv_m = _inproj_conv(h_b, hm_b, w_t, short_conv_w[0], 2 * cw, 3 * gw, tm, _tile(3 * gw, 512))
    z_x, z_m = _inproj_plain(h_b, hm_b, w_t, 2 * cw + 3 * gw, gw, tm, _tile(gw, 512))
    bg_x, bg_m = _inproj_decay(h_b, hm_b, w_t[2 * cw + 4 * gw:], alog_row, dtb_row, tm, n_heads)

    y_conv = _conv_module(c_x, c_m, conv_w[0], row(conv_b[0]), row(conv_ln_g[0]), row(conv_ln_b[0]),
                          _tile(t, 256))

    front = lambda a: jnp.pad(a, ((CHUNK - N_META, 0), (0, 0)))
    bg_all = jnp.concatenate([front(bg_m), bg_x], axis=0)
    gt3 = bg_all[:, n_heads:].reshape(-1, CHUNK, n_heads).transpose(0, 2, 1)
    y_gdn = _gdn(qkv_x, front(qkv_m), z_x, front(z_m), bg_all, gt3, row(gdn_norm_g[0]), n_heads)

    r1 = _proj_residual(y_conv, y_gdn, w_out[0], x[0], h_stats, row(emb_ln_g), row(emb_ln_b), alpha, tm,
                        _tile(d, 512))
    rwt = router_w[0].T
    rw_hi = rwt.astype(BF16)
    rw_lo = (rwt - rw_hi.astype(F32)).astype(BF16)
    h1, h1_b, h1_p, eidx, wsel, pos, cnt = _router(
        r1, row(ln1_g[0]), row(ln1_b[0]), rw_hi, rw_lo, router_bias[0].reshape(-1, 1), _tile(t, 256))
    counts = cnt[:, 0]
    gstart = jnp.cumsum(counts) - counts
    dest = _dest_rows(gstart.astype(I32), eidx, pos)

    tmd = _tile(t, 1024)
    xs = _dispatch(dest.T.reshape(t // tmd, 1, tmd * TOP_K), h1_p, tmd)
    tmg = _tile(t * TOP_K, 512)
    sbg = _tile(tmg, 128)
    sched, n_visits = _visit_schedule(counts, t * TOP_K, tmg)
    act = _gmm_up(sched, n_visits, xs, expert_w_gate[0], expert_w_up[0], tmg, sbg)
    ys = _gmm_down(sched, n_visits, act, expert_w_down[0], tmg, _tile(tmg, 256))

    act_s = _swiglu(h1_b, shared_w_gate[0], shared_w_up[0], tm, _tile(shared_w_gate.shape[2], 256), BF16)
    tmc = _tile(t, 128)
    out = _combine(dest.T.reshape(t // tmc, 1, tmc * TOP_K), wsel.T, h1, act_s, shared_w_down[0].astype(BF16),
                   row(ln2_g[0]), row(ln2_b[0]), ys, tmc, alpha)
    return out[None]
```

```python
import functools

import jax
import jax.numpy as jnp
from jax import lax
from jax.experimental import pallas as pl
from jax.experimental.pallas import tpu as pltpu

F32 = jnp.float32
BF16 = jnp.bfloat16
I32 = jnp.int32
U32 = jnp.uint32

N_META = 16
CHUNK = 64
HEAD_DIM = 128
HEAD_GROUP = 16
TOP_K = 8
N_EXPERT_GROUPS = 8
TOPK_GROUPS = 4
ROUTED_SCALE = 2.5
LN_EPS = 1e-5
RMS_EPS = 1e-6
CONV_HALO = 32
CONV_ROWS = 32
CONV_LANES = 512
SUBLANES = 8
LANES = 128
VMEM_LIMIT = 56 * 1024 * 1024


def _params(n_axes, vmem=VMEM_LIMIT):
    return pltpu.CompilerParams(dimension_semantics=("arbitrary",) * n_axes, vmem_limit_bytes=vmem)


def _ln_rows(x, g, b):
    mu = jnp.mean(x, -1, keepdims=True)
    xc = x - mu
    var = jnp.mean(xc * xc, -1, keepdims=True)
    return xc * lax.rsqrt(var + LN_EPS) * g + b


def _sigmoid(x):
    return 1.0 / (1.0 + jnp.exp(-x))


def _silu(x):
    return x * _sigmoid(x)


def _dot(a, b):
    return jnp.dot(a, b, preferred_element_type=F32)


def _dot_nt(a, b, precision=None):
    return lax.dot_general(a, b, (((1,), (1,)), ((), ())), precision=precision, preferred_element_type=F32)


def _dot_tn(a, b):
    return lax.dot_general(a, b, (((0,), (0,)), ((), ())), preferred_element_type=F32)


def _embed_ln_kernel(x_ref, g_ref, b_ref, hb_ref, st_ref):
    x = x_ref[...]
    mu = jnp.mean(x, -1, keepdims=True)
    xc = x - mu
    rs = lax.rsqrt(jnp.mean(xc * xc, -1, keepdims=True) + LN_EPS)
    hb_ref[...] = (xc * rs * g_ref[...] + b_ref[...]).astype(BF16)
    lane = lax.broadcasted_iota(I32, st_ref.shape, 1)
    st_ref[...] = jnp.where(lane == 0, mu, jnp.where(lane == 1, rs, 0.0))


def _embed_ln(x, g, b, tm, name):
    m, d = x.shape
    return pl.pallas_call(
        _embed_ln_kernel,
        grid=(m // tm,),
        in_specs=[pl.BlockSpec((tm, d), lambda i: (i, 0)),
                  pl.BlockSpec((1, d), lambda i: (0, 0)),
                  pl.BlockSpec((1, d), lambda i: (0, 0))],
        out_specs=[pl.BlockSpec((tm, d), lambda i: (i, 0)),
                   pl.BlockSpec((tm, LANES), lambda i: (i, 0))],
        out_shape=[jax.ShapeDtypeStruct((m, d), BF16), jax.ShapeDtypeStruct((m, LANES), F32)],
        compiler_params=_params(1),
        name=name,
    )(x, g, b)


def _inproj_plain_kernel(a_ref, am_ref, w_ref, o_ref, om_ref, wb_ref):
    @pl.when(pl.program_id(1) == 0)
    def _():
        wb_ref[...] = w_ref[...].astype(BF16)
        om_ref[...] = _dot_nt(am_ref[...], wb_ref[...])
    o_ref[...] = _dot_nt(a_ref[...], wb_ref[...])


def _inproj_conv_kernel(a_ref, am_ref, w_ref, cw_ref, o_ref, om_ref, wb_ref, hbuf_ref):
    i = pl.program_id(1)
    tm = a_ref.shape[0]
    kw = cw_ref.shape[0]

    def conv_silu(rows):
        acc = cw_ref[kw - 1:kw, :] * hbuf_ref[SUBLANES:SUBLANES + rows, :]
        for j in range(kw - 1):
            off = SUBLANES - (kw - 1) + j
            acc = acc + cw_ref[j:j + 1, :] * hbuf_ref[off:off + rows, :]
        return _silu(acc)

    @pl.when(i == 0)
    def _():
        wb_ref[...] = w_ref[...].astype(BF16)
        hbuf_ref[0:SUBLANES, :] = jnp.zeros((SUBLANES, hbuf_ref.shape[1]), F32)
        hbuf_ref[SUBLANES:SUBLANES + N_META, :] = _dot_nt(am_ref[...], wb_ref[...])
        om_ref[...] = conv_silu(N_META)
        hbuf_ref[0:SUBLANES, :] = hbuf_ref[N_META:N_META + SUBLANES, :]

    @pl.when(i > 0)
    def _():
        hbuf_ref[0:SUBLANES, :] = hbuf_ref[tm:tm + SUBLANES, :]

    hbuf_ref[SUBLANES:SUBLANES + tm, :] = _dot_nt(a_ref[...], wb_ref[...])
    o_ref[...] = conv_silu(tm)


def _inproj_glu_kernel(a_ref, am_ref, w1_ref, w2_ref, o_ref, om_ref, w1b_ref, w2b_ref):
    @pl.when(pl.program_id(1) == 0)
    def _():
        w1b_ref[...] = w1_ref[...].astype(BF16)
        w2b_ref[...] = w2_ref[...].astype(BF16)
        am = am_ref[...]
        om_ref[...] = _dot_nt(am, w1b_ref[...]) * _sigmoid(_dot_nt(am, w2b_ref[...]))
    a = a_ref[...]
    o_ref[...] = _dot_nt(a, w1b_ref[...]) * _sigmoid(_dot_nt(a, w2b_ref[...]))


def _decay_epilogue(p, alog, dtb, n_heads):
    x = p + dtb
    softplus = jnp.maximum(x, 0.0) + jnp.log1p(jnp.exp(-jnp.abs(x)))
    g = -jnp.exp(alog) * softplus
    lane = lax.broadcasted_iota(I32, p.shape, 1)
    return jnp.where(lane < n_heads, _sigmoid(p), g)


def _inproj_decay_kernel(a_ref, am_ref, w_ref, alog_ref, dtb_ref, o_ref, om_ref, wb_ref, *, n_heads):
    @pl.when(pl.program_id(1) == 0)
    def _():
        wb_ref[...] = w_ref[...].astype(BF16)
        om_ref[...] = _decay_epilogue(_dot_nt(am_ref[...], wb_ref[...]), alog_ref[...], dtb_ref[...], n_heads)
    o_ref[...] = _decay_epilogue(_dot_nt(a_ref[...], wb_ref[...]), alog_ref[...], dtb_ref[...], n_heads)


def _swiglu_kernel(a_ref, w1_ref, w2_ref, o_ref, w1b_ref, w2b_ref):
    @pl.when(pl.program_id(1) == 0)
    def _():
        w1b_ref[...] = w1_ref[...].astype(BF16)
        w2b_ref[...] = w2_ref[...].astype(BF16)
    hi, lo = _unpack_rows(a_ref[...])
    kh = hi.shape[1]
    gate = _dot(hi, w1b_ref[0:kh, :]) + _dot(lo, w1b_ref[kh:, :])
    up = _dot(hi, w2b_ref[0:kh, :]) + _dot(lo, w2b_ref[kh:, :])
    o_ref[...] = (_silu(gate) * up).astype(o_ref.dtype)


def _proj_residual_kernel(a1_ref, a2_ref, w_ref, x_ref, st_ref, g_ref, b_ref, o_ref, wb_ref, *, alpha, k1):
    @pl.when(pl.program_id(1) == 0)
    def _():
        wb_ref[...] = w_ref[...].astype(BF16)
    acc = _dot(a1_ref[...], wb_ref[0:k1, :]) + _dot(a2_ref[...], wb_ref[k1:, :])
    st = st_ref[...]
    resid = (x_ref[...] - st[:, 0:1]) * st[:, 1:2] * g_ref[...] + b_ref[...]
    o_ref[...] = alpha * resid + acc


def _inproj_plain(a, am, w_t, row0, n, tm, tn):
    m, k = a.shape
    mm = am.shape[0]
    rb = row0 // tn
    return pl.pallas_call(
        _inproj_plain_kernel,
        grid=(n // tn, m // tm),
        in_specs=[pl.BlockSpec((tm, k), lambda j, i: (i, 0)),
                  pl.BlockSpec((mm, k), lambda j, i: (0, 0)),
                  pl.BlockSpec((tn, k), lambda j, i: (j + rb, 0))],
        out_specs=[pl.BlockSpec((tm, tn), lambda j, i: (i, j)),
                   pl.BlockSpec((mm, tn), lambda j, i: (0, j))],
        out_shape=[jax.ShapeDtypeStruct((m, n), F32), jax.ShapeDtypeStruct((mm, n), F32)],
        scratch_shapes=[pltpu.VMEM((tn, k), BF16)],
        compiler_params=_params(2),
        name="inproj_z",
    )(a, am, w_t)


def _inproj_conv(a, am, w_t, conv_w, row0, n, tm, tn):
    m, k = a.shape
    mm = am.shape[0]
    kw = conv_w.shape[0]
    assert kw <= SUBLANES and mm == N_META and N_META % SUBLANES == 0
    rb = row0 // tn
    return pl.pallas_call(
        _inproj_conv_kernel,
        grid=(n // tn, m // tm),
        in_specs=[pl.BlockSpec((tm, k), lambda j, i: (i, 0)),
                  pl.BlockSpec((mm, k), lambda j, i: (0, 0)),
                  pl.BlockSpec((tn, k), lambda j, i: (j + rb, 0)),
                  pl.BlockSpec((kw, tn), lambda j, i: (0, j))],
        out_specs=[pl.BlockSpec((tm, tn), lambda j, i: (i, j)),
                   pl.BlockSpec((mm, tn), lambda j, i: (0, j))],
        out_shape=[jax.ShapeDtypeStruct((m, n), F32), jax.ShapeDtypeStruct((mm, n), F32)],
        scratch_shapes=[pltpu.VMEM((tn, k), BF16), pltpu.VMEM((SUBLANES + tm, tn), F32)],
        compiler_params=_params(2),
        name="inproj_qkv_conv",
    )(a, am, w_t, conv_w)


def _inproj_glu(a, am, w_t, row1, row2, n, tm, tn):
    m, k = a.shape
    mm = am.shape[0]
    r1, r2 = row1 // tn, row2 // tn
    return pl.pallas_call(
        _inproj_glu_kernel,
        grid=(n // tn, m // tm),
        in_specs=[pl.BlockSpec((tm, k), lambda j, i: (i, 0)),
                  pl.BlockSpec((mm, k), lambda j, i: (0, 0)),
                  pl.BlockSpec((tn, k), lambda j, i: (j + r1, 0)),
                  pl.BlockSpec((tn, k), lambda j, i: (j + r2, 0))],
        out_specs=[pl.BlockSpec((tm, tn), lambda j, i: (i, j)),
                   pl.BlockSpec((mm, tn), lambda j, i: (0, j))],
        out_shape=[jax.ShapeDtypeStruct((m, n), F32), jax.ShapeDtypeStruct((mm, n), F32)],
        scratch_shapes=[pltpu.VMEM((tn, k), BF16), pltpu.VMEM((tn, k), BF16)],
        compiler_params=_params(2),
        name="inproj_glu",
    )(a, am, w_t, w_t)


def _inproj_decay(a, am, w_ba_t, alog_row, dtb_row, tm, n_heads):
    m, k = a.shape
    mm = am.shape[0]
    n = w_ba_t.shape[0]
    return pl.pallas_call(
        functools.partial(_inproj_decay_kernel, n_heads=n_heads),
        grid=(1, m // tm),
        in_specs=[pl.BlockSpec((tm, k), lambda j, i: (i, 0)),
                  pl.BlockSpec((mm, k), lambda j, i: (0, 0)),
                  pl.BlockSpec((n, k), lambda j, i: (0, 0)),
                  pl.BlockSpec((1, n), lambda j, i: (0, 0)),
                  pl.BlockSpec((1, n), lambda j, i: (0, 0))],
        out_specs=[pl.BlockSpec((tm, n), lambda j, i: (i, 0)),
                   pl.BlockSpec((mm, n), lambda j, i: (0, 0))],
        out_shape=[jax.ShapeDtypeStruct((m, n), F32), jax.ShapeDtypeStruct((mm, n), F32)],
        scratch_shapes=[pltpu.VMEM((n, k), BF16)],
        compiler_params=_params(2),
        name="inproj_decay",
    )(a, am, w_ba_t, alog_row, dtb_row)


def _swiglu(a, w1, w2, tm, tn, out_dtype):
    m = a.shape[0]
    k, n = w1.shape
    return pl.pallas_call(
        _swiglu_kernel,
        grid=(n // tn, m // tm),
        in_specs=[pl.BlockSpec((tm, k // 2), lambda j, i: (i, 0)),
                  pl.BlockSpec((k, tn), lambda j, i: (0, j)),
                  pl.BlockSpec((k, tn), lambda j, i: (0, j))],
        out_specs=pl.BlockSpec((tm, tn), lambda j, i: (i, j)),
        out_shape=jax.ShapeDtypeStruct((m, n), out_dtype),
        scratch_shapes=[pltpu.VMEM((k, tn), BF16), pltpu.VMEM((k, tn), BF16)],
        compiler_params=_params(2),
        name="shared_swiglu",
    )(a, w1, w2)


def _proj_residual(a1, a2, w, x, stats, g, b, alpha, tm, tn):
    m, k1 = a1.shape
    k2 = a2.shape[1]
    n = w.shape[1]
    return pl.pallas_call(
        functools.partial(_proj_residual_kernel, alpha=alpha, k1=k1),
        grid=(n // tn, m // tm),
        in_specs=[pl.BlockSpec((tm, k1), lambda j, i: (i, 0)),
                  pl.BlockSpec((tm, k2), lambda j, i: (i, 0)),
                  pl.BlockSpec((k1 + k2, tn), lambda j, i: (0, j)),
                  pl.BlockSpec((tm, tn), lambda j, i: (i, j)),
                  pl.BlockSpec((tm, LANES), lambda j, i: (i, 0)),
                  pl.BlockSpec((1, tn), lambda j, i: (0, j)),
                  pl.BlockSpec((1, tn), lambda j, i: (0, j))],
        out_specs=pl.BlockSpec((tm, tn), lambda j, i: (i, j)),
        out_shape=jax.ShapeDtypeStruct((m, n), F32),
        scratch_shapes=[pltpu.VMEM((k1 + k2, tn), BF16)],
        compiler_params=_params(2),
        name="outproj_residual",
    )(a1, a2, w, x, stats, g, b)


def _conv_module_kernel(c_ref, cm_ref, w_ref, b_ref, g_ref, be_ref, o_ref, buf_ref, sh_ref, acc_ref, *, ts, kw):
    i = pl.program_id(0)
    ch = c_ref.shape[1]

    @pl.when(i == 0)
    def _():
        buf_ref[0:CONV_HALO - N_META, :] = jnp.zeros((CONV_HALO - N_META, ch), F32)
        buf_ref[CONV_HALO - N_META:CONV_HALO, :] = cm_ref[...]

    @pl.when(i > 0)
    def _():
        buf_ref[0:CONV_HALO, :] = buf_ref[ts:ts + CONV_HALO, :]

    buf_ref[CONV_HALO:CONV_HALO + ts, :] = c_ref[...]
    base = CONV_HALO - (kw - 1)
    span = ts + CONV_HALO - SUBLANES
    for r in range(1, SUBLANES):
        sh_ref[r - 1] = buf_ref[r:r + span, :]

    def row_block(rb, carry):
        r0 = pl.multiple_of(rb * CONV_ROWS, CONV_ROWS)
        for cb in range(ch // CONV_LANES):
            lanes = slice(cb * CONV_LANES, (cb + 1) * CONV_LANES)
            acc = jnp.zeros((CONV_ROWS, CONV_LANES), F32) + b_ref[:, lanes]
            for j in range(kw):
                off = base + j
                r, q = off % SUBLANES, off - off % SUBLANES
                if r == 0:
                    src = buf_ref[pl.ds(r0 + q, CONV_ROWS), lanes]
                else:
                    src = sh_ref[r - 1, pl.ds(r0 + q, CONV_ROWS), lanes]
                acc = acc + w_ref[j:j + 1, lanes] * src
            acc_ref[pl.ds(r0, CONV_ROWS), lanes] = acc
        return carry
    lax.fori_loop(0, ts // CONV_ROWS, row_block, 0)
    y = _ln_rows(acc_ref[...], g_ref[...], be_ref[...])
    o_ref[...] = _silu(y).astype(o_ref.dtype)


def _conv_module(c, c_meta, w, b, g, be, ts):
    t, ch = c.shape
    kw = w.shape[0]
    assert kw - 1 <= CONV_HALO and N_META <= CONV_HALO and ts >= CONV_HALO
    assert ts % CONV_ROWS == 0 and ch % CONV_LANES == 0
    return pl.pallas_call(
        functools.partial(_conv_module_kernel, ts=ts, kw=kw),
        grid=(t // ts,),
        in_specs=[pl.BlockSpec((ts, ch), lambda i: (i, 0)),
                  pl.BlockSpec((N_META, ch), lambda i: (0, 0)),
                  pl.BlockSpec((kw, ch), lambda i: (0, 0)),
                  pl.BlockSpec((1, ch), lambda i: (0, 0)),
                  pl.BlockSpec((1, ch), lambda i: (0, 0)),
                  pl.BlockSpec((1, ch), lambda i: (0, 0))],
        out_specs=pl.BlockSpec((ts, ch), lambda i: (i, 0)),
        out_shape=jax.ShapeDtypeStruct((t, ch), BF16),
        scratch_shapes=[pltpu.VMEM((CONV_HALO + ts, ch), F32),
                        pltpu.VMEM((SUBLANES - 1, ts + CONV_HALO - SUBLANES, ch), F32),
                        pltpu.VMEM((ts, ch), F32)],
        compiler_params=_params(1),
        name="conv_module",
    )(c, c_meta, w, b, g, be)


def _lane_sums(blocks):
    rows = blocks[0].shape[0]
    x = jnp.concatenate(blocks, axis=0)
    hi = x.astype(BF16)
    lo = (x - hi.astype(F32)).astype(BF16)
    ones = jnp.ones((HEAD_DIM, HEAD_DIM), BF16)
    s = _dot(hi, ones) + _dot(lo, ones)
    return [s[n * rows:(n + 1) * rows] for n in range(len(blocks))]


def _gdn_kernel(ax_ref, am_ref, zx_ref, zm_ref, bg_ref, gt_ref, gn_ref, o_ref, s_ref, *, n_heads):
    c = pl.program_id(0)
    gw = n_heads * HEAD_DIM
    C = CHUNK

    @pl.when(c == 0)
    def _():
        s_ref[...] = jnp.zeros(s_ref.shape, F32)

    def act(col0):
        cols = slice(col0, col0 + HEAD_DIM)
        return jnp.where(c == 0, am_ref[:, cols], ax_ref[:, cols])

    bg = bg_ref[...]
    beta_all = bg[:, 0:n_heads]
    g_all = bg[:, n_heads:2 * n_heads]
    row = lax.broadcasted_iota(I32, (C, C), 0)
    col = lax.broadcasted_iota(I32, (C, C), 1)
    incl = row >= col
    strict = row > col
    tri_incl = jnp.where(incl, 1.0, 0.0).astype(F32)
    tri_upper = jnp.where(row <= col, 1.0, 0.0).astype(F32)
    hp = lax.Precision.HIGHEST
    gc_cols = jnp.dot(tri_incl, g_all, precision=hp, preferred_element_type=F32)
    gc_rows = jnp.dot(gt_ref[0], tri_upper, precision=hp, preferred_element_type=F32)
    eye = jnp.where(row == col, 1.0, 0.0).astype(F32)
    right_half = lax.broadcasted_iota(I32, (C, 2 * C), 1) >= C
    gn = gn_ref[...]

    for h0 in range(0, n_heads, HEAD_GROUP):
        heads = range(h0, min(h0 + HEAD_GROUP, n_heads))
        st = {h: {} for h in heads}

        qs = [act(h * HEAD_DIM) for h in heads]
        ks = [act(gw + h * HEAD_DIM) for h in heads]
        ssq = _lane_sums([x * x for x in qs + ks])

        for n, h in enumerate(heads):
            d = st[h]
            q = qs[n] * lax.rsqrt(ssq[n] + RMS_EPS) * (HEAD_DIM ** -0.5)
            k = ks[n] * lax.rsqrt(ssq[len(qs) + n] + RMS_EPS)
            beta = beta_all[:, h:h + 1]
            gcol = gc_cols[:, h:h + 1]
            grow = gc_rows[h:h + 1, :]
            g_last = gc_cols[C - 1:C, h:h + 1]
            d["decay"] = jnp.where(incl, jnp.exp(jnp.where(incl, gcol - grow, 0.0)), 0.0)
            egc = jnp.exp(gcol)
            kb = k * beta
            d["a1"] = _dot_nt(jnp.concatenate([kb, q], axis=0).astype(BF16), k.astype(BF16))
            v = act(2 * gw + h * HEAD_DIM)
            d["rhs"] = jnp.concatenate([v * beta, kb * egc], axis=1).astype(BF16)
            d["q_dec"] = q * egc
            d["k_dec"] = (k * jnp.exp(g_last - gcol)).astype(BF16)
            d["s_scale"] = jnp.exp(g_last)

        for h in heads:
            d = st[h]
            lmat = jnp.where(strict, d["a1"][0:C] * d["decay"], 0.0)
            d["attn"] = (d["a1"][C:2 * C] * d["decay"]).astype(BF16)
            d["wmat"] = jnp.concatenate([-lmat, eye], axis=1)

        for _ in range(6):
            for h in heads:
                d = st[h]
                wmat = d["wmat"]
                r = _dot(wmat[:, 0:C].astype(BF16), wmat.astype(BF16))
                d["wmat"] = r + jnp.where(right_half, wmat, 0.0)

        for h in heads:
            d = st[h]
            tinv = jnp.where(right_half, d["wmat"], 0.0).astype(BF16)
            d["sol"] = _dot(tinv, jnp.concatenate([d["rhs"], d["rhs"]], axis=0))

        for h in heads:
            d = st[h]
            w = d["sol"][:, HEAD_DIM:2 * HEAD_DIM]
            d["ws_qs"] = _dot(jnp.concatenate([w, d["q_dec"]], axis=0).astype(BF16), s_ref[h].astype(BF16))

        for h in heads:
            d = st[h]
            vnb = (d["sol"][:, 0:HEAD_DIM] - d["ws_qs"][0:C]).astype(BF16)
            d["o"] = d["ws_qs"][C:2 * C] + _dot(d["attn"], vnb)
            s_ref[h] = s_ref[h] * d["s_scale"] + _dot_tn(d["k_dec"], vnb)

        osq = _lane_sums([st[h]["o"] * st[h]["o"] for h in heads])
        for n, h in enumerate(heads):
            o = st[h]["o"]
            sl = slice(h * HEAD_DIM, (h + 1) * HEAD_DIM)
            o = o * lax.rsqrt(osq[n] * (1.0 / HEAD_DIM) + RMS_EPS) * gn
            z = jnp.where(c == 0, zm_ref[:, sl], zx_ref[:, sl])
            o_ref[:, sl] = (o * _silu(z)).astype(o_ref.dtype)


def _gdn(qkv, qkv_meta, z, z_meta, bg_all, gt3, gn_row, n_heads):
    t = qkv.shape[0]
    gw = n_heads * HEAD_DIM
    n_chunks = t // CHUNK + 1
    xmap = lambda c: (jnp.maximum(c - 1, 0), 0)
    fixed = lambda c: (0, 0)
    return pl.pallas_call(
        functools.partial(_gdn_kernel, n_heads=n_heads),
        grid=(n_chunks,),
        in_specs=[pl.BlockSpec((CHUNK, 3 * gw), xmap),
                  pl.BlockSpec((CHUNK, 3 * gw), fixed),
                  pl.BlockSpec((CHUNK, gw), xmap),
                  pl.BlockSpec((CHUNK, gw), fixed),
                  pl.BlockSpec((CHUNK, 2 * n_heads), lambda c: (c, 0)),
                  pl.BlockSpec((1, n_heads, CHUNK), lambda c: (c, 0, 0)),
                  pl.BlockSpec((1, HEAD_DIM), fixed)],
        out_specs=pl.BlockSpec((CHUNK, gw), xmap),
        out_shape=jax.ShapeDtypeStruct((t, gw), BF16),
        scratch_shapes=[pltpu.VMEM((n_heads, HEAD_DIM, HEAD_DIM), F32)],
        compiler_params=_params(1),
        name="gated_delta_rule",
    )(qkv, qkv_meta, z, z_meta, bg_all, gt3, gn_row)


def _router_kernel(r_ref, g_ref, b_ref, rwh_ref, rwl_ref, bias_ref, h1_ref, h1p_ref,
                   eidx_ref, wsel_ref, pos_ref, cnt_ref, carry_ref, *, tm, n_exp):
    i = pl.program_id(0)
    per = n_exp // N_EXPERT_GROUPS

    @pl.when(i == 0)
    def _():
        carry_ref[...] = jnp.zeros(carry_ref.shape, F32)

    h1 = _ln_rows(r_ref[...], g_ref[...], b_ref[...])
    h1_ref[...] = h1
    hb = h1.astype(BF16)
    h1p_ref[...] = _pack_rows(h1)

    hl = (h1 - hb.astype(F32)).astype(BF16)
    logits = _dot_nt(rwh_ref[...], hb) + (_dot_nt(rwl_ref[...], hb) + _dot_nt(rwh_ref[...], hl))
    scores = _sigmoid(logits)
    biased = scores + bias_ref[...]
    neg_inf = F32(-jnp.inf)

    gs_rows = []
    sub = lax.broadcasted_iota(I32, (per, tm), 0)
    for g in range(N_EXPERT_GROUPS):
        xg = biased[g * per:(g + 1) * per, :]
        m1 = jnp.max(xg, axis=0, keepdims=True)
        first = jnp.min(jnp.where(xg == m1, sub, per), axis=0, keepdims=True)
        m2 = jnp.max(jnp.where(sub == first, neg_inf, xg), axis=0, keepdims=True)
        gs_rows.append(m1 + m2)
    gs = jnp.concatenate(gs_rows, axis=0)
    gid = lax.broadcasted_iota(I32, (N_EXPERT_GROUPS, tm), 0)
    grank = jnp.zeros((N_EXPERT_GROUPS, tm), F32)
    for g in range(N_EXPERT_GROUPS):
        rowv = gs[g:g + 1, :]
        grank = grank + jnp.where(rowv > gs, 1.0, jnp.where((rowv == gs) & (gid > g), 1.0, 0.0))
    gsel = jnp.where(grank < TOPK_GROUPS, 1.0, 0.0)
    emask = jnp.concatenate(
        [jnp.broadcast_to(gsel[g:g + 1, :], (per, tm)) for g in range(N_EXPERT_GROUPS)], axis=0)
    masked = jnp.where(emask > 0.0, biased, neg_inf)
    eid = lax.broadcasted_iota(I32, (n_exp, tm), 0)
    erank = jnp.zeros((n_exp, tm), F32)
    for e in range(n_exp):
        rowv = masked[e:e + 1, :]
        erank = erank + jnp.where(rowv > masked, 1.0, jnp.where((rowv == masked) & (eid > e), 1.0, 0.0))
    sel = jnp.where(erank < TOP_K, emask, 0.0)
    wdense = scores * sel
    gate = wdense / jnp.sum(wdense, axis=0, keepdims=True) * ROUTED_SCALE

    er = lax.broadcasted_iota(I32, (n_exp, n_exp), 0)
    ec = lax.broadcasted_iota(I32, (n_exp, n_exp), 1)
    lower = jnp.where(er > ec, 1.0, 0.0).astype(BF16)
    selb = sel.astype(BF16)
    slot = _dot(lower, selb)
    tr = lax.broadcasted_iota(I32, (tm, tm), 0)
    tc = lax.broadcasted_iota(I32, (tm, tm), 1)
    upper = jnp.where(tr < tc, 1.0, 0.0).astype(BF16)
    carry = carry_ref[:, 0:1]
    pos = _dot(selb, upper) + carry
    eid_f = eid.astype(F32)
    e_rows, w_rows, p_rows = [], [], []
    for k in range(TOP_K):
        mk = jnp.where(slot == float(k), sel, 0.0)
        e_rows.append(jnp.sum(mk * eid_f, axis=0, keepdims=True))
        w_rows.append(jnp.sum(mk * gate, axis=0, keepdims=True))
        p_rows.append(jnp.sum(mk * pos, axis=0, keepdims=True))
    eidx_ref[...] = jnp.concatenate(e_rows, axis=0).astype(I32)
    wsel_ref[...] = jnp.concatenate(w_rows, axis=0)
    pos_ref[...] = jnp.concatenate(p_rows, axis=0).astype(I32)
    new_carry = carry + jnp.sum(sel, axis=1, keepdims=True)
    carry_ref[...] = jnp.broadcast_to(new_carry, carry_ref.shape)
    cnt_ref[...] = jnp.broadcast_to(new_carry, cnt_ref.shape).astype(I32)


def _router(r, g, b, rw_hi, rw_lo, bias_col, tm):
    t, d = r.shape
    n_exp = rw_hi.shape[0]
    row = lambda i: (i, 0)
    colb = lambda i: (0, i)
    fixed = lambda i: (0, 0)
    return pl.pallas_call(
        functools.partial(_router_kernel, tm=tm, n_exp=n_exp),
        grid=(t // tm,),
        in_specs=[pl.BlockSpec((tm, d), row),
                  pl.BlockSpec((1, d), fixed),
                  pl.BlockSpec((1, d), fixed),
                  pl.BlockSpec((n_exp, d), fixed),
                  pl.BlockSpec((n_exp, d), fixed),
                  pl.BlockSpec((n_exp, 1), fixed)],
        out_specs=[pl.BlockSpec((tm, d), row),
                   pl.BlockSpec((tm, d // 2), row),
                   pl.BlockSpec((TOP_K, tm), colb),
                   pl.BlockSpec((TOP_K, tm), colb),
                   pl.BlockSpec((TOP_K, tm), colb),
                   pl.BlockSpec((n_exp, 128), fixed)],
        out_shape=[jax.ShapeDtypeStruct((t, d), F32),
                   jax.ShapeDtypeStruct((t, d // 2), U32),
                   jax.ShapeDtypeStruct((TOP_K, t), I32),
                   jax.ShapeDtypeStruct((TOP_K, t), F32),
                   jax.ShapeDtypeStruct((TOP_K, t), I32),
                   jax.ShapeDtypeStruct((n_exp, 128), I32)],
        scratch_shapes=[pltpu.VMEM((n_exp, 128), F32)],
        compiler_params=_params(1),
        name="ln1_router",
    )(r, g, b, rw_hi, rw_lo, bias_col)


def _dest_kernel(gstart_ref, eidx_ref, pos_ref, dest_ref, *, n_exp):
    eidx = eidx_ref[...]
    acc = pos_ref[...]
    for e in range(n_exp):
        acc = acc + jnp.where(eidx == e, gstart_ref[e], 0)
    dest_ref[...] = acc


def _dest_rows(gstart, eidx, pos):
    n_exp = gstart.shape[0]
    return pl.pallas_call(
        functools.partial(_dest_kernel, n_exp=n_exp),
        grid_spec=pltpu.PrefetchScalarGridSpec(
            num_scalar_prefetch=1,
            grid=(1,),
            in_specs=[pl.BlockSpec(eidx.shape, lambda i, gs: (0, 0)),
                      pl.BlockSpec(pos.shape, lambda i, gs: (0, 0))],
            out_specs=pl.BlockSpec(eidx.shape, lambda i, gs: (0, 0))),
        out_shape=jax.ShapeDtypeStruct(eidx.shape, I32),
        compiler_params=_params(1),
        name="dest_rows",
    )(gstart, eidx, pos)


def _dispatch_kernel(dest_ref, hp_ref, xs_ref, sem, *, tm):
    def issue(t, carry):
        for k in range(TOP_K):
            d = dest_ref[0, 0, t * TOP_K + k]
            pltpu.make_async_copy(hp_ref.at[pl.ds(t, 1)], xs_ref.at[pl.ds(d, 1)], sem).start(priority=k % 2)
        return carry
    lax.fori_loop(0, tm, issue, 0)
    pltpu.make_async_copy(xs_ref.at[pl.ds(0, tm * TOP_K)], xs_ref.at[pl.ds(0, tm * TOP_K)], sem).wait()


def _dispatch(dest_tiles, h1p, tm):
    t, dw = h1p.shape
    return pl.pallas_call(
        functools.partial(_dispatch_kernel, tm=tm),
        grid=(t // tm,),
        in_specs=[pl.BlockSpec((1, 1, tm * TOP_K), lambda i: (i, 0, 0), memory_space=pltpu.SMEM),
                  pl.BlockSpec((tm, dw), lambda i: (i, 0))],
        out_specs=pl.BlockSpec(memory_space=pl.ANY),
        out_shape=jax.ShapeDtypeStruct((t * TOP_K, dw), U32),
        scratch_shapes=[pltpu.SemaphoreType.DMA(())],
        compiler_params=_params(1),
        name="dispatch_rows",
    )(dest_tiles, h1p)


def _pack_rows(y):
    n = y.shape[1] // 2
    yb = y.astype(BF16)
    hi = lax.bitcast_convert_type(yb[:, 0:n].astype(F32), U32)
    lo = lax.bitcast_convert_type(yb[:, n:].astype(F32), U32)
    return hi | (lo >> 16)


def _unpack_rows_f32(xu):
    hi = lax.bitcast_convert_type(xu & jnp.uint32(0xFFFF0000), F32)
    lo = lax.bitcast_convert_type(xu << 16, F32)
    return hi, lo


def _unpack_rows(xu):
    hi, lo = _unpack_rows_f32(xu)
    return hi.astype(BF16), lo.astype(BF16)


def _expert_weights(v, ve_ref, vfe_ref, eo_ref, vne_ref, hbm_refs, f32_refs, bf16_refs, sems):
    @pl.when(vfe_ref[v] == 1)
    def _():
        slot = eo_ref[v] % 2
        e = ve_ref[v]

        def copies(expert, s):
            return [pltpu.make_async_copy(h.at[expert], f.at[s], sem.at[s])
                    for h, f, sem in zip(hbm_refs, f32_refs, sems)]

        @pl.when(v == 0)
        def _():
            for cp in copies(e, slot):
                cp.start()

        for cp in copies(e, slot):
            cp.wait()
        ne = vne_ref[v]

        @pl.when(ne >= 0)
        def _():
            for cp in copies(ne, 1 - slot):
                cp.start()

        for f, bf in zip(f32_refs, bf16_refs):
            bf[...] = f[slot].astype(BF16)


def _visit_subblocks(vt_ref, ve_ref, vft_ref, gs_ref, ge_ref, v, tm, sb, o_ref, compute):
    e = ve_ref[v]
    gs, ge = gs_ref[e], ge_ref[e]
    first = vft_ref[v] == 1
    width = o_ref.shape[1]
    t0 = vt_ref[v] * tm
    whole = (t0 >= gs) & (t0 + tm <= ge)

    @pl.when(whole)
    def _():
        o_ref[...] = compute(0, tm).astype(o_ref.dtype)

    for s in range(tm // sb):
        r0 = t0 + s * sb
        hit = (r0 < ge) & (r0 + sb > gs)
        rows = pl.ds(s * sb, sb)

        @pl.when(hit & jnp.logical_not(whole))
        def _():
            y = compute(s * sb, sb).astype(o_ref.dtype)
            owned = (r0 >= gs) & (r0 + sb <= ge)

            @pl.when(owned)
            def _():
                o_ref[rows, :] = y

            @pl.when(jnp.logical_not(owned))
            def _():
                ridx = r0 + lax.broadcasted_iota(I32, (sb, width), 0)
                mask = (ridx >= gs) & (ridx < ge)

                @pl.when(first)
                def _():
                    o_ref[rows, :] = jnp.where(mask, y, jnp.zeros_like(y))

                @pl.when(jnp.logical_not(first))
                def _():
                    o_ref[rows, :] = jnp.where(mask, y, o_ref[rows, :])

        @pl.when(jnp.logical_not(hit) & first)
        def _():
            o_ref[rows, :] = jnp.zeros((sb, width), o_ref.dtype)


def _gmm_up_kernel(vt_ref, ve_ref, vft_ref, vfe_ref, eo_ref, vne_ref, gs_ref, ge_ref, nv_ref,
                   xs_ref, wg_hbm, wu_hbm, o_ref, wgf_ref, wuf_ref, wgb_ref, wub_ref, semg, semu, *, tm, sb):
    v = pl.program_id(0)

    @pl.when(v < nv_ref[0])
    def _():
        _expert_weights(v, ve_ref, vfe_ref, eo_ref, vne_ref, (wg_hbm, wu_hbm), (wgf_ref, wuf_ref),
                        (wgb_ref, wub_ref), (semg, semu))

        def compute(start, size):
            hi, lo = _unpack_rows(xs_ref[start:start + size, :])
            kh = hi.shape[1]
            hg = _dot(hi, wgb_ref[0:kh, :]) + _dot(lo, wgb_ref[kh:, :])
            hu = _dot(hi, wub_ref[0:kh, :]) + _dot(lo, wub_ref[kh:, :])
            return _silu(hg) * hu

        _visit_subblocks(vt_ref, ve_ref, vft_ref, gs_ref, ge_ref, v, tm, sb, o_ref, compute)


def _gmm_down_kernel(vt_ref, ve_ref, vft_ref, vfe_ref, eo_ref, vne_ref, gs_ref, ge_ref, nv_ref,
                     a_ref, wd_hbm, o_ref, wdf_ref, wdb_ref, semd, *, tm, sb):
    v = pl.program_id(0)

    @pl.when(v < nv_ref[0])
    def _():
        _expert_weights(v, ve_ref, vfe_ref, eo_ref, vne_ref, (wd_hbm,), (wdf_ref,), (wdb_ref,), (semd,))

        def compute(start, size):
            return _pack_rows(_dot(a_ref[start:start + size, :], wdb_ref[...]))

        _visit_subblocks(vt_ref, ve_ref, vft_ref, gs_ref, ge_ref, v, tm, sb, o_ref, compute)


def _visit_schedule(counts, n_rows, tm):
    n_exp = counts.shape[0]
    n_tiles = n_rows // tm
    n_visits = n_tiles + n_exp - 1
    ends = jnp.cumsum(counts)
    starts = ends - counts
    first_tile = starts // tm
    last_tile = jnp.maximum(ends - 1, 0) // tm
    tiles_e = jnp.where(counts > 0, last_tile - first_tile + 1, 0)
    vend = jnp.cumsum(tiles_e)
    vstart = vend - tiles_e
    total = vend[-1]
    v = jnp.arange(n_visits, dtype=I32)
    vc = jnp.minimum(v, total - 1)
    ve = jnp.sum(vend[None, :] <= vc[:, None], axis=1).astype(I32)
    vt = (vc - vstart[ve] + first_tile[ve]).astype(I32)
    prev_t = jnp.concatenate([jnp.full((1,), -1, I32), vt[:-1]])
    prev_e = jnp.concatenate([jnp.full((1,), -1, I32), ve[:-1]])
    vft = (vt != prev_t).astype(I32)
    vfe = (ve != prev_e).astype(I32)
    eo = (jnp.cumsum(vfe) - 1).astype(I32)
    eidx = jnp.arange(n_exp, dtype=I32)
    later = lax.cummin(jnp.where(counts > 0, eidx, n_exp), axis=0, reverse=True)
    nxt = jnp.concatenate([later[1:], jnp.full((1,), n_exp, I32)])
    vne = jnp.where(nxt < n_exp, nxt, -1)[ve].astype(I32)
    sched = (vt, ve, vft, vfe, eo, vne, starts.astype(I32), ends.astype(I32), total.reshape(1).astype(I32))
    return sched, n_visits


def _gmm_up(sched, n_visits, xs, wg, wu, tm, sb):
    n, dw = xs.shape
    n_exp, d, f = wg.shape
    return pl.pallas_call(
        functools.partial(_gmm_up_kernel, tm=tm, sb=sb),
        grid_spec=pltpu.PrefetchScalarGridSpec(
            num_scalar_prefetch=len(sched),
            grid=(n_visits,),
            in_specs=[pl.BlockSpec((tm, dw), lambda v, vt, *_: (vt[v], 0)),
                      pl.BlockSpec(memory_space=pl.ANY),
                      pl.BlockSpec(memory_space=pl.ANY)],
            out_specs=pl.BlockSpec((tm, f), lambda v, vt, *_: (vt[v], 0)),
            scratch_shapes=[pltpu.VMEM((2, d, f), F32), pltpu.VMEM((2, d, f), F32),
                            pltpu.VMEM((d, f), BF16), pltpu.VMEM((d, f), BF16),
                            pltpu.SemaphoreType.DMA((2,)), pltpu.SemaphoreType.DMA((2,))]),
        out_shape=jax.ShapeDtypeStruct((n, f), BF16),
        compiler_params=_params(1),
        name="experts_gate_up",
    )(*sched, xs, wg, wu)


def _gmm_down(sched, n_visits, act, wd, tm, sb):
    n, f = act.shape
    n_exp, _, d = wd.shape
    return pl.pallas_call(
        functools.partial(_gmm_down_kernel, tm=tm, sb=sb),
        grid_spec=pltpu.PrefetchScalarGridSpec(
            num_scalar_prefetch=len(sched),
            grid=(n_visits,),
            in_specs=[pl.BlockSpec((tm, f), lambda v, vt, *_: (vt[v], 0)),
                      pl.BlockSpec(memory_space=pl.ANY)],
            out_specs=pl.BlockSpec((tm, d // 2), lambda v, vt, *_: (vt[v], 0)),
            scratch_shapes=[pltpu.VMEM((2, f, d), F32), pltpu.VMEM((f, d), BF16),
                            pltpu.SemaphoreType.DMA((2,))]),
        out_shape=jax.ShapeDtypeStruct((n, d // 2), U32),
        compiler_params=_params(1),
        name="experts_down",
    )(*sched, act, wd)


def _combine_kernel(dcur_ref, dnxt_ref, w_ref, h1_ref, as_ref, wsd_ref, g_ref, b_ref, ys_ref, o_ref,
                    gbuf0_ref, gbuf1_ref, ybuf_ref, sem, *, tm, alpha):
    i = pl.program_id(0)
    n = pl.num_programs(0)
    groups = tm // SUBLANES
    half = o_ref.shape[1] // 2

    def issue_token(dref, buf_ref, s, t):
        for k in range(TOP_K):
            src = dref[0, 0, t * TOP_K + k]
            pltpu.make_async_copy(ys_ref.at[pl.ds(src, 1)], buf_ref.at[k, pl.ds(t, 1)],
                                  sem.at[s]).start(priority=k % 2)

    def wait_buf(buf_ref, s):
        for k in range(TOP_K):
            pltpu.make_async_copy(ys_ref.at[pl.ds(0, tm)], buf_ref.at[k], sem.at[s]).wait()

    @pl.when(i == 0)
    def _():
        def first(t, carry):
            issue_token(dcur_ref, gbuf0_ref, 0, t)
            return carry
        lax.fori_loop(0, tm, first, 0)

    ybuf_ref[...] = alpha * h1_ref[...] + _dot(as_ref[...], wsd_ref[...])

    def reduce_tile(cur_ref, cur_s, nxt_ref, nxt_s):
        wait_buf(cur_ref, cur_s)

        def reduce_group(r, carry):
            r8 = pl.multiple_of(r * SUBLANES, SUBLANES)
            rows = pl.ds(r8, SUBLANES)
            y_hi = ybuf_ref[rows, 0:half]
            y_lo = ybuf_ref[rows, half:]
            w = w_ref[rows, :]
            for j in range(SUBLANES):
                issue_token(dnxt_ref, nxt_ref, nxt_s, r8 + j)
                for k in range(j * TOP_K // SUBLANES, (j + 1) * TOP_K // SUBLANES):
                    e_hi, e_lo = _unpack_rows_f32(cur_ref[k, rows, :])
                    wk = w[:, k:k + 1]
                    y_hi = y_hi + wk * e_hi
                    y_lo = y_lo + wk * e_lo
            ybuf_ref[rows, 0:half] = y_hi
            ybuf_ref[rows, half:] = y_lo
            return carry
        lax.fori_loop(0, groups, reduce_group, 0)

        @pl.when(i == n - 1)
        def _():
            wait_buf(nxt_ref, nxt_s)

    @pl.when(i % 2 == 0)
    def _():
        reduce_tile(gbuf0_ref, 0, gbuf1_ref, 1)

    @pl.when(i % 2 == 1)
    def _():
        reduce_tile(gbuf1_ref, 1, gbuf0_ref, 0)

    o_ref[...] = _ln_rows(ybuf_ref[...], g_ref[...], b_ref[...])


def _combine(dest_tiles, w_tok, h1, act_s, wsd, g, b, ys, tm, alpha):
    t, d = h1.shape
    f = act_s.shape[1]
    n = t // tm
    row = lambda i: (i, 0)
    fixed = lambda i: (0, 0)
    return pl.pallas_call(
        functools.partial(_combine_kernel, tm=tm, alpha=alpha),
        grid=(n,),
        in_specs=[pl.BlockSpec((1, 1, tm * TOP_K), lambda i: (i, 0, 0), memory_space=pltpu.SMEM),
                  pl.BlockSpec((1, 1, tm * TOP_K), lambda i: (jnp.minimum(i + 1, n - 1), 0, 0),
                               memory_space=pltpu.SMEM),
                  pl.BlockSpec((tm, TOP_K), row),
                  pl.BlockSpec((tm, d), row),
                  pl.BlockSpec((tm, f), row),
                  pl.BlockSpec((f, d), fixed),
                  pl.BlockSpec((1, d), fixed),
                  pl.BlockSpec((1, d), fixed),
                  pl.BlockSpec(memory_space=pl.ANY)],
        out_specs=pl.BlockSpec((tm, d), row),
        out_shape=jax.ShapeDtypeStruct((t, d), F32),
        scratch_shapes=[pltpu.VMEM((TOP_K, tm, d // 2), U32),
                        pltpu.VMEM((TOP_K, tm, d // 2), U32),
                        pltpu.VMEM((tm, d), F32),
                        pltpu.SemaphoreType.DMA((2,))],
        compiler_params=_params(1),
        name="combine_ln2",
    )(dest_tiles, dest_tiles, w_tok, h1, act_s, wsd, g, b, ys)


def _tile(n, pref):
    t = min(n, pref)
    assert n % t == 0, (n, pref)
    return t


def kernel(x, meta_tokens, emb_ln_g, emb_ln_b, w_in, conv_w, conv_b, conv_ln_g, conv_ln_b, short_conv_w, a_log, dt_bias, gdn_norm_g, w_out, ln1_g, ln1_b, router_w, router_bias, expert_w_gate, expert_w_up, expert_w_down, shared_w_gate, shared_w_up, shared_w_down, ln2_g, ln2_b):
    depth = w_in.shape[0]
    assert depth == 1 and x.shape[0] == 1, "single layer, single sequence"
    t, d = x.shape[1], x.shape[2]
    cw = conv_w.shape[2]
    n_heads = a_log.shape[1]
    gw = n_heads * HEAD_DIM
    alpha = (2.0 * depth) ** 0.25
    assert meta_tokens.shape[0] == N_META and t % CHUNK == 0 and w_in.shape[2] == 2 * cw + 4 * gw + 2 * n_heads
    row = lambda a: a.reshape(1, -1)

    h_b, h_stats = _embed_ln(x[0], row(emb_ln_g), row(emb_ln_b), _tile(t, 256), "embed_ln")
    hm_b, _ = _embed_ln(meta_tokens, row(emb_ln_g), row(emb_ln_b), N_META, "embed_ln_meta")

    w_t = jnp.transpose(w_in[0])
    tm = _tile(t, 1024)
    zeros_h = jnp.zeros((n_heads,), F32)
    alog_row = row(jnp.concatenate([zeros_h, a_log[0]]))
    dtb_row = row(jnp.concatenate([zeros_h, dt_bias[0]]))
    c_x, c_m = _inproj_glu(h_b, hm_b, w_t, 0, cw, cw, tm, _tile(cw, 256))
    qkv_x, qkv_m = _inproj_conv(h_b, hm_b, w_t, short_conv_w[0], 2 * cw, 3 * gw, tm, _tile(3 * gw, 512))
    z_x, z_m = _inproj_plain(h_b, hm_b, w_t, 2 * cw + 3 * gw, gw, tm, _tile(gw, 512))
    bg_x, bg_m = _inproj_decay(h_b, hm_b, w_t[2 * cw + 4 * gw:], alog_row, dtb_row, tm, n_heads)

    y_conv = _conv_module(c_x, c_m, conv_w[0], row(conv_b[0]), row(conv_ln_g[0]), row(conv_ln_b[0]),
                          _tile(t, 256))

    front = lambda a: jnp.pad(a, ((CHUNK - N_META, 0), (0, 0)))
    bg_all = jnp.concatenate([front(bg_m), bg_x], axis=0)
    gt3 = bg_all[:, n_heads:].reshape(-1, CHUNK, n_heads).transpose(0, 2, 1)
    y_gdn = _gdn(qkv_x, front(qkv_m), z_x, front(z_m), bg_all, gt3, row(gdn_norm_g[0]), n_heads)

    r1 = _proj_residual(y_conv, y_gdn, w_out[0], x[0], h_stats, row(emb_ln_g), row(emb_ln_b), alpha, tm,
                        _tile(d, 512))
    rwt = router_w[0].T
    rw_hi = rwt.astype(BF16)
    rw_lo = (rwt - rw_hi.astype(F32)).astype(BF16)
    h1, h1_p, eidx, wsel, pos, cnt = _router(
        r1, row(ln1_g[0]), row(ln1_b[0]), rw_hi, rw_lo, router_bias[0].reshape(-1, 1), _tile(t, 256))
    counts = cnt[:, 0]
    gstart = jnp.cumsum(counts) - counts
    dest = _dest_rows(gstart.astype(I32), eidx, pos)

    tmd = _tile(t, 1024)
    xs = _dispatch(dest.T.reshape(t // tmd, 1, tmd * TOP_K), h1_p, tmd)
    tmg = _tile(t * TOP_K, 512)
    sbg = _tile(tmg, 128)
    sched, n_visits = _visit_schedule(counts, t * TOP_K, tmg)
    act = _gmm_up(sched, n_visits, xs, expert_w_gate[0], expert_w_up[0], tmg, sbg)
    ys = _gmm_down(sched, n_visits, act, expert_w_down[0], tmg, _tile(tmg, 256))

    act_s = _swiglu(h1_p, shared_w_gate[0], shared_w_up[0], tm, _tile(shared_w_gate.shape[2], 256), BF16)
    tmc = _tile(t, 128)
    out = _combine(dest.T.reshape(t // tmc, 1, tmc * TOP_K), wsel.T, h1, act_s, shared_w_down[0].astype(BF16),
                   row(ln2_g[0]), row(ln2_b[0]), ys, tmc, alpha)
    return out[None]
```
